```python
import jax
import jax.numpy as jnp
from jax import lax
import numpy as np

D_MODEL = 1024
BATCH = 2
SEQ = 16384
DEPTH = 4

GRID_W = 64
CTX_LEN = 256
HEAD_DIM = 64
F_DIM = D_MODEL // 4
F_GROUPS = F_DIM // HEAD_DIM
RW_DIM = 3 * D_MODEL // 8
RW_HEADS = RW_DIM // HEAD_DIM
NA_DIM = D_MODEL - F_DIM - RW_DIM
NA_HEADS = NA_DIM // HEAD_DIM
DECAY_LORA = 64
ICLR_LORA = 64
GATE_LORA = 128
VRES_LORA = 32
WIN_ROWS = 8
WIN_COLS = 16
N_EXPERTS = 32
TOP_K = 4
D_EXPERT = D_MODEL
SWIGLU_LIMIT = 7.0
SWIGLU_ALPHA = 1.702
MOE_BLOCK = 512
DN_ALPHA = (2 * DEPTH) ** 0.25
DN_BETA = (8 * DEPTH) ** -0.25
LN_EPS = 1e-5
GN_EPS = HEAD_DIM * 1e-5
COL_SIZES = (F_DIM, RW_DIM, RW_DIM, RW_DIM, DECAY_LORA, DECAY_LORA, ICLR_LORA, ICLR_LORA,
             GATE_LORA, GATE_LORA, NA_DIM, NA_DIM, NA_DIM)
IN_DIM = sum(COL_SIZES)

kernel_name = 'hybrid_fourier_rwkv7_natten_moe_dit'


def layer_norm(x, g, b):
    xf = x.astype(jnp.float32)
    mu = jnp.mean(xf, -1, keepdims=True)
    var = jnp.mean(jnp.square(xf - mu), -1, keepdims=True)
    return ((xf - mu) * lax.rsqrt(var + LN_EPS) * g + b).astype(x.dtype)


def ada_mod(cond, w, b):
    return jnp.split(jax.nn.silu(cond) @ w + b, 6, axis=-1)


def split_cols(p):
    parts, idx = [], 0
    for n in COL_SIZES:
        parts.append(p[..., idx:idx + n])
        idx += n
    parts.append(p[..., idx:])
    return parts


def to_heads(z):
    b, t, _ = z.shape
    return z.reshape(b, t, -1, HEAD_DIM).transpose(0, 2, 1, 3)


def from_heads(z):
    b, h, t, dh = z.shape
    return z.transpose(0, 2, 1, 3).reshape(b, t, h * dh)


def fourier_mix(p, w_map):
    b, t, _ = p.shape
    pg = p.astype(jnp.float32).reshape(b, t, F_GROUPS, HEAD_DIM)
    f = jnp.fft.fft2(pg, axes=(1, 3), norm='ortho').real
    return f.reshape(b, t, F_DIM).astype(p.dtype) @ w_map


def centred_shift(p, mu_prev, mu_next):
    prev = jnp.pad(p[:, :-1], ((0, 0), (1, 0), (0, 0)))
    nxt = jnp.pad(p[:, 1:], ((0, 0), (0, 1), (0, 0)))
    return p + mu_prev * (prev - p) + mu_next * (nxt - p)


def rwkv_prepare(r, k, v, d_v, v_first, ts_prev, ts_next, k_k, v0, v2):
    rkv = centred_shift(jnp.concatenate([r, k, v], -1), ts_prev, ts_next)
    r, k, v = jnp.split(rkv, 3, axis=-1)
    if v_first is None:
        v_first = v
    else:
        v = v + (v_first - v) * jax.nn.sigmoid(v0 + d_v @ v2)
    b, t, _ = k.shape
    kk = (k * k_k).reshape(b, t, RW_HEADS, HEAD_DIM).astype(jnp.float32)
    kk = kk * lax.rsqrt(jnp.maximum(jnp.sum(kk * kk, -1, keepdims=True), 1e-24))
    return r, k, v, kk.reshape(b, t, RW_DIM), v_first


def rwkv_scan(decay, k, v, a_vec, b_vec, r, state0, reverse):
    def step(s, inp):
        w_t, k_t, v_t, a_t, b_t = inp[:5]
        s = (s * w_t[:, :, None, :]
             + jnp.einsum('bhvk,bhk->bhv', s, a_t)[..., None] * b_t[:, :, None, :]
             + v_t[..., None] * k_t[:, :, None, :])
        y = None if r is None else jnp.einsum('bhvk,bhk->bhv', s, inp[5])
        return s, y
    xs = (decay, k, v, a_vec, b_vec) if r is None else (decay, k, v, a_vec, b_vec, r)
    return lax.scan(step, state0, xs, reverse=reverse)


def rwkv_direction(r, k_raw, v, kk, d_w, d_a, d_g, w0, w2, a0, a2, g2, k_a, r_k, gn_g, gn_b,
                   state0, reverse, with_output):
    b, t, _ = k_raw.shape
    log_w = -jax.nn.softplus(-(w0 + jnp.tanh(d_w) @ w2)) - 0.5
    decay = jnp.exp(-jnp.exp(log_w.astype(jnp.float32)))
    iclr = jax.nn.sigmoid(a0 + d_a @ a2)
    k = k_raw * (1 + (iclr - 1) * k_a)

    def tm(z):
        return z.astype(jnp.float32).reshape(b, t, RW_HEADS, HEAD_DIM).swapaxes(0, 1)

    state, y = rwkv_scan(tm(decay), tm(k), tm(v), tm(-kk), tm(kk * iclr),
                         tm(r) if with_output else None, state0, reverse)
    if not with_output:
        return state, None
    y = y.swapaxes(0, 1)
    mu = jnp.mean(y, -1, keepdims=True)
    var = jnp.mean(jnp.square(y - mu), -1, keepdims=True)
    y = ((y - mu) * lax.rsqrt(var + GN_EPS)).reshape(b, t, RW_DIM) * gn_g + gn_b

    def heads(z):
        return z.astype(jnp.float32).reshape(b, t, RW_HEADS, HEAD_DIM)

    bonus = (jnp.sum(heads(r) * heads(k) * r_k, -1, keepdims=True) * heads(v)).reshape(b, t, RW_DIM)
    gate = jax.nn.sigmoid(d_g) @ g2
    return state, ((y + bonus) * gate).astype(k_raw.dtype)


def context_attention(q, k, v):
    s = jnp.einsum('bhqd,bhkd->bhqk', q, k).astype(jnp.float32) * (HEAD_DIM ** -0.5)
    p = jax.nn.softmax(s, axis=-1).astype(v.dtype)
    return jnp.einsum('bhqk,bhkd->bhqd', p, v)


def neighbourhood_attention(q, k, v, k_ctx, v_ctx, rpb, rows):
    b, h, t, dh = q.shape
    wr = min(WIN_ROWS, rows)
    scale = dh ** -0.5
    qg = q.reshape(b, h, rows, GRID_W, dh)
    kg = k.reshape(b, h, rows, GRID_W, dh)
    vg = v.reshape(b, h, rows, GRID_W, dh)
    col = jnp.arange(GRID_W)
    col_start = jnp.clip(col - WIN_COLS // 2, 0, GRID_W - WIN_COLS)
    col_in = (col[None, :] >= col_start[:, None]) & (col[None, :] < col_start[:, None] + WIN_COLS)
    col_rel = jnp.clip(col[None, :] - col[:, None] + WIN_COLS - 1, 0, 2 * WIN_COLS - 2)
    n_loc = wr * GRID_W

    def row_block(r):
        r0 = jnp.clip(r - wr // 2, 0, rows - wr)
        q_r = lax.dynamic_index_in_dim(qg, r, axis=2, keepdims=False)
        k_r = lax.dynamic_slice_in_dim(kg, r0, wr, axis=2)
        v_r = lax.dynamic_slice_in_dim(vg, r0, wr, axis=2)
        row_rel = r0 + jnp.arange(wr) - r + WIN_ROWS - 1
        bias = rpb[:, row_rel[None, :, None], col_rel[:, None, :]]
        s_loc = jnp.einsum('bhqd,bhrkd->bhqrk', q_r, k_r).astype(jnp.float32) * scale + bias
        s_loc = jnp.where(col_in[:, None, :], s_loc, -jnp.inf)
        s_ctx = jnp.einsum('bhqd,bhcd->bhqc', q_r, k_ctx).astype(jnp.float32) * scale
        s = jnp.concatenate([s_loc.reshape(b, h, GRID_W, n_loc), s_ctx], -1)
        p = jax.nn.softmax(s, axis=-1).astype(v.dtype)
        p_loc = p[..., :n_loc].reshape(b, h, GRID_W, wr, GRID_W)
        return (jnp.einsum('bhqrk,bhrkd->bhqd', p_loc, v_r)
                + jnp.einsum('bhqc,bhcd->bhqd', p[..., n_loc:], v_ctx))

    out = lax.map(row_block, jnp.arange(rows))
    return out.transpose(1, 2, 0, 3, 4).reshape(b, h, t, dh)


def moe_ffn(h, w_router, b_router, w_gu, b_gu, w_dn, b_dn):
    n_tok, d = h.shape
    logits = (h @ w_router + b_router).astype(jnp.float32)
    top_logit, top_e = lax.top_k(logits, TOP_K)
    gate = jax.nn.softmax(top_logit, axis=-1).reshape(-1)
    flat_e = top_e.reshape(-1)
    n_slot = n_tok * TOP_K
    order = jnp.argsort(flat_e)
    e_sorted = flat_e[order]
    counts = jnp.bincount(flat_e, length=N_EXPERTS)
    padded = (counts + MOE_BLOCK - 1) // MOE_BLOCK * MOE_BLOCK
    pad_end = jnp.cumsum(padded)
    pad_start = pad_end - padded
    first = jnp.cumsum(counts) - counts
    dest = pad_start[e_sorted] + jnp.arange(n_slot) - first[e_sorted]
    n_blocks = -(-n_slot // MOE_BLOCK) + N_EXPERTS
    cap = n_blocks * MOE_BLOCK
    slot_tok = jnp.zeros((cap,), jnp.int32).at[dest].set((order // TOP_K).astype(jnp.int32))
    slot_gate = jnp.zeros((cap,), jnp.float32).at[dest].set(gate[order])
    block_e = jnp.minimum(jnp.searchsorted(pad_end, jnp.arange(n_blocks) * MOE_BLOCK, side='right'),
                          N_EXPERTS - 1)

    def expert_block(args):
        tok, e = args
        gu = h[tok] @ w_gu[e] + b_gu[e]
        g, u = jnp.split(gu, 2, axis=-1)
        g = jnp.minimum(g, SWIGLU_LIMIT)
        u = jnp.clip(u, -SWIGLU_LIMIT, SWIGLU_LIMIT)
        return ((u + 1) * g * jax.nn.sigmoid(SWIGLU_ALPHA * g)) @ w_dn[e] + b_dn[e]

    y = lax.map(expert_block, (slot_tok.reshape(n_blocks, MOE_BLOCK), block_e))
    y = y.reshape(cap, d) * slot_gate[:, None].astype(y.dtype)
    return jnp.zeros_like(h).at[slot_tok].add(y.astype(h.dtype))


def setup_inputs(seed: int = 0) -> dict:
    key = jax.random.key(seed)
    keys = iter(jax.random.split(key, 64))

    def normal(shape, std):
        return jax.random.normal(next(keys), shape, jnp.float32) * std

    def uniform(shape, lo, hi):
        return jax.random.uniform(next(keys), shape, jnp.float32, lo, hi)

    L, D = DEPTH, D_MODEL
    return {
        'x': normal((BATCH, SEQ, D), 1.0),
        'c': normal((BATCH, D), 1.0),
        'ctx': normal((BATCH, CTX_LEN, D), 1.0),
        'c_ctx': normal((D,), 1.0),
        'w_ada': normal((L, D, 6 * D), 0.5 * D ** -0.5),
        'b_ada': normal((L, 6 * D), 0.02),
        'w_in': normal((L, D, IN_DIM), D ** -0.5),
        'w_vdown': normal((L - 1, D, VRES_LORA), D ** -0.5),
        'w_fmap': normal((L, F_DIM, F_DIM), F_DIM ** -0.5),
        'ts_prev': uniform((L, 3 * RW_DIM), 0.0, 0.5),
        'ts_next': uniform((L, 3 * RW_DIM), 0.0, 0.5),
        'rw_w0': uniform((L, 2, RW_DIM), -6.0, -1.0),
        'rw_w2': normal((L, 2, DECAY_LORA, RW_DIM), 0.5 * DECAY_LORA ** -0.5),
        'rw_a0': normal((L, 2, RW_DIM), 0.1),
        'rw_a2': normal((L, 2, ICLR_LORA, RW_DIM), 0.5 * ICLR_LORA ** -0.5),
        'rw_g2': normal((L, 2, GATE_LORA, RW_DIM), GATE_LORA ** -0.5),
        'rw_kk': 0.85 + normal((L, RW_DIM), 0.05),
        'rw_ka': 1.0 + normal((L, RW_DIM), 0.05),
        'rw_rk': normal((L, RW_HEADS, HEAD_DIM), 0.1),
        'rw_gn_g': 1.0 + normal((L, RW_DIM), 0.05),
        'rw_gn_b': normal((L, RW_DIM), 0.02),
        'rw_v0': normal((L - 1, RW_DIM), 0.1),
        'rw_v2': normal((L - 1, VRES_LORA, RW_DIM), 0.5 * VRES_LORA ** -0.5),
        'na_rpb': normal((L, NA_HEADS, 2 * WIN_ROWS - 1, 2 * WIN_COLS - 1), 0.1),
        'w_out': normal((L, D, D), DN_BETA * D ** -0.5),
        'ln1_g': 1.0 + normal((L, D), 0.05),
        'ln1_b': normal((L, D), 0.02),
        'w_router': normal((L, D, N_EXPERTS), D ** -0.5),
        'b_router': normal((L, N_EXPERTS), 0.01),
        'w_gu': normal((L, N_EXPERTS, D, 2 * D_EXPERT), D ** -0.5),
        'b_gu': normal((L, N_EXPERTS, 2 * D_EXPERT), 0.01),
        'w_dn': normal((L, N_EXPERTS, D_EXPERT, D), DN_BETA * D_EXPERT ** -0.5),
        'b_dn': normal((L, N_EXPERTS, D), 0.01),
        'ln2_g': 1.0 + normal((L, D), 0.05),
        'ln2_b': normal((L, D), 0.02),
    }


def reference(x, c, ctx, c_ctx, w_ada, b_ada, w_in, w_vdown, w_fmap, ts_prev, ts_next,
              rw_w0, rw_w2, rw_a0, rw_a2, rw_g2, rw_kk, rw_ka, rw_rk, rw_gn_g, rw_gn_b,
              rw_v0, rw_v2, na_rpb, w_out, ln1_g, ln1_b, w_router, b_router,
              w_gu, b_gu, w_dn, b_dn, ln2_g, ln2_b):
    bsz, n_lat, d = x.shape
    rows = n_lat // GRID_W
    xl, xc = x, ctx
    vf_l = vf_c = None
    for l in range(DEPTH):
        last = l == DEPTH - 1
        sh1, sc1, ga1, sh2, sc2, ga2 = ada_mod(c[:, None, :], w_ada[l], b_ada[l])
        sh1c, sc1c, ga1c, sh2c, sc2c, ga2c = ada_mod(c_ctx, w_ada[l], b_ada[l])

        w_comb = w_in[l] if l == 0 else jnp.concatenate([w_in[l], w_vdown[l - 1]], axis=1)
        pl = split_cols((xl * (1 + sc1) + sh1) @ w_comb)
        pc = split_cols((xc * (1 + sc1c) + sh1c) @ w_comb)

        v0 = None if l == 0 else rw_v0[l - 1]
        v2 = None if l == 0 else rw_v2[l - 1]
        rc, kc, vc, kkc, vf_c = rwkv_prepare(pc[1], pc[2], pc[3], pc[13], vf_c, ts_prev[l], ts_next[l],
                                             rw_kk[l], v0, v2)
        rl, kl, vl, kkl, vf_l = rwkv_prepare(pl[1], pl[2], pl[3], pl[13], vf_l, ts_prev[l], ts_next[l],
                                             rw_kk[l], v0, v2)
        rw_out_c, rw_out_l = [], []
        for dr in range(2):
            prm = (rw_w0[l, dr], rw_w2[l, dr], rw_a0[l, dr], rw_a2[l, dr], rw_g2[l, dr],
                   rw_ka[l], rw_rk[l], rw_gn_g[l], rw_gn_b[l])
            s0 = jnp.zeros((bsz, RW_HEADS, HEAD_DIM, HEAD_DIM), jnp.float32)
            s_ctx, y_c = rwkv_direction(rc, kc, vc, kkc, pc[4 + dr], pc[6 + dr], pc[8 + dr], *prm,
                                        s0, dr == 1, not last)
            _, y_l = rwkv_direction(rl, kl, vl, kkl, pl[4 + dr], pl[6 + dr], pl[8 + dr], *prm,
                                    s_ctx, dr == 1, True)
            rw_out_c.append(y_c)
            rw_out_l.append(y_l)

        k_ctx, v_ctx = to_heads(pc[11]), to_heads(pc[12])
        att_l = from_heads(neighbourhood_attention(to_heads(pl[10]), to_heads(pl[11]), to_heads(pl[12]),
                                                   k_ctx, v_ctx, na_rpb[l], rows))
        mix_l = jnp.concatenate([fourier_mix(pl[0], w_fmap[l]), rw_out_l[0] + rw_out_l[1], att_l], -1) @ w_out[l]
        if not last:
            att_c = from_heads(context_attention(to_heads(pc[10]), k_ctx, v_ctx))
            mix_c = jnp.concatenate([fourier_mix(pc[0], w_fmap[l]), rw_out_c[0] + rw_out_c[1], att_c],
                                    -1) @ w_out[l]
            xc = layer_norm(DN_ALPHA * xc + ga1c * mix_c, ln1_g[l], ln1_b[l])
        xl = layer_norm(DN_ALPHA * xl + ga1 * mix_l, ln1_g[l], ln1_b[l])

        hl = (xl * (1 + sc2) + sh2).reshape(-1, d)
        n_l = hl.shape[0]
        if last:
            y = moe_ffn(hl, w_router[l], b_router[l], w_gu[l], b_gu[l], w_dn[l], b_dn[l])
            xl = layer_norm(DN_ALPHA * xl + ga2 * y.reshape(xl.shape), ln2_g[l], ln2_b[l])
        else:
            hc = (xc * (1 + sc2c) + sh2c).reshape(-1, d)
            y = moe_ffn(jnp.concatenate([hl, hc], 0), w_router[l], b_router[l], w_gu[l], b_gu[l],
                        w_dn[l], b_dn[l])
            xl = layer_norm(DN_ALPHA * xl + ga2 * y[:n_l].reshape(xl.shape), ln2_g[l], ln2_b[l])
            xc = layer_norm(DN_ALPHA * xc + ga2c * y[n_l:].reshape(xc.shape), ln2_g[l], ln2_b[l])
    return xl
```

```python
import functools

import numpy as np
import jax
import jax.numpy as jnp
from jax import lax
from jax.experimental import pallas as pl
from jax.experimental.pallas import tpu as pltpu

F32 = jnp.float32
BF16 = jnp.bfloat16
HI = lax.Precision.HIGHEST

GRID_W = 64
HEAD_DIM = 64
WIN_ROWS = 8
WIN_COLS = 16
N_EXPERTS = 32
TOP_K = 4
SWIGLU_LIMIT = 7.0
SWIGLU_ALPHA = 1.702
LN_EPS = 1e-5
GN_EPS = HEAD_DIM * 1e-5
DECAY_LORA = 64
ICLR_LORA = 64
GATE_LORA = 128
VRES_LORA = 32

LANES = 128
SUBLANES = 8
VMEM_LIMIT = 56 * 1024 * 1024

CHUNK = 64
NEG_BIG = -1e30


def _cp(sem, vmem=VMEM_LIMIT):
    return pltpu.CompilerParams(dimension_semantics=sem, vmem_limit_bytes=vmem)


def _dot(a, b, prec=HI):
    return jnp.dot(a, b, precision=prec, preferred_element_type=F32)


def _dot_nt(a, b, prec=HI):
    return lax.dot_general(a, b, (((1,), (1,)), ((), ())), precision=prec,
                           preferred_element_type=F32)


def _bdot(a, b):
    return jnp.dot(a.astype(BF16), b.astype(BF16), preferred_element_type=F32)


def _bdot_nt(a, b):
    return lax.dot_general(a.astype(BF16), b.astype(BF16), (((1,), (1,)), ((), ())),
                           preferred_element_type=F32)


def _sigmoid(x):
    return 1.0 / (1.0 + jnp.exp(-x))


def _ada_kernel(c_ref, w_ref, b_ref, o_ref):
    c = c_ref[...]
    o_ref[0] = _dot(c * _sigmoid(c), w_ref[0]) + b_ref[0]


def _ada_all(cond, w_ada, b_ada):
    n_layer, d, n = w_ada.shape
    tn = 1536
    return pl.pallas_call(
        _ada_kernel,
        out_shape=jax.ShapeDtypeStruct((n_layer, SUBLANES, n), F32),
        grid=(n_layer, n // tn),
        in_specs=[pl.BlockSpec((SUBLANES, d), lambda l, j: (0, 0)),
                  pl.BlockSpec((1, d, tn), lambda l, j: (l, 0, j)),
                  pl.BlockSpec((1, 1, tn), lambda l, j: (l, 0, j))],
        out_specs=pl.BlockSpec((1, SUBLANES, tn), lambda l, j: (l, 0, j)),
        compiler_params=_cp(("arbitrary", "arbitrary")),
        name="ada",
    )(cond, w_ada, b_ada.reshape(n_layer, 1, n))


def _inproj_kernel(x_ref, sc_ref, sh_ref, wf_ref, wr_ref, wl_ref, wq_ref,
                   of_ref, or_ref, ol_ref, oq_ref):
    xm = (x_ref[0] * (1.0 + sc_ref[0]) + sh_ref[0]).astype(BF16)
    of_ref[0] = jnp.dot(xm, wf_ref[...], preferred_element_type=F32)
    or_ref[0] = jnp.dot(xm, wr_ref[...], preferred_element_type=F32)
    ol_ref[0] = jnp.dot(xm, wl_ref[...], preferred_element_type=F32)
    oq_ref[0] = jnp.dot(xm, wq_ref[...], preferred_element_type=F32)


def _inproj(x, sc, sh, wf, wr, wl, wq, tm):
    g, m, d = x.shape
    ws = (wf, wr, wl, wq)
    mod = pl.BlockSpec((1, 1, d), lambda b, i: (b, 0, 0))
    return pl.pallas_call(
        _inproj_kernel,
        out_shape=[jax.ShapeDtypeStruct((g, m, w.shape[1]), F32) for w in ws],
        grid=(g, m // tm),
        in_specs=[pl.BlockSpec((1, tm, d), lambda b, i: (b, i, 0)), mod, mod]
                 + [pl.BlockSpec(w.shape, lambda b, i: (0, 0)) for w in ws],
        out_specs=[pl.BlockSpec((1, tm, w.shape[1]), lambda b, i: (b, i, 0)) for w in ws],
        compiler_params=_cp(("parallel", "parallel")),
        name="inproj",
    )(x, sc, sh, *ws)


def _dft_consts(t_len, n1, n2, tb2, width):
    groups = width // HEAD_DIM
    j = np.arange(HEAD_DIM)
    ang = 2.0 * np.pi * np.outer(j, j) / HEAD_DIM
    eye = np.eye(groups)
    cbd = np.kron(eye, np.cos(ang)) / np.sqrt(HEAD_DIM)
    sbd = np.kron(eye, np.sin(ang)) / np.sqrt(HEAD_DIM)
    a1 = 2.0 * np.pi * np.outer(np.arange(n1), np.arange(n1)) / n1
    c1, s1 = np.cos(a1), np.sin(a1)
    atw = 2.0 * np.pi * np.outer(np.arange(n1), np.arange(n2)) / t_len
    twc = np.cos(atw).reshape(n1, n2 // tb2, tb2).transpose(1, 0, 2)
    tws = np.sin(atw).reshape(n1, n2 // tb2, tb2).transpose(1, 0, 2)
    a2 = 2.0 * np.pi * np.outer(np.arange(n2), np.arange(n2)) / n2
    c2, s2 = np.cos(a2) / np.sqrt(t_len), np.sin(a2) / np.sqrt(t_len)
    return [jnp.asarray(v, F32) for v in (cbd, sbd, c1, s1, twc, tws, c2, s2)]


def _four_a_kernel(tb2, width, x_ref, cbd_ref, sbd_ref, c1_ref, s1_ref, twc_ref, tws_ref,
                   yr_ref, yi_ref):
    c1 = c1_ref[...]
    s1 = s1_ref[...]
    cbd = cbd_ref[...]
    sbd = sbd_ref[...]
    for jj in range(tb2):
        x = x_ref[0, :, width * jj:width * (jj + 1)]
        zr = _dot(x, cbd)
        zi = -_dot(x, sbd)
        ar = _dot(c1, zr) + _dot(s1, zi)
        ai = _dot(c1, zi) - _dot(s1, zr)
        tc = twc_ref[0, :, jj:jj + 1]
        ts = tws_ref[0, :, jj:jj + 1]
        yr_ref[0, jj] = tc * ar + ts * ai
        yi_ref[0, jj] = tc * ai - ts * ar


def _four_b_kernel(yr_ref, yi_ref, c2_ref, s2_ref, o_ref):
    o_ref[0] = _dot(c2_ref[...], yr_ref[0]) + _dot(s2_ref[...], yi_ref[0])


def _fourier(f, n1, n2):
    bsz, t_len, width = f.shape
    tb2 = SUBLANES
    cbd, sbd, c1, s1, twc, tws, c2, s2 = _dft_consts(t_len, n1, n2, tb2, width)
    const = lambda a: pl.BlockSpec(a.shape, lambda b, j: (0,) * a.ndim)
    yr, yi = pl.pallas_call(
        functools.partial(_four_a_kernel, tb2, width),
        out_shape=[jax.ShapeDtypeStruct((bsz, n2, n1, width), F32)] * 2,
        grid=(bsz, n2 // tb2),
        in_specs=[pl.BlockSpec((1, n1, tb2 * width), lambda b, j: (b, 0, j)),
                  const(cbd), const(sbd), const(c1), const(s1),
                  pl.BlockSpec((1, n1, tb2), lambda b, j: (j, 0, 0)),
                  pl.BlockSpec((1, n1, tb2), lambda b, j: (j, 0, 0))],
        out_specs=[pl.BlockSpec((1, tb2, n1, width), lambda b, j: (b, j, 0, 0))] * 2,
        compiler_params=_cp(("parallel", "parallel")),
        name="fourier_a",
    )(f.reshape(bsz, n1, n2 * width), cbd, sbd, c1, s1, twc, tws)
    ncol = n1 * width
    tc = min(ncol, 2048)
    out = pl.pallas_call(
        _four_b_kernel,
        out_shape=jax.ShapeDtypeStruct((bsz, n2, ncol), F32),
        grid=(bsz, ncol // tc),
        in_specs=[pl.BlockSpec((1, n2, tc), lambda b, j: (b, 0, j)),
                  pl.BlockSpec((1, n2, tc), lambda b, j: (b, 0, j)),
                  const(c2), const(s2)],
        out_specs=pl.BlockSpec((1, n2, tc), lambda b, j: (b, 0, j)),
        compiler_params=_cp(("parallel", "parallel")),
        name="fourier_b",
    )(yr.reshape(bsz, n2, ncol), yi.reshape(bsz, n2, ncol), c2, s2)
    return out.reshape(bsz, t_len, width)


def _split_len(t_len):
    n1 = 1 << ((t_len.bit_length() - 1 + 1) // 2)
    return n1, t_len // n1


def _head_ones(width):
    h = np.arange(width) // HEAD_DIM
    return jnp.asarray((h[:, None] == h[None, :]).astype(np.float32))


def _rwkv_prep_kernel(first, x_ref, xp_ref, xn_ref, lo_ref, tsp_ref, tsn_ref, kk_ref,
                      ones_ref, *rest):
    if first:
        o_ref, vf_out_ref = rest
    else:
        v0_ref, v2_ref, vf_ref, o_ref = rest
    i = pl.program_id(1)
    last = pl.num_programs(1) - 1
    x = x_ref[0]
    tb, w3 = x.shape
    rw = w3 // 3
    row = lax.broadcasted_iota(jnp.int32, (tb, 1), 0)
    prev_edge = jnp.where(i == 0, 0.0, xp_ref[0, SUBLANES - 1:SUBLANES, :])
    next_edge = jnp.where(i == last, 0.0, xn_ref[0, 0:1, :])
    prev = jnp.where(row == 0, prev_edge, pltpu.roll(x, 1, 0))
    nxt = jnp.where(row == tb - 1, next_edge, pltpu.roll(x, tb - 1, 0))
    s = x + tsp_ref[...] * (prev - x) + tsn_ref[...] * (nxt - x)
    r = s[:, 0:rw]
    k = s[:, rw:2 * rw]
    v = s[:, 2 * rw:3 * rw]
    if first:
        vf_out_ref[0] = v
    else:
        mix = _sigmoid(v0_ref[...] + _dot(lo_ref[0], v2_ref[...]))
        v = v + (vf_ref[0] - v) * mix
    kk = k * kk_ref[...]
    ss = _dot(kk * kk, ones_ref[...])
    kk = kk * lax.rsqrt(jnp.maximum(ss, 1e-24))
    o_ref[0, :, 0:rw] = r
    o_ref[0, :, rw:2 * rw] = k
    o_ref[0, :, 2 * rw:3 * rw] = v
    o_ref[0, :, 3 * rw:4 * rw] = kk


def _rwkv_prep(rkv, lora, ts_prev, ts_next, k_k, v_first, v0, v2pad, tb):
    g, m, w3 = rkv.shape
    rw = w3 // 3
    first = v_first is None
    nb8 = tb // SUBLANES
    n8 = m // SUBLANES
    ones = _head_ones(rw)
    row = lambda a: a.reshape(1, -1)
    vec = lambda n: pl.BlockSpec((1, n), lambda b, i: (0, 0))
    in_specs = [
        pl.BlockSpec((1, tb, w3), lambda b, i: (b, i, 0)),
        pl.BlockSpec((1, SUBLANES, w3), lambda b, i: (b, jnp.maximum(i * nb8 - 1, 0), 0)),
        pl.BlockSpec((1, SUBLANES, w3), lambda b, i: (b, jnp.minimum((i + 1) * nb8, n8 - 1), 0)),
        pl.BlockSpec((1, tb, LANES), lambda b, i: (b, i, 4)),
        vec(w3), vec(w3), vec(rw),
        pl.BlockSpec((rw, rw), lambda b, i: (0, 0)),
    ]
    args = [rkv, rkv, rkv, lora, row(ts_prev), row(ts_next), row(k_k), ones]
    out_main = jax.ShapeDtypeStruct((g, m, 4 * rw), F32)
    spec_main = pl.BlockSpec((1, tb, 4 * rw), lambda b, i: (b, i, 0))
    spec_v = pl.BlockSpec((1, tb, rw), lambda b, i: (b, i, 0))
    if first:
        out_shape = [out_main, jax.ShapeDtypeStruct((g, m, rw), F32)]
        out_specs = [spec_main, spec_v]
    else:
        in_specs += [vec(rw), pl.BlockSpec((LANES, rw), lambda b, i: (0, 0)), spec_v]
        args += [row(v0), v2pad, v_first]
        out_shape = out_main
        out_specs = spec_main
    res = pl.pallas_call(
        functools.partial(_rwkv_prep_kernel, first),
        out_shape=out_shape, grid=(g, m // tb), in_specs=in_specs, out_specs=out_specs,
        compiler_params=_cp(("parallel", "parallel")),
        name="rwkv_prep",
    )(*args)
    if first:
        return res[0], res[1]
    return res, v_first


def _stack_heads(x, lo_mask):
    return jnp.concatenate([jnp.where(lo_mask, x, 0.0), jnp.where(lo_mask, 0.0, x)], axis=0)


def _rwkv_scan_kernel(reverse, with_acc, n_chunk, p_ref, lo_ref, s0_ref, w0_ref, w2_ref,
                      a0_ref, a2_ref, g2_ref, ka_ref, rk_ref, gng_ref, gnb_ref, ones_ref,
                      *rest):
    if with_acc:
        acc_ref, o_ref, so_ref, s_scr, y_scr = rest
    else:
        o_ref, so_ref, s_scr, y_scr = rest
    i = pl.program_id(1)
    L = CHUNK
    L2 = 2 * L

    @pl.when(i == 0)
    def _():
        s_scr[...] = s0_ref[0]

    p = p_ref[0]
    rw = p.shape[1] // 4
    n_pair = rw // LANES
    r = p[:, 0:rw]
    k = p[:, rw:2 * rw]
    v = p[:, 2 * rw:3 * rw]
    kk = p[:, 3 * rw:4 * rw]
    lo = lo_ref[0]
    dcol = 1 if reverse else 0
    zw = w0_ref[...] + _dot(jnp.tanh(lo[:, 0:LANES]), w2_ref[...])
    log_w = -(jnp.maximum(-zw, 0.0) + jnp.log(1.0 + jnp.exp(-jnp.abs(zw)))) - 0.5
    lw = -jnp.exp(log_w)
    iclr = _sigmoid(a0_ref[...] + _dot(lo[:, LANES:2 * LANES], a2_ref[...]))
    kd = k * (1.0 + (iclr - 1.0) * ka_ref[...])
    av = -kk
    bv = kk * iclr
    gate = _dot(_sigmoid(lo[:, (2 + dcol) * LANES:(3 + dcol) * LANES]), g2_ref[...])
    ones = ones_ref[...]
    bonus = _dot(r * kd * rk_ref[...], ones) * v

    ti = lax.broadcasted_iota(jnp.int32, (L2, L2), 0)
    si = lax.broadcasted_iota(jnp.int32, (L2, L2), 1)
    same_head = (ti // L) == (si // L)
    if reverse:
        before = si > ti
    else:
        before = si < ti
    m_strict = same_head & before
    m_incl = same_head & (before | (si == ti))
    eye = jnp.where(si == ti, 1.0, 0.0)
    ci = lax.broadcasted_iota(jnp.int32, (L, L), 0)
    cj = lax.broadcasted_iota(jnp.int32, (L, L), 1)
    tri = jnp.where((cj >= ci) if reverse else (cj <= ci), 1.0, 0.0)
    lo_mask = lax.broadcasted_iota(jnp.int32, (1, LANES), 1) < HEAD_DIM
    end_row = 0 if reverse else L - 1

    order = range(n_chunk - 1, -1, -1) if reverse else range(n_chunk)
    for c in order:
        rs = slice(c * L, (c + 1) * L)
        for j in range(n_pair):
            cs = slice(j * LANES, (j + 1) * LANES)
            lwc = lw[rs, cs]
            cum = _dot(tri, lwc)
            tot = cum[end_row:end_row + 1, :]
            g_inc = jnp.exp(cum)
            g_exc = jnp.exp(cum - lwc)
            g_inv = jnp.exp(-cum)
            g_end = jnp.exp(tot - cum)
            a_s = _stack_heads(av[rs, cs] * g_exc, lo_mask)
            b_s = _stack_heads(bv[rs, cs] * g_inv, lo_mask)
            k_s = _stack_heads(kd[rs, cs] * g_inv, lo_mask)
            r_s = _stack_heads(r[rs, cs] * g_inc, lo_mask)
            v_s = _stack_heads(v[rs, cs], lo_mask)
            bh_s = _stack_heads(bv[rs, cs] * g_end, lo_mask)
            kh_s = _stack_heads(kd[rs, cs] * g_end, lo_mask)
            a_ab = jnp.where(m_strict, _dot_nt(a_s, b_s), 0.0)
            a_ak = jnp.where(m_strict, _dot_nt(a_s, k_s), 0.0)
            a_rb = jnp.where(m_incl, _dot_nt(r_s, b_s), 0.0)
            a_rk = jnp.where(m_incl, _dot_nt(r_s, k_s), 0.0)
            tinv = eye + a_ab
            apow = a_ab
            for _ in range(5):
                apow = _dot(apow, apow)
                tinv = tinv + _dot(tinv, apow)
            s_old = s_scr[j]
            u = _dot_nt(_dot(tinv, a_s), s_old) + _dot(tinv, _dot(a_ak, v_s))
            y = _dot_nt(r_s, s_old) + _dot(a_rb, u) + _dot(a_rk, v_s)
            s_scr[j] = (s_old * jnp.exp(tot) + _dot(u.T, bh_s) + _dot(v_s.T, kh_s))
            y_scr[rs, cs] = y[0:L] + y[L:L2]

    y = y_scr[...]
    mu = _dot(y, ones) * (1.0 / HEAD_DIM)
    d = y - mu
    var = _dot(d * d, ones) * (1.0 / HEAD_DIM)
    out = (d * lax.rsqrt(var + GN_EPS) * gng_ref[...] + gnb_ref[...] + bonus) * gate
    if with_acc:
        out = out + acc_ref[0]
    o_ref[0] = out

    @pl.when(i == pl.num_programs(1) - 1)
    def _():
        so_ref[0] = s_scr[...]


def _rwkv_scan(prep, lora, state0, prm, reverse, acc, tb):
    g, m, w4 = prep.shape
    rw = w4 // 4
    n_pair = rw // LANES
    nblk = m // tb
    w0, w2pad, a0, a2pad, g2, k_a, r_k, gn_g, gn_b = prm
    row = lambda a: a.reshape(1, -1)
    if reverse:
        tmap = lambda b, i: (b, nblk - 1 - i, 0)
    else:
        tmap = lambda b, i: (b, i, 0)
    vec = pl.BlockSpec((1, rw), lambda b, i: (0, 0))
    mat = pl.BlockSpec((LANES, rw), lambda b, i: (0, 0))
    st_spec = pl.BlockSpec((1, n_pair, LANES, LANES), lambda b, i: (b, 0, 0, 0))
    in_specs = [pl.BlockSpec((1, tb, w4), tmap),
                pl.BlockSpec((1, tb, lora.shape[2]), tmap),
                st_spec, vec, mat, vec, mat, mat, vec, vec, vec, vec,
                pl.BlockSpec((rw, rw), lambda b, i: (0, 0))]
    args = [prep, lora, state0, row(w0), w2pad, row(a0), a2pad, g2, row(k_a), row(r_k),
            row(gn_g), row(gn_b), _head_ones(rw)]
    if acc is not None:
        in_specs.append(pl.BlockSpec((1, tb, rw), tmap))
        args.append(acc)
    y, s_out = pl.pallas_call(
        functools.partial(_rwkv_scan_kernel, reverse, acc is not None, tb // CHUNK),
        out_shape=[jax.ShapeDtypeStruct((g, m, rw), F32),
                   jax.ShapeDtypeStruct((g, n_pair, LANES, LANES), F32)],
        grid=(g, nblk), in_specs=in_specs,
        out_specs=[pl.BlockSpec((1, tb, rw), tmap), st_spec],
        scratch_shapes=[pltpu.VMEM((n_pair, LANES, LANES), F32), pltpu.VMEM((tb, rw), F32)],
        compiler_params=_cp(("arbitrary", "arbitrary")),
        name="rwkv_scan_bwd" if reverse else "rwkv_scan_fwd",
    )(*args)
    return y, s_out


def _na_bias_table(rpb):
    n_head = rpb.shape[0]
    col = np.arange(GRID_W)
    col_start = np.clip(col - WIN_COLS // 2, 0, GRID_W - WIN_COLS)
    col_in = (col[None, :] >= col_start[:, None]) & (col[None, :] < col_start[:, None] + WIN_COLS)
    col_rel = np.clip(col[None, :] - col[:, None] + WIN_COLS - 1, 0, 2 * WIN_COLS - 2)
    d = np.arange(WIN_ROWS)
    jr = np.arange(WIN_ROWS)
    row_rel = d[:, None] + jr[None, :]
    t = rpb[:, row_rel[:, :, None, None], col_rel[None, None, :, :]]
    t = jnp.where(col_in[None, None, None], t, NEG_BIG)
    t = t.transpose(0, 1, 3, 2, 4).reshape(n_head // 2, 2, WIN_ROWS, GRID_W, WIN_ROWS * GRID_W)
    return t.astype(F32)


def _na_kernel(rows, q_ref, kp_ref, kc_ref, kn_ref, vp_ref, vc_ref, vn_ref, kx_ref, vx_ref,
               b_ref, o_ref, kcat, vcat):
    i = pl.program_id(2)
    w = GRID_W
    blk = WIN_ROWS * w
    scale = HEAD_DIM ** -0.5
    kcat[0:blk] = kp_ref[0].astype(BF16)
    kcat[blk:2 * blk] = kc_ref[0].astype(BF16)
    kcat[2 * blk:3 * blk] = kn_ref[0].astype(BF16)
    vcat[0:blk] = vp_ref[0].astype(BF16)
    vcat[blk:2 * blk] = vc_ref[0].astype(BF16)
    vcat[2 * blk:3 * blk] = vn_ref[0].astype(BF16)
    kx = kx_ref[0].astype(BF16)
    vx = vx_ref[0].astype(BF16)
    lo_mask = lax.broadcasted_iota(jnp.int32, (1, LANES), 1) < HEAD_DIM
    for rr in range(WIN_ROWS):
        r = i * WIN_ROWS + rr
        r0 = jnp.clip(r - WIN_ROWS // 2, 0, rows - WIN_ROWS)
        off = pl.multiple_of((r0 - (i - 1) * WIN_ROWS) * w, w)
        dlt = r0 - r + WIN_ROWS - 1
        q = q_ref[0, rr * w:(rr + 1) * w, :]
        qs = _stack_heads(q, lo_mask).astype(BF16)
        kw = kcat[pl.ds(off, blk), :]
        vw = vcat[pl.ds(off, blk), :]
        bias = jnp.concatenate([b_ref[0, 0, dlt], b_ref[0, 1, dlt]], axis=0)
        s = _bdot_nt(qs, kw) * scale + bias
        sx = _bdot_nt(qs, kx) * scale
        mx = jnp.maximum(jnp.max(s, axis=-1, keepdims=True), jnp.max(sx, axis=-1, keepdims=True))
        p = jnp.exp(s - mx)
        px = jnp.exp(sx - mx)
        den = jnp.sum(p, axis=-1, keepdims=True) + jnp.sum(px, axis=-1, keepdims=True)
        o = (_bdot(p, vw) + _bdot(px, vx)) / den
        o_ref[0, rr * w:(rr + 1) * w, :] = jnp.where(lo_mask, o[0:w], o[w:2 * w])


def _na_attention(qkv, qkv_ctx, bias_tab):
    bsz, t_len, w3 = qkv.shape
    na = w3 // 3
    n_pair = na // LANES
    rows = t_len // GRID_W
    blk = WIN_ROWS * GRID_W
    nblk = rows // WIN_ROWS
    c_len = qkv_ctx.shape[1]
    kv = lambda sel, shift: pl.BlockSpec(
        (1, blk, LANES), lambda b, j, i: (b, jnp.clip(i + shift, 0, nblk - 1), sel * n_pair + j))
    cx = lambda sel: pl.BlockSpec((1, c_len, LANES), lambda b, j, i: (b, 0, sel * n_pair + j))
    return pl.pallas_call(
        functools.partial(_na_kernel, rows),
        out_shape=jax.ShapeDtypeStruct((bsz, t_len, na), F32),
        grid=(bsz, n_pair, nblk),
        in_specs=[kv(0, 0), kv(1, -1), kv(1, 0), kv(1, 1), kv(2, -1), kv(2, 0), kv(2, 1),
                  cx(1), cx(2),
                  pl.BlockSpec((1, 2, WIN_ROWS, GRID_W, blk), lambda b, j, i: (j, 0, 0, 0, 0))],
        out_specs=pl.BlockSpec((1, blk, LANES), lambda b, j, i: (b, i, j)),
        scratch_shapes=[pltpu.VMEM((3 * blk, LANES), BF16), pltpu.VMEM((3 * blk, LANES), BF16)],
        compiler_params=_cp(("parallel", "parallel", "arbitrary")),
        name="na_attention",
    )(qkv, qkv, qkv, qkv, qkv, qkv, qkv, qkv_ctx, qkv_ctx, bias_tab)


def _ctx_attn_kernel(q_ref, k_ref, v_ref, o_ref):
    lo_mask = lax.broadcasted_iota(jnp.int32, (1, LANES), 1) < HEAD_DIM
    q = q_ref[0]
    c = q.shape[0]
    qs = _stack_heads(q, lo_mask)
    s = _bdot_nt(qs, k_ref[0]) * (HEAD_DIM ** -0.5)
    mx = jnp.max(s, axis=-1, keepdims=True)
    p = jnp.exp(s - mx)
    o = _bdot(p, v_ref[0]) / jnp.sum(p, axis=-1, keepdims=True)
    o_ref[0] = jnp.where(lo_mask, o[0:c], o[c:2 * c])


def _ctx_attention(qkv_ctx):
    bsz, c_len, w3 = qkv_ctx.shape
    na = w3 // 3
    n_pair = na // LANES
    sp = lambda sel: pl.BlockSpec((1, c_len, LANES), lambda b, j: (b, 0, sel * n_pair + j))
    return pl.pallas_call(
        _ctx_attn_kernel,
        out_shape=jax.ShapeDtypeStruct((bsz, c_len, na), F32),
        grid=(bsz, n_pair),
        in_specs=[sp(0), sp(1), sp(2)],
        out_specs=pl.BlockSpec((1, c_len, LANES), lambda b, j: (b, 0, j)),
        compiler_params=_cp(("parallel", "parallel")),
        name="ctx_attention",
    )(qkv_ctx, qkv_ctx, qkv_ctx)


def _layer_norm(z, g, b):
    mu = jnp.mean(z, axis=-1, keepdims=True)
    d = z - mu
    var = jnp.mean(d * d, axis=-1, keepdims=True)
    return d * lax.rsqrt(var + LN_EPS) * g + b


def _outproj_kernel(alpha, f_ref, rw_ref, at_ref, x_ref, ga_ref, sc_ref, sh_ref, wfm_ref,
                    wo_ref, g_ref, b_ref, wr_ref, br_ref, xo_ref, h_ref, lg_ref):
    nf = f_ref.shape[2]
    nr = rw_ref.shape[2]
    fm = jnp.dot(f_ref[0].astype(BF16), wfm_ref[...], preferred_element_type=F32)
    mix = (jnp.dot(fm.astype(BF16), wo_ref[0:nf, :], preferred_element_type=F32)
           + jnp.dot(rw_ref[0].astype(BF16), wo_ref[nf:nf + nr, :], preferred_element_type=F32)
           + jnp.dot(at_ref[0].astype(BF16), wo_ref[nf + nr:, :], preferred_element_type=F32))
    xn = _layer_norm(alpha * x_ref[0] + ga_ref[0] * mix, g_ref[...], b_ref[...])
    xo_ref[0] = xn
    h = xn * (1.0 + sc_ref[0]) + sh_ref[0]
    h_ref[0] = h
    lg_ref[0] = _dot(h, wr_ref[...]) + br_ref[...]


def _outproj(alpha, four, rwo, att, x, ga, sc, sh, wfm, wo, ln_g, ln_b, wr_pad, br_pad, tm):
    g, m, d = x.shape
    blk = lambda a: pl.BlockSpec((1, tm, a.shape[2]), lambda b, i: (b, i, 0))
    mod = pl.BlockSpec((1, 1, d), lambda b, i: (b, 0, 0))
    cst = lambda a: pl.BlockSpec(a.shape, lambda b, i: (0, 0))
    row = lambda a: a.reshape(1, -1)
    ln_g, ln_b, br_pad = row(ln_g), row(ln_b), row(br_pad)
    return pl.pallas_call(
        functools.partial(_outproj_kernel, alpha),
        out_shape=[jax.ShapeDtypeStruct((g, m, d), F32), jax.ShapeDtypeStruct((g, m, d), F32),
                   jax.ShapeDtypeStruct((g, m, LANES), F32)],
        grid=(g, m // tm),
        in_specs=[blk(four), blk(rwo), blk(att), blk(x), mod, mod, mod, cst(wfm), cst(wo),
                  cst(ln_g), cst(ln_b), cst(wr_pad), cst(br_pad)],
        out_specs=[pl.BlockSpec((1, tm, d), lambda b, i: (b, i, 0)),
                   pl.BlockSpec((1, tm, d), lambda b, i: (b, i, 0)),
                   pl.BlockSpec((1, tm, LANES), lambda b, i: (b, i, 0))],
        compiler_params=_cp(("parallel", "parallel")),
        name="outproj_ln",
    )(four, rwo, att, x, ga, sc, sh, wfm, wo, ln_g, ln_b, wr_pad, br_pad)


def _ln2_kernel(alpha, x_ref, y_ref, ga_ref, g_ref, b_ref, o_ref):
    o_ref[0] = _layer_norm(alpha * x_ref[0] + ga_ref[0] * y_ref[0], g_ref[...], b_ref[...])


def _ln2(alpha, x, y, ga, ln_g, ln_b, tm):
    g, m, d = x.shape
    blk = pl.BlockSpec((1, tm, d), lambda b, i: (b, i, 0))
    vec = pl.BlockSpec((1, d), lambda b, i: (0, 0))
    return pl.pallas_call(
        functools.partial(_ln2_kernel, alpha),
        out_shape=jax.ShapeDtypeStruct((g, m, d), F32),
        grid=(g, m // tm),
        in_specs=[blk, blk, pl.BlockSpec((1, 1, d), lambda b, i: (b, 0, 0)), vec, vec],
        out_specs=blk,
        compiler_params=_cp(("parallel", "parallel")),
        name="ln2",
    )(x, y, ga, ln_g.reshape(1, -1), ln_b.reshape(1, -1))


def _moe_kernel(be_ref, nu_ref, x_ref, gt_ref, wgu_ref, bgu_ref, wdn_ref, bdn_ref, o_ref):
    i = pl.program_id(0)

    @pl.when(i < nu_ref[0])
    def _():
        de = wdn_ref.shape[1]
        gu = jnp.dot(x_ref[...].astype(BF16), wgu_ref[0], preferred_element_type=F32) + bgu_ref[0]
        gg = jnp.minimum(gu[:, 0:de], SWIGLU_LIMIT)
        uu = jnp.clip(gu[:, de:2 * de], -SWIGLU_LIMIT, SWIGLU_LIMIT)
        act = (uu + 1.0) * gg * _sigmoid(SWIGLU_ALPHA * gg)
        y = jnp.dot(act.astype(BF16), wdn_ref[0], preferred_element_type=F32) + bdn_ref[0]
        o_ref[...] = y * gt_ref[...]


def _moe_experts(xs, slot_gate, block_e, n_used, wgu, bgu, wdn, bdn, tm):
    cap, d = xs.shape
    n_exp, _, de2 = wgu.shape
    de = de2 // 2
    n_blocks = cap // tm
    grid_spec = pltpu.PrefetchScalarGridSpec(
        num_scalar_prefetch=2,
        grid=(n_blocks,),
        in_specs=[pl.BlockSpec((tm, d), lambda i, be, nu: (i, 0)),
                  pl.BlockSpec((tm, 1), lambda i, be, nu: (i, 0)),
                  pl.BlockSpec((1, d, de2), lambda i, be, nu: (be[i], 0, 0)),
                  pl.BlockSpec((1, 1, de2), lambda i, be, nu: (be[i], 0, 0)),
                  pl.BlockSpec((1, de, d), lambda i, be, nu: (be[i], 0, 0)),
                  pl.BlockSpec((1, 1, d), lambda i, be, nu: (be[i], 0, 0))],
        out_specs=pl.BlockSpec((tm, d), lambda i, be, nu: (i, 0)),
    )
    return pl.pallas_call(
        _moe_kernel,
        out_shape=jax.ShapeDtypeStruct((cap, d), F32),
        grid_spec=grid_spec,
        compiler_params=_cp(("arbitrary",)),
        name="moe_experts",
    )(block_e, n_used, xs, slot_gate.reshape(cap, 1), wgu, bgu.reshape(n_exp, 1, de2),
      wdn, bdn.reshape(n_exp, 1, d))


def _moe(h, logits, wgu, bgu, wdn, bdn, tm):
    n_tok, d = h.shape
    top_logit, top_e = lax.top_k(logits, TOP_K)
    gate = jax.nn.softmax(top_logit, axis=-1).reshape(-1)
    flat_e = top_e.reshape(-1).astype(jnp.int32)
    n_slot = n_tok * TOP_K
    order = jnp.argsort(flat_e)
    e_sorted = flat_e[order]
    counts = jnp.sum(flat_e[:, None] == jnp.arange(N_EXPERTS, dtype=jnp.int32)[None, :], axis=0,
                     dtype=jnp.int32)
    padded = (counts + tm - 1) // tm * tm
    pad_end = jnp.cumsum(padded)
    pad_start = pad_end - padded
    first = jnp.cumsum(counts) - counts
    dest = pad_start[e_sorted] + jnp.arange(n_slot, dtype=jnp.int32) - first[e_sorted]
    n_blocks = -(-n_slot // tm) + N_EXPERTS
    cap = n_blocks * tm
    slot_tok = jnp.zeros((cap,), jnp.int32).at[dest].set((order // TOP_K).astype(jnp.int32))
    slot_gate = jnp.zeros((cap,), F32).at[dest].set(gate[order])
    pos = jnp.zeros((n_slot,), jnp.int32).at[order].set(dest.astype(jnp.int32))
    block_e = jnp.minimum(
        jnp.searchsorted(pad_end, jnp.arange(n_blocks, dtype=jnp.int32) * tm, side='right'),
        N_EXPERTS - 1).astype(jnp.int32)
    n_used = (pad_end[-1:] // tm).astype(jnp.int32)
    xs = jnp.take(h, slot_tok, axis=0)
    ys = _moe_experts(xs, slot_gate, block_e, n_used, wgu, bgu, wdn, bdn, tm)
    return jnp.sum(jnp.take(ys, pos, axis=0).reshape(n_tok, TOP_K, d), axis=1)


def _pad_rows(w, start, total):
    return jnp.zeros((total, w.shape[1]), w.dtype).at[start:start + w.shape[0]].set(w)


def kernel(x, c, ctx, c_ctx, w_ada, b_ada, w_in, w_vdown, w_fmap, ts_prev, ts_next, rw_w0, rw_w2, rw_a0, rw_a2, rw_g2, rw_kk, rw_ka, rw_rk, rw_gn_g, rw_gn_b, rw_v0, rw_v2, na_rpb, w_out, ln1_g, ln1_b, w_router, b_router, w_gu, b_gu, w_dn, b_dn, ln2_g, ln2_b):
    bsz, n_lat, d = x.shape
    c_len = ctx.shape[1]
    depth = w_in.shape[0]
    f_dim = w_fmap.shape[1]
    rw_dim = rw_kk.shape[1]
    na_dim = d - f_dim - rw_dim
    alpha = (2 * depth) ** 0.25
    n_pair = rw_dim // LANES

    tm_lat = min(512, n_lat)
    tm_ctx = min(256, c_len)
    tb_lat = min(128, n_lat)
    tb_ctx = min(128, c_len)
    tm_moe = 512
    n1_lat, n2_lat = _split_len(n_lat)
    n1_ctx, n2_ctx = _split_len(c_len)

    cond = jnp.zeros((SUBLANES, d), F32).at[0:bsz].set(c).at[bsz].set(c_ctx)
    ada = _ada_all(cond, w_ada, b_ada)

    xl, xc = x, ctx
    vf_l = vf_c = None
    o_r = f_dim
    o_lora = f_dim + 3 * rw_dim
    o_q = o_lora + 2 * DECAY_LORA + 2 * ICLR_LORA + 2 * GATE_LORA
    for l in range(depth):
        last = l == depth - 1
        mods = ada[l].reshape(SUBLANES, 6, d)
        lat = lambda k: mods[0:bsz, k][:, None, :]
        cx = lambda k: jnp.broadcast_to(mods[bsz, k][None, None, :], (bsz, 1, d))

        wi = w_in[l]
        vdown = jnp.zeros((d, LANES), F32)
        if l > 0:
            vdown = vdown.at[:, 0:VRES_LORA].set(w_vdown[l - 1])
        wf = wi[:, 0:f_dim].astype(BF16)
        wr = wi[:, o_r:o_lora].astype(BF16)
        wl = jnp.concatenate([wi[:, o_lora:o_q], vdown], axis=1).astype(BF16)
        wq = wi[:, o_q:].astype(BF16)
        f_l, rkv_l, lora_l, qkv_l = _inproj(xl, lat(1), lat(0), wf, wr, wl, wq, tm_lat)
        f_c, rkv_c, lora_c, qkv_c = _inproj(xc, cx(1), cx(0), wf, wr, wl, wq, tm_ctx)

        if l == 0:
            v0 = v2pad = None
        else:
            v0 = rw_v0[l - 1]
            v2pad = _pad_rows(rw_v2[l - 1], 0, LANES)
        prep_c, vf_c = _rwkv_prep(rkv_c, lora_c, ts_prev[l], ts_next[l], rw_kk[l], vf_c, v0, v2pad,
                                  tb_ctx)
        prep_l, vf_l = _rwkv_prep(rkv_l, lora_l, ts_prev[l], ts_next[l], rw_kk[l], vf_l, v0, v2pad,
                                  tm_lat)
        rw_c = rw_l = None
        for dr in range(2):
            prm = (rw_w0[l, dr], _pad_rows(rw_w2[l, dr], dr * DECAY_LORA, LANES),
                   rw_a0[l, dr], _pad_rows(rw_a2[l, dr], dr * ICLR_LORA, LANES),
                   rw_g2[l, dr], rw_ka[l], rw_rk[l].reshape(-1), rw_gn_g[l], rw_gn_b[l])
            s0 = jnp.zeros((bsz, n_pair, LANES, LANES), F32)
            rw_c, s_ctx = _rwkv_scan(prep_c, lora_c, s0, prm, dr == 1, rw_c, tb_ctx)
            rw_l, _ = _rwkv_scan(prep_l, lora_l, s_ctx, prm, dr == 1, rw_l, tb_lat)

        att_l = _na_attention(qkv_l, qkv_c, _na_bias_table(na_rpb[l]))
        four_l = _fourier(f_l, n1_lat, n2_lat)

        wfm = w_fmap[l].astype(BF16)
        wo = w_out[l].astype(BF16)
        wr_pad = jnp.zeros((d, LANES), F32).at[:, 0:N_EXPERTS].set(w_router[l])
        br_pad = jnp.zeros((LANES,), F32).at[0:N_EXPERTS].set(b_router[l])
        xl, hl, lg_l = _outproj(alpha, four_l, rw_l, att_l, xl, lat(2), lat(4), lat(3), wfm, wo,
                                ln1_g[l], ln1_b[l], wr_pad, br_pad, tm_lat)
        wgu = w_gu[l].astype(BF16)
        wdn = w_dn[l].astype(BF16)
        if last:
            y = _moe(hl.reshape(-1, d), lg_l.reshape(-1, LANES)[:, 0:N_EXPERTS], wgu, b_gu[l], wdn,
                     b_dn[l], tm_moe)
            xl = _ln2(alpha, xl, y.reshape(xl.shape), lat(5), ln2_g[l], ln2_b[l], tm_lat)
        else:
            att_c = _ctx_attention(qkv_c)
            four_c = _fourier(f_c, n1_ctx, n2_ctx)
            xc, hc, lg_c = _outproj(alpha, four_c, rw_c, att_c, xc, cx(2), cx(4), cx(3), wfm, wo,
                                    ln1_g[l], ln1_b[l], wr_pad, br_pad, tm_ctx)
            h_all = jnp.concatenate([hl.reshape(-1, d), hc.reshape(-1, d)], axis=0)
            lg_all = jnp.concatenate([lg_l.reshape(-1, LANES), lg_c.reshape(-1, LANES)],
                                     axis=0)[:, 0:N_EXPERTS]
            y = _moe(h_all, lg_all, wgu, b_gu[l], wdn, b_dn[l], tm_moe)
            n_l = bsz * n_lat
            xl = _ln2(alpha, xl, y[:n_l].reshape(xl.shape), lat(5), ln2_g[l], ln2_b[l], tm_lat)
            xc = _ln2(alpha, xc, y[n_l:].reshape(xc.shape), cx(5), ln2_g[l], ln2_b[l], tm_ctx)
    return xl
```

```python
import functools

import numpy as np
import jax
import jax.numpy as jnp
from jax import lax
from jax.experimental import pallas as pl
from jax.experimental.pallas import tpu as pltpu

F32 = jnp.float32
BF16 = jnp.bfloat16
HI = lax.Precision.HIGHEST

GRID_W = 64
HEAD_DIM = 64
WIN_ROWS = 8
WIN_COLS = 16
N_EXPERTS = 32
TOP_K = 4
SWIGLU_LIMIT = 7.0
SWIGLU_ALPHA = 1.702
LN_EPS = 1e-5
GN_EPS = HEAD_DIM * 1e-5
DECAY_LORA = 64
ICLR_LORA = 64
GATE_LORA = 128
VRES_LORA = 32

LANES = 128
SUBLANES = 8
VMEM_LIMIT = 56 * 1024 * 1024

CHUNK = 64
NEG_BIG = -1e30


def _cp(sem, vmem=VMEM_LIMIT):
    return pltpu.CompilerParams(dimension_semantics=sem, vmem_limit_bytes=vmem)


def _dot(a, b, prec=HI):
    return jnp.dot(a, b, precision=prec, preferred_element_type=F32)


def _dot_nt(a, b, prec=HI):
    return lax.dot_general(a, b, (((1,), (1,)), ((), ())), precision=prec,
                           preferred_element_type=F32)


def _bdot(a, b):
    return jnp.dot(a.astype(BF16), b.astype(BF16), preferred_element_type=F32)


def _bdot_nt(a, b):
    return lax.dot_general(a.astype(BF16), b.astype(BF16), (((1,), (1,)), ((), ())),
                           preferred_element_type=F32)


def _sigmoid(x):
    return 1.0 / (1.0 + jnp.exp(-x))


def _split2(x):
    x1 = x.astype(BF16)
    x2 = (x - x1.astype(F32)).astype(BF16)
    return x1, x2


def _lhs3(x):
    x1, x2 = _split2(x)
    return jnp.concatenate([x1, x2, x1], axis=1)


def _rhs3(y):
    y1, y2 = _split2(y)
    return jnp.concatenate([y1, y1, y2], axis=0)


def _rhs3_nt(y):
    y1, y2 = _split2(y)
    return jnp.concatenate([y1, y1, y2], axis=1)


def _mm3(a, b):
    return jnp.dot(_lhs3(a), _rhs3(b), preferred_element_type=F32)


def _mm3_nt(a, b):
    return lax.dot_general(_lhs3(a), _rhs3_nt(b), (((1,), (1,)), ((), ())),
                           preferred_element_type=F32)


def _head_sum(x, ones3):
    outs = []
    for j in range(x.shape[1] // LANES):
        xs = x[:, j * LANES:(j + 1) * LANES]
        x1 = xs.astype(BF16)
        r1 = xs - x1.astype(F32)
        x2 = r1.astype(BF16)
        x3 = (r1 - x2.astype(F32)).astype(BF16)
        outs.append(jnp.dot(jnp.concatenate([x1, x2, x3], axis=1), ones3,
                            preferred_element_type=F32))
    return jnp.concatenate(outs, axis=1)


def _ada_kernel(c_ref, w_ref, b_ref, o_ref):
    c = c_ref[...]
    o_ref[0] = _dot(c * _sigmoid(c), w_ref[0]) + b_ref[0]


def _ada_all(cond, w_ada, b_ada):
    n_layer, d, n = w_ada.shape
    tn = 1536
    return pl.pallas_call(
        _ada_kernel,
        out_shape=jax.ShapeDtypeStruct((n_layer, SUBLANES, n), F32),
        grid=(n_layer, n // tn),
        in_specs=[pl.BlockSpec((SUBLANES, d), lambda l, j: (0, 0)),
                  pl.BlockSpec((1, d, tn), lambda l, j: (l, 0, j)),
                  pl.BlockSpec((1, 1, tn), lambda l, j: (l, 0, j))],
        out_specs=pl.BlockSpec((1, SUBLANES, tn), lambda l, j: (l, 0, j)),
        compiler_params=_cp(("arbitrary", "arbitrary")),
        name="ada",
    )(cond, w_ada, b_ada.reshape(n_layer, 1, n))


def _inproj_kernel(x_ref, sc_ref, sh_ref, wf_ref, wr_ref, wl_ref, wq_ref,
                   of_ref, or_ref, ol_ref, oq_ref):
    xm = (x_ref[0] * (1.0 + sc_ref[0]) + sh_ref[0]).astype(BF16)
    of_ref[0] = jnp.dot(xm, wf_ref[...], preferred_element_type=F32)
    or_ref[0] = jnp.dot(xm, wr_ref[...], preferred_element_type=F32)
    ol_ref[0] = jnp.dot(xm, wl_ref[...], preferred_element_type=F32)
    oq_ref[0] = jnp.dot(xm, wq_ref[...], preferred_element_type=F32)


def _inproj(x, sc, sh, wf, wr, wl, wq, tm):
    g, m, d = x.shape
    ws = (wf, wr, wl, wq)
    mod = pl.BlockSpec((1, 1, d), lambda b, i: (b, 0, 0))
    return pl.pallas_call(
        _inproj_kernel,
        out_shape=[jax.ShapeDtypeStruct((g, m, w.shape[1]), F32) for w in ws],
        grid=(g, m // tm),
        in_specs=[pl.BlockSpec((1, tm, d), lambda b, i: (b, i, 0)), mod, mod]
                 + [pl.BlockSpec(w.shape, lambda b, i: (0, 0)) for w in ws],
        out_specs=[pl.BlockSpec((1, tm, w.shape[1]), lambda b, i: (b, i, 0)) for w in ws],
        compiler_params=_cp(("parallel", "parallel")),
        name="inproj",
    )(x, sc, sh, *ws)


def _dft_consts(t_len, n1, n2, tb2, width):
    groups = width // HEAD_DIM
    j = np.arange(HEAD_DIM)
    ang = 2.0 * np.pi * np.outer(j, j) / HEAD_DIM
    eye = np.eye(groups)
    cbd = np.kron(eye, np.cos(ang)) / np.sqrt(HEAD_DIM)
    sbd = np.kron(eye, np.sin(ang)) / np.sqrt(HEAD_DIM)
    a1 = 2.0 * np.pi * np.outer(np.arange(n1), np.arange(n1)) / n1
    c1, s1 = np.cos(a1), np.sin(a1)
    atw = 2.0 * np.pi * np.outer(np.arange(n1), np.arange(n2)) / t_len
    twc = np.cos(atw).reshape(n1, n2 // tb2, tb2).transpose(1, 0, 2)
    tws = np.sin(atw).reshape(n1, n2 // tb2, tb2).transpose(1, 0, 2)
    a2 = 2.0 * np.pi * np.outer(np.arange(n2), np.arange(n2)) / n2
    c2, s2 = np.cos(a2) / np.sqrt(t_len), np.sin(a2) / np.sqrt(t_len)
    return [jnp.asarray(v, F32) for v in (cbd, sbd, c1, s1, twc, tws, c2, s2)]


def _four_a_kernel(tb2, width, x_ref, cbd_ref, sbd_ref, c1_ref, s1_ref, twc_ref, tws_ref,
                   yr_ref, yi_ref):
    c1 = c1_ref[...]
    s1 = s1_ref[...]
    cbd = cbd_ref[...]
    sbd = sbd_ref[...]
    for jj in range(tb2):
        x = x_ref[0, :, width * jj:width * (jj + 1)]
        zr = _dot(x, cbd)
        zi = -_dot(x, sbd)
        ar = _dot(c1, zr) + _dot(s1, zi)
        ai = _dot(c1, zi) - _dot(s1, zr)
        tc = twc_ref[0, :, jj:jj + 1]
        ts = tws_ref[0, :, jj:jj + 1]
        yr_ref[0, jj] = tc * ar + ts * ai
        yi_ref[0, jj] = tc * ai - ts * ar


def _four_b_kernel(yr_ref, yi_ref, c2_ref, s2_ref, o_ref):
    o_ref[0] = _dot(c2_ref[...], yr_ref[0]) + _dot(s2_ref[...], yi_ref[0])


def _fourier(f, n1, n2):
    bsz, t_len, width = f.shape
    tb2 = SUBLANES
    cbd, sbd, c1, s1, twc, tws, c2, s2 = _dft_consts(t_len, n1, n2, tb2, width)
    const = lambda a: pl.BlockSpec(a.shape, lambda b, j: (0,) * a.ndim)
    yr, yi = pl.pallas_call(
        functools.partial(_four_a_kernel, tb2, width),
        out_shape=[jax.ShapeDtypeStruct((bsz, n2, n1, width), F32)] * 2,
        grid=(bsz, n2 // tb2),
        in_specs=[pl.BlockSpec((1, n1, tb2 * width), lambda b, j: (b, 0, j)),
                  const(cbd), const(sbd), const(c1), const(s1),
                  pl.BlockSpec((1, n1, tb2), lambda b, j: (j, 0, 0)),
                  pl.BlockSpec((1, n1, tb2), lambda b, j: (j, 0, 0))],
        out_specs=[pl.BlockSpec((1, tb2, n1, width), lambda b, j: (b, j, 0, 0))] * 2,
        compiler_params=_cp(("parallel", "parallel")),
        name="fourier_a",
    )(f.reshape(bsz, n1, n2 * width), cbd, sbd, c1, s1, twc, tws)
    ncol = n1 * width
    tc = min(ncol, 2048)
    out = pl.pallas_call(
        _four_b_kernel,
        out_shape=jax.ShapeDtypeStruct((bsz, n2, ncol), F32),
        grid=(bsz, ncol // tc),
        in_specs=[pl.BlockSpec((1, n2, tc), lambda b, j: (b, 0, j)),
                  pl.BlockSpec((1, n2, tc), lambda b, j: (b, 0, j)),
                  const(c2), const(s2)],
        out_specs=pl.BlockSpec((1, n2, tc), lambda b, j: (b, 0, j)),
        compiler_params=_cp(("parallel", "parallel")),
        name="fourier_b",
    )(yr.reshape(bsz, n2, ncol), yi.reshape(bsz, n2, ncol), c2, s2)
    return out.reshape(bsz, t_len, width)


def _split_len(t_len):
    n1 = 1 << ((t_len.bit_length() - 1 + 1) // 2)
    return n1, t_len // n1


def _head_ones3():
    h = np.arange(LANES) // HEAD_DIM
    one = (h[:, None] == h[None, :]).astype(np.float32)
    return jnp.asarray(np.concatenate([one, one, one], axis=0), BF16)


def _rwkv_prep_kernel(first, x_ref, xp_ref, xn_ref, lo_ref, tsp_ref, tsn_ref, kk_ref,
                      ones_ref, *rest):
    if first:
        o_ref, vf_out_ref = rest
    else:
        v0_ref, v2_ref, vf_ref, o_ref = rest
    i = pl.program_id(1)
    last = pl.num_programs(1) - 1
    x = x_ref[0]
    tb, w3 = x.shape
    rw = w3 // 3
    row = lax.broadcasted_iota(jnp.int32, (tb, 1), 0)
    prev_edge = jnp.where(i == 0, 0.0, xp_ref[0, SUBLANES - 1:SUBLANES, :])
    next_edge = jnp.where(i == last, 0.0, xn_ref[0, 0:1, :])
    prev = jnp.where(row == 0, prev_edge, pltpu.roll(x, 1, 0))
    nxt = jnp.where(row == tb - 1, next_edge, pltpu.roll(x, tb - 1, 0))
    s = x + tsp_ref[...] * (prev - x) + tsn_ref[...] * (nxt - x)
    r = s[:, 0:rw]
    k = s[:, rw:2 * rw]
    v = s[:, 2 * rw:3 * rw]
    if first:
        vf_out_ref[0] = v
    else:
        mix = _sigmoid(v0_ref[...] + _mm3(lo_ref[0], v2_ref[...]))
        v = v + (vf_ref[0] - v) * mix
    kk = k * kk_ref[...]
    ss = _head_sum(kk * kk, ones_ref[...])
    kk = kk * lax.rsqrt(jnp.maximum(ss, 1e-24))
    o_ref[0, :, 0:rw] = r
    o_ref[0, :, rw:2 * rw] = k
    o_ref[0, :, 2 * rw:3 * rw] = v
    o_ref[0, :, 3 * rw:4 * rw] = kk


def _rwkv_prep(rkv, lora, ts_prev, ts_next, k_k, v_first, v0, v2pad, tb):
    g, m, w3 = rkv.shape
    rw = w3 // 3
    first = v_first is None
    nb8 = tb // SUBLANES
    n8 = m // SUBLANES
    ones3 = _head_ones3()
    row = lambda a: a.reshape(1, -1)
    vec = lambda n: pl.BlockSpec((1, n), lambda b, i: (0, 0))
    in_specs = [
        pl.BlockSpec((1, tb, w3), lambda b, i: (b, i, 0)),
        pl.BlockSpec((1, SUBLANES, w3), lambda b, i: (b, jnp.maximum(i * nb8 - 1, 0), 0)),
        pl.BlockSpec((1, SUBLANES, w3), lambda b, i: (b, jnp.minimum((i + 1) * nb8, n8 - 1), 0)),
        pl.BlockSpec((1, tb, LANES), lambda b, i: (b, i, 4)),
        vec(w3), vec(w3), vec(rw),
        pl.BlockSpec(ones3.shape, lambda b, i: (0, 0)),
    ]
    args = [rkv, rkv, rkv, lora, row(ts_prev), row(ts_next), row(k_k), ones3]
    out_main = jax.ShapeDtypeStruct((g, m, 4 * rw), F32)
    spec_main = pl.BlockSpec((1, tb, 4 * rw), lambda b, i: (b, i, 0))
    spec_v = pl.BlockSpec((1, tb, rw), lambda b, i: (b, i, 0))
    if first:
        out_shape = [out_main, jax.ShapeDtypeStruct((g, m, rw), F32)]
        out_specs = [spec_main, spec_v]
    else:
        in_specs += [vec(rw), pl.BlockSpec((LANES, rw), lambda b, i: (0, 0)), spec_v]
        args += [row(v0), v2pad, v_first]
        out_shape = out_main
        out_specs = spec_main
    res = pl.pallas_call(
        functools.partial(_rwkv_prep_kernel, first),
        out_shape=out_shape, grid=(g, m // tb), in_specs=in_specs, out_specs=out_specs,
        compiler_params=_cp(("parallel", "parallel")),
        name="rwkv_prep",
    )(*args)
    if first:
        return res[0], res[1]
    return res, v_first


def _stack_heads(x, lo_mask):
    return jnp.concatenate([jnp.where(lo_mask, x, 0.0), jnp.where(lo_mask, 0.0, x)], axis=0)


def _rwkv_scan_kernel(reverse, with_acc, n_chunk, p_ref, lo_ref, s0_ref, w0_ref, w2_ref,
                      a0_ref, a2_ref, g2_ref, ka_ref, rk_ref, gng_ref, gnb_ref, ones_ref,
                      *rest):
    if with_acc:
        acc_ref, o_ref, so_ref, s_scr, y_scr = rest
    else:
        o_ref, so_ref, s_scr, y_scr = rest
    i = pl.program_id(1)
    L = CHUNK
    L2 = 2 * L
    L4 = 4 * L

    @pl.when(i == 0)
    def _():
        s_scr[...] = s0_ref[0]

    p = p_ref[0]
    rw = p.shape[1] // 4
    n_pair = rw // LANES
    r = p[:, 0:rw]
    k = p[:, rw:2 * rw]
    v = p[:, 2 * rw:3 * rw]
    kk = p[:, 3 * rw:4 * rw]
    lo = lo_ref[0]
    dcol = 1 if reverse else 0
    zw = w0_ref[...] + _mm3(jnp.tanh(lo[:, 0:LANES]), w2_ref[...])
    log_w = -(jnp.maximum(-zw, 0.0) + jnp.log(1.0 + jnp.exp(-jnp.abs(zw)))) - 0.5
    lw = -jnp.exp(log_w)
    iclr = _sigmoid(a0_ref[...] + _mm3(lo[:, LANES:2 * LANES], a2_ref[...]))
    kd = k * (1.0 + (iclr - 1.0) * ka_ref[...])
    av = -kk
    bv = kk * iclr
    gate = _mm3(_sigmoid(lo[:, (2 + dcol) * LANES:(3 + dcol) * LANES]), g2_ref[...])
    ones3 = ones_ref[...]
    bonus = _head_sum(r * kd * rk_ref[...], ones3) * v

    ti = lax.broadcasted_iota(jnp.int32, (L4, L4), 0)
    si = lax.broadcasted_iota(jnp.int32, (L4, L4), 1)
    tq = ti % L2
    sq = si % L2
    tt = tq % L
    st = sq % L
    before = (st > tt) if reverse else (st < tt)
    keep = ((tq // L) == (sq // L)) & (before | ((ti >= L2) & (st == tt)))
    ei = lax.broadcasted_iota(jnp.int32, (L2, L2), 0)
    ej = lax.broadcasted_iota(jnp.int32, (L2, L2), 1)
    eye = jnp.where(ei == ej, 1.0, 0.0)
    ci = lax.broadcasted_iota(jnp.int32, (L, L), 0)
    cj = lax.broadcasted_iota(jnp.int32, (L, L), 1)
    tri = jnp.where((cj >= ci) if reverse else (cj <= ci), 1.0, 0.0).astype(BF16)
    tri3 = jnp.concatenate([tri, tri, tri], axis=1)
    lo_mask = lax.broadcasted_iota(jnp.int32, (1, LANES), 1) < HEAD_DIM
    end_row = 0 if reverse else L - 1

    order = range(n_chunk - 1, -1, -1) if reverse else range(n_chunk)
    for c in order:
        rs = slice(c * L, (c + 1) * L)
        for j in range(n_pair):
            cs = slice(j * LANES, (j + 1) * LANES)
            lwc = lw[rs, cs]
            l1 = lwc.astype(BF16)
            lr = lwc - l1.astype(F32)
            l2 = lr.astype(BF16)
            l3 = (lr - l2.astype(F32)).astype(BF16)
            cum = jnp.dot(tri3, jnp.concatenate([l1, l2, l3], axis=0),
                          preferred_element_type=F32)
            tot = cum[end_row:end_row + 1, :]
            g_inc = jnp.exp(cum)
            g_exc = jnp.exp(cum - lwc)
            g_inv = jnp.exp(-cum)
            g_end = jnp.exp(tot - cum)
            a_s = _stack_heads(av[rs, cs] * g_exc, lo_mask)
            b_s = _stack_heads(bv[rs, cs] * g_inv, lo_mask)
            k_s = _stack_heads(kd[rs, cs] * g_inv, lo_mask)
            r_s = _stack_heads(r[rs, cs] * g_inc, lo_mask)
            v_s = _stack_heads(v[rs, cs], lo_mask)
            bh_s = _stack_heads(bv[rs, cs] * g_end, lo_mask)
            kh_s = _stack_heads(kd[rs, cs] * g_end, lo_mask)
            gram = _mm3_nt(jnp.concatenate([a_s, r_s], axis=0),
                           jnp.concatenate([b_s, k_s], axis=0))
            gram = jnp.where(keep, gram, 0.0)
            a_ab = gram[0:L2, 0:L2]
            a_ak = gram[0:L2, L2:L4]
            a_r = gram[L2:L4, :]
            tinv = eye + a_ab
            apow = a_ab
            for _ in range(5):
                apow = _mm3(apow, apow)
                tinv = tinv + _mm3(tinv, apow)
            wu = _mm3(tinv, jnp.concatenate([a_s, _mm3(a_ak, v_s)], axis=1))
            s_old = s_scr[j]
            hy = _mm3_nt(jnp.concatenate([wu[:, 0:L2], r_s], axis=0), s_old)
            u = hy[0:L2] + wu[:, L2:L4]
            y = hy[L2:L4] + _mm3(a_r, jnp.concatenate([u, v_s], axis=0))
            s_scr[j] = s_old * jnp.exp(tot) + _mm3(
                jnp.concatenate([u.T, v_s.T], axis=1), jnp.concatenate([bh_s, kh_s], axis=0))
            y_scr[rs, cs] = y[0:L] + y[L:L2]

    y = y_scr[...]
    mu = _head_sum(y, ones3) * (1.0 / HEAD_DIM)
    d = y - mu
    var = _head_sum(d * d, ones3) * (1.0 / HEAD_DIM)
    out = (d * lax.rsqrt(var + GN_EPS) * gng_ref[...] + gnb_ref[...] + bonus) * gate
    if with_acc:
        out = out + acc_ref[0]
    o_ref[0] = out

    @pl.when(i == pl.num_programs(1) - 1)
    def _():
        so_ref[0] = s_scr[...]


def _rwkv_scan(prep, lora, state0, prm, reverse, acc, tb):
    g, m, w4 = prep.shape
    rw = w4 // 4
    n_pair = rw // LANES
    nblk = m // tb
    w0, w2pad, a0, a2pad, g2, k_a, r_k, gn_g, gn_b = prm
    ones3 = _head_ones3()
    row = lambda a: a.reshape(1, -1)
    if reverse:
        tmap = lambda b, i: (b, nblk - 1 - i, 0)
    else:
        tmap = lambda b, i: (b, i, 0)
    vec = pl.BlockSpec((1, rw), lambda b, i: (0, 0))
    mat = pl.BlockSpec((LANES, rw), lambda b, i: (0, 0))
    st_spec = pl.BlockSpec((1, n_pair, LANES, LANES), lambda b, i: (b, 0, 0, 0))
    in_specs = [pl.BlockSpec((1, tb, w4), tmap),
                pl.BlockSpec((1, tb, lora.shape[2]), tmap),
                st_spec, vec, mat, vec, mat, mat, vec, vec, vec, vec,
                pl.BlockSpec(ones3.shape, lambda b, i: (0, 0))]
    args = [prep, lora, state0, row(w0), w2pad, row(a0), a2pad, g2, row(k_a), row(r_k),
            row(gn_g), row(gn_b), ones3]
    if acc is not None:
        in_specs.append(pl.BlockSpec((1, tb, rw), tmap))
        args.append(acc)
    y, s_out = pl.pallas_call(
        functools.partial(_rwkv_scan_kernel, reverse, acc is not None, tb // CHUNK),
        out_shape=[jax.ShapeDtypeStruct((g, m, rw), F32),
                   jax.ShapeDtypeStruct((g, n_pair, LANES, LANES), F32)],
        grid=(g, nblk), in_specs=in_specs,
        out_specs=[pl.BlockSpec((1, tb, rw), tmap), st_spec],
        scratch_shapes=[pltpu.VMEM((n_pair, LANES, LANES), F32), pltpu.VMEM((tb, rw), F32)],
        compiler_params=_cp(("arbitrary", "arbitrary")),
        name="rwkv_scan_bwd" if reverse else "rwkv_scan_fwd",
    )(*args)
    return y, s_out


def _scan_masks(reverse):
    L = CHUNK
    L2, L4 = 2 * L, 4 * L
    ti = lax.broadcasted_iota(jnp.int32, (L4, L4), 0)
    si = lax.broadcasted_iota(jnp.int32, (L4, L4), 1)
    tq = ti % L2
    sq = si % L2
    tt = tq % L
    st = sq % L
    before = (st > tt) if reverse else (st < tt)
    keep = ((tq // L) == (sq // L)) & (before | ((ti >= L2) & (st == tt)))
    ci = lax.broadcasted_iota(jnp.int32, (L, L), 0)
    cj = lax.broadcasted_iota(jnp.int32, (L, L), 1)
    tri = jnp.where((cj >= ci) if reverse else (cj <= ci), 1.0, 0.0).astype(BF16)
    return keep, jnp.concatenate([tri, tri, tri], axis=1)


def _rwkv_bidir_kernel(pf_ref, pb_ref, lof_ref, lob_ref, s0_ref, w0_ref, w2_ref, a0_ref, a2_ref,
                       g2_ref, ka_ref, rk_ref, gng_ref, gnb_ref, ones_ref,
                       of_ref, ob_ref, so_ref, s_scr):
    i = pl.program_id(0)
    L = CHUNK
    L2, L4 = 2 * L, 4 * L
    bsz = pf_ref.shape[0]
    rw = pf_ref.shape[2] // 4
    n_pair = rw // LANES

    @pl.when(i == 0)
    def _():
        s_scr[...] = s0_ref[...]

    ones3 = ones_ref[...]
    lo_mask = lax.broadcasted_iota(jnp.int32, (1, LANES), 1) < HEAD_DIM
    ei = lax.broadcasted_iota(jnp.int32, (L2, L2), 0)
    ej = lax.broadcasted_iota(jnp.int32, (L2, L2), 1)
    eye = jnp.where(ei == ej, 1.0, 0.0)
    masks = (_scan_masks(False), _scan_masks(True))

    streams = []
    for dr, (p_ref, lo_ref) in enumerate(((pf_ref, lof_ref), (pb_ref, lob_ref))):
        for b in range(bsz):
            p = p_ref[b]
            lo = lo_ref[b]
            r = p[:, 0:rw]
            k = p[:, rw:2 * rw]
            v = p[:, 2 * rw:3 * rw]
            kk = p[:, 3 * rw:4 * rw]
            zw = w0_ref[dr] + _mm3(jnp.tanh(lo[:, 0:LANES]), w2_ref[dr])
            log_w = -(jnp.maximum(-zw, 0.0) + jnp.log(1.0 + jnp.exp(-jnp.abs(zw)))) - 0.5
            lw = -jnp.exp(log_w)
            iclr = _sigmoid(a0_ref[dr] + _mm3(lo[:, LANES:2 * LANES], a2_ref[dr]))
            kd = k * (1.0 + (iclr - 1.0) * ka_ref[...])
            gate = _mm3(_sigmoid(lo[:, (2 + dr) * LANES:(3 + dr) * LANES]), g2_ref[dr])
            bonus = _head_sum(r * kd * rk_ref[...], ones3) * v
            streams.append(dict(dr=dr, b=b, r=r, kd=kd, v=v, av=-kk, bv=kk * iclr, lw=lw,
                                gate=gate, bonus=bonus))

    chains = [(si_, j) for si_ in range(len(streams)) for j in range(n_pair)]

    def cols(si_, j, name):
        return streams[si_][name][:, j * LANES:(j + 1) * LANES]

    ops = []
    for si_, j in chains:
        dr = streams[si_]['dr']
        keep, tri3 = masks[dr]
        lwc = cols(si_, j, 'lw')
        l1 = lwc.astype(BF16)
        lr = lwc - l1.astype(F32)
        l2 = lr.astype(BF16)
        l3 = (lr - l2.astype(F32)).astype(BF16)
        cum = jnp.dot(tri3, jnp.concatenate([l1, l2, l3], axis=0), preferred_element_type=F32)
        end_row = 0 if dr else L - 1
        tot = cum[end_row:end_row + 1, :]
        g_inv = jnp.exp(-cum)
        g_end = jnp.exp(tot - cum)
        bvc = cols(si_, j, 'bv')
        kdc = cols(si_, j, 'kd')
        ops.append(dict(
            keep=keep, tot=tot,
            a_s=_stack_heads(cols(si_, j, 'av') * jnp.exp(cum - lwc), lo_mask),
            b_s=_stack_heads(bvc * g_inv, lo_mask),
            k_s=_stack_heads(kdc * g_inv, lo_mask),
            r_s=_stack_heads(cols(si_, j, 'r') * jnp.exp(cum), lo_mask),
            v_s=_stack_heads(cols(si_, j, 'v'), lo_mask),
            bh_s=_stack_heads(bvc * g_end, lo_mask),
            kh_s=_stack_heads(kdc * g_end, lo_mask)))

    grams = [jnp.where(o['keep'],
                       _mm3_nt(jnp.concatenate([o['a_s'], o['r_s']], axis=0),
                               jnp.concatenate([o['b_s'], o['k_s']], axis=0)), 0.0)
             for o in ops]
    tinv = [eye + g[0:L2, 0:L2] for g in grams]
    apb = [g[0:L2, 0:L2].astype(BF16) for g in grams]
    for _ in range(5):
        apb = [jnp.dot(a, a, preferred_element_type=F32).astype(BF16) for a in apb]
        tinv = [t + jnp.dot(t.astype(BF16), a, preferred_element_type=F32)
                for t, a in zip(tinv, apb)]
    x1 = [_bdot(g[0:L2, L2:L4], o['v_s']) for g, o in zip(grams, ops)]
    wu = [_bdot(t, jnp.concatenate([o['a_s'], x], axis=1))
          for t, o, x in zip(tinv, ops, x1)]
    s_old = [s_scr[c] for c in range(len(chains))]
    hy = [_mm3_nt(jnp.concatenate([w[:, 0:L2], o['r_s']], axis=0), s)
          for w, o, s in zip(wu, ops, s_old)]
    u = [h[0:L2] + w[:, L2:L4] for h, w in zip(hy, wu)]
    ys = [h[L2:L4] + _mm3(g[L2:L4, :], jnp.concatenate([uu, o['v_s']], axis=0))
          for h, g, uu, o in zip(hy, grams, u, ops)]
    for c, (s, uu, o) in enumerate(zip(s_old, u, ops)):
        s_scr[c] = s * jnp.exp(o['tot']) + _mm3(
            jnp.concatenate([uu.T, o['v_s'].T], axis=1),
            jnp.concatenate([o['bh_s'], o['kh_s']], axis=0))

    for si_, st in enumerate(streams):
        y = jnp.concatenate([ys[si_ * n_pair + j][0:L] + ys[si_ * n_pair + j][L:L2]
                             for j in range(n_pair)], axis=1)
        mu = _head_sum(y, ones3) * (1.0 / HEAD_DIM)
        d = y - mu
        var = _head_sum(d * d, ones3) * (1.0 / HEAD_DIM)
        out = (d * lax.rsqrt(var + GN_EPS) * gng_ref[...] + gnb_ref[...] + st['bonus']) * st['gate']
        if st['dr'] == 0:
            of_ref[st['b']] = out
        else:
            ob_ref[st['b']] = out

    @pl.when(i == pl.num_programs(0) - 1)
    def _():
        so_ref[...] = s_scr[...]


def _rwkv_bidir(prep, lora, state0, prm):
    g, m, w4 = prep.shape
    rw = w4 // 4
    n_pair = rw // LANES
    nblk = m // CHUNK
    w0, w2pad, a0, a2pad, g2, k_a, r_k, gn_g, gn_b = prm
    ones3 = _head_ones3()
    row = lambda a: a.reshape(1, -1)
    fmap = lambda i: (0, i, 0)
    bmap = lambda i: (0, nblk - 1 - i, 0)
    full = lambda a: pl.BlockSpec(a.shape, lambda i: (0,) * a.ndim)
    w0, a0 = w0[:, None, :], a0[:, None, :]
    consts = [w0, w2pad, a0, a2pad, g2, row(k_a), row(r_k), row(gn_g), row(gn_b), ones3]
    st_spec = pl.BlockSpec(state0.shape, lambda i: (0, 0, 0))
    y_f, y_b, s_out = pl.pallas_call(
        _rwkv_bidir_kernel,
        out_shape=[jax.ShapeDtypeStruct((g, m, rw), F32), jax.ShapeDtypeStruct((g, m, rw), F32),
                   jax.ShapeDtypeStruct(state0.shape, F32)],
        grid=(nblk,),
        in_specs=[pl.BlockSpec((g, CHUNK, w4), fmap), pl.BlockSpec((g, CHUNK, w4), bmap),
                  pl.BlockSpec((g, CHUNK, lora.shape[2]), fmap),
                  pl.BlockSpec((g, CHUNK, lora.shape[2]), bmap),
                  st_spec] + [full(a) for a in consts],
        out_specs=[pl.BlockSpec((g, CHUNK, rw), fmap), pl.BlockSpec((g, CHUNK, rw), bmap), st_spec],
        scratch_shapes=[pltpu.VMEM(state0.shape, F32)],
        compiler_params=_cp(("arbitrary",)),
        name="rwkv_scan",
    )(prep, prep, lora, lora, state0, *consts)
    return y_f, y_b, s_out


def _na_bias_table(rpb):
    n_head = rpb.shape[0]
    col = np.arange(GRID_W)
    col_start = np.clip(col - WIN_COLS // 2, 0, GRID_W - WIN_COLS)
    col_in = (col[None, :] >= col_start[:, None]) & (col[None, :] < col_start[:, None] + WIN_COLS)
    col_rel = np.clip(col[None, :] - col[:, None] + WIN_COLS - 1, 0, 2 * WIN_COLS - 2)
    n_rel = 2 * WIN_COLS - 1
    sel = (col_rel.reshape(-1)[None, :] == np.arange(n_rel)[:, None]).astype(np.float32)
    c = jnp.einsum('hrc,cn->hrn', rpb, jnp.asarray(sel), precision=HI)
    c = c.reshape(n_head, 2 * WIN_ROWS - 1, GRID_W, GRID_W)
    c = jnp.where(col_in[None, None], c, NEG_BIG)
    t = jnp.stack([c[:, d:d + WIN_ROWS] for d in range(WIN_ROWS)], axis=1)
    t = t.transpose(0, 1, 3, 2, 4).reshape(n_head // 2, 2, WIN_ROWS, GRID_W, WIN_ROWS * GRID_W)
    return t.astype(F32)


def _na_kernel(rows, q_ref, kp_ref, kc_ref, kn_ref, vp_ref, vc_ref, vn_ref, kx_ref, vx_ref,
               b_ref, o_ref, kcat, vcat):
    i = pl.program_id(2)
    w = GRID_W
    blk = WIN_ROWS * w
    scale = HEAD_DIM ** -0.5
    kcat[0:blk] = kp_ref[0].astype(BF16)
    kcat[blk:2 * blk] = kc_ref[0].astype(BF16)
    kcat[2 * blk:3 * blk] = kn_ref[0].astype(BF16)
    vcat[0:blk] = vp_ref[0].astype(BF16)
    vcat[blk:2 * blk] = vc_ref[0].astype(BF16)
    vcat[2 * blk:3 * blk] = vn_ref[0].astype(BF16)
    kx = kx_ref[0].astype(BF16)
    vx = vx_ref[0].astype(BF16)
    lo_mask = lax.broadcasted_iota(jnp.int32, (1, LANES), 1) < HEAD_DIM
    for rr in range(WIN_ROWS):
        r = i * WIN_ROWS + rr
        r0 = jnp.clip(r - WIN_ROWS // 2, 0, rows - WIN_ROWS)
        off = pl.multiple_of((r0 - (i - 1) * WIN_ROWS) * w, w)
        dlt = r0 - r + WIN_ROWS - 1
        q = q_ref[0, rr * w:(rr + 1) * w, :]
        qs = _stack_heads(q, lo_mask).astype(BF16)
        kw = kcat[pl.ds(off, blk), :]
        vw = vcat[pl.ds(off, blk), :]
        bias = jnp.concatenate([b_ref[0, 0, dlt], b_ref[0, 1, dlt]], axis=0)
        s = _bdot_nt(qs, kw) * scale + bias
        sx = _bdot_nt(qs, kx) * scale
        mx = jnp.maximum(jnp.max(s, axis=-1, keepdims=True), jnp.max(sx, axis=-1, keepdims=True))
        p = jnp.exp(s - mx)
        px = jnp.exp(sx - mx)
        den = jnp.sum(p, axis=-1, keepdims=True) + jnp.sum(px, axis=-1, keepdims=True)
        o = (_bdot(p, vw) + _bdot(px, vx)) / den
        o_ref[0, rr * w:(rr + 1) * w, :] = jnp.where(lo_mask, o[0:w], o[w:2 * w])


def _na_attention(qkv, qkv_ctx, bias_tab):
    bsz, t_len, w3 = qkv.shape
    na = w3 // 3
    n_pair = na // LANES
    rows = t_len // GRID_W
    blk = WIN_ROWS * GRID_W
    nblk = rows // WIN_ROWS
    c_len = qkv_ctx.shape[1]
    kv = lambda sel, shift: pl.BlockSpec(
        (1, blk, LANES), lambda b, j, i: (b, jnp.clip(i + shift, 0, nblk - 1), sel * n_pair + j))
    cx = lambda sel: pl.BlockSpec((1, c_len, LANES), lambda b, j, i: (b, 0, sel * n_pair + j))
    return pl.pallas_call(
        functools.partial(_na_kernel, rows),
        out_shape=jax.ShapeDtypeStruct((bsz, t_len, na), F32),
        grid=(bsz, n_pair, nblk),
        in_specs=[kv(0, 0), kv(1, -1), kv(1, 0), kv(1, 1), kv(2, -1), kv(2, 0), kv(2, 1),
                  cx(1), cx(2),
                  pl.BlockSpec((1, 2, WIN_ROWS, GRID_W, blk), lambda b, j, i: (j, 0, 0, 0, 0))],
        out_specs=pl.BlockSpec((1, blk, LANES), lambda b, j, i: (b, i, j)),
        scratch_shapes=[pltpu.VMEM((3 * blk, LANES), BF16), pltpu.VMEM((3 * blk, LANES), BF16)],
        compiler_params=_cp(("parallel", "parallel", "arbitrary")),
        name="na_attention",
    )(qkv, qkv, qkv, qkv, qkv, qkv, qkv, qkv_ctx, qkv_ctx, bias_tab)


def _ctx_attn_kernel(q_ref, k_ref, v_ref, o_ref):
    lo_mask = lax.broadcasted_iota(jnp.int32, (1, LANES), 1) < HEAD_DIM
    q = q_ref[0]
    c = q.shape[0]
    qs = _stack_heads(q, lo_mask)
    s = _bdot_nt(qs, k_ref[0]) * (HEAD_DIM ** -0.5)
    mx = jnp.max(s, axis=-1, keepdims=True)
    p = jnp.exp(s - mx)
    o = _bdot(p, v_ref[0]) / jnp.sum(p, axis=-1, keepdims=True)
    o_ref[0] = jnp.where(lo_mask, o[0:c], o[c:2 * c])


def _ctx_attention(qkv_ctx):
    bsz, c_len, w3 = qkv_ctx.shape
    na = w3 // 3
    n_pair = na // LANES
    sp = lambda sel: pl.BlockSpec((1, c_len, LANES), lambda b, j: (b, 0, sel * n_pair + j))
    return pl.pallas_call(
        _ctx_attn_kernel,
        out_shape=jax.ShapeDtypeStruct((bsz, c_len, na), F32),
        grid=(bsz, n_pair),
        in_specs=[sp(0), sp(1), sp(2)],
        out_specs=pl.BlockSpec((1, c_len, LANES), lambda b, j: (b, 0, j)),
        compiler_params=_cp(("parallel", "parallel")),
        name="ctx_attention",
    )(qkv_ctx, qkv_ctx, qkv_ctx)


def _layer_norm(z, g, b):
    mu = jnp.mean(z, axis=-1, keepdims=True)
    d = z - mu
    var = jnp.mean(d * d, axis=-1, keepdims=True)
    return d * lax.rsqrt(var + LN_EPS) * g + b


def _outproj_kernel(alpha, f_ref, rwf_ref, rwb_ref, at_ref, x_ref, ga_ref, sc_ref, sh_ref, wfm_ref,
                    wo_ref, g_ref, b_ref, wr_ref, br_ref, xo_ref, h_ref, lg_ref):
    nf = f_ref.shape[2]
    nr = rwf_ref.shape[2]
    fm = jnp.dot(f_ref[0].astype(BF16), wfm_ref[...], preferred_element_type=F32)
    rwo = (rwf_ref[0] + rwb_ref[0]).astype(BF16)
    mix = (jnp.dot(fm.astype(BF16), wo_ref[0:nf, :], preferred_element_type=F32)
           + jnp.dot(rwo, wo_ref[nf:nf + nr, :], preferred_element_type=F32)
           + jnp.dot(at_ref[0].astype(BF16), wo_ref[nf + nr:, :], preferred_element_type=F32))
    xn = _layer_norm(alpha * x_ref[0] + ga_ref[0] * mix, g_ref[...], b_ref[...])
    xo_ref[0] = xn
    h = xn * (1.0 + sc_ref[0]) + sh_ref[0]
    h_ref[0] = h
    lg_ref[0] = _mm3(h, wr_ref[...]) + br_ref[...]


def _outproj(alpha, four, rwo, rwb, att, x, ga, sc, sh, wfm, wo, ln_g, ln_b, wr_pad, br_pad, tm):
    g, m, d = x.shape
    blk = lambda a: pl.BlockSpec((1, tm, a.shape[2]), lambda b, i: (b, i, 0))
    mod = pl.BlockSpec((1, 1, d), lambda b, i: (b, 0, 0))
    cst = lambda a: pl.BlockSpec(a.shape, lambda b, i: (0, 0))
    row = lambda a: a.reshape(1, -1)
    ln_g, ln_b, br_pad = row(ln_g), row(ln_b), row(br_pad)
    return pl.pallas_call(
        functools.partial(_outproj_kernel, alpha),
        out_shape=[jax.ShapeDtypeStruct((g, m, d), F32), jax.ShapeDtypeStruct((g, m, d), F32),
                   jax.ShapeDtypeStruct((g, m, LANES), F32)],
        grid=(g, m // tm),
        in_specs=[blk(four), blk(rwo), blk(rwb), blk(att), blk(x), mod, mod, mod, cst(wfm), cst(wo),
                  cst(ln_g), cst(ln_b), cst(wr_pad), cst(br_pad)],
        out_specs=[pl.BlockSpec((1, tm, d), lambda b, i: (b, i, 0)),
                   pl.BlockSpec((1, tm, d), lambda b, i: (b, i, 0)),
                   pl.BlockSpec((1, tm, LANES), lambda b, i: (b, i, 0))],
        compiler_params=_cp(("parallel", "parallel")),
        name="outproj_ln",
    )(four, rwo, rwb, att, x, ga, sc, sh, wfm, wo, ln_g, ln_b, wr_pad, br_pad)


def _ln2_kernel(alpha, x_ref, y0_ref, y1_ref, y2_ref, y3_ref, gt_ref, ga_ref, g_ref, b_ref,
                o_ref):
    gt = gt_ref[...]
    y = (gt[:, 0:1] * y0_ref[0] + gt[:, 1:2] * y1_ref[0]
         + gt[:, 2:3] * y2_ref[0] + gt[:, 3:4] * y3_ref[0])
    o_ref[0] = _layer_norm(alpha * x_ref[0] + ga_ref[0] * y, g_ref[...], b_ref[...])


def _ln2(alpha, x, yk, gates, row0, ga, ln_g, ln_b, tm):
    g, m, d = x.shape
    nb = m // tm
    base = row0 // tm
    blk = pl.BlockSpec((1, tm, d), lambda b, i: (b, i, 0))
    vec = pl.BlockSpec((1, d), lambda b, i: (0, 0))
    ysp = lambda k: pl.BlockSpec((1, tm, d), lambda b, i: (k, base + b * nb + i, 0))
    return pl.pallas_call(
        functools.partial(_ln2_kernel, alpha),
        out_shape=jax.ShapeDtypeStruct((g, m, d), F32),
        grid=(g, nb),
        in_specs=[blk, ysp(0), ysp(1), ysp(2), ysp(3),
                  pl.BlockSpec((tm, LANES), lambda b, i: (base + b * nb + i, 0)),
                  pl.BlockSpec((1, 1, d), lambda b, i: (b, 0, 0)), vec, vec],
        out_specs=blk,
        compiler_params=_cp(("parallel", "parallel")),
        name="ln2",
    )(x, yk, yk, yk, yk, gates, ga, ln_g.reshape(1, -1), ln_b.reshape(1, -1))


def _moe_kernel(be_ref, nu_ref, x_ref, wgu_ref, bgu_ref, wdn_ref, bdn_ref, o_ref):
    i = pl.program_id(0)

    @pl.when(i < nu_ref[0])
    def _():
        de = wdn_ref.shape[1]
        gu = jnp.dot(x_ref[...].astype(BF16), wgu_ref[0], preferred_element_type=F32) + bgu_ref[0]
        gg = jnp.minimum(gu[:, 0:de], SWIGLU_LIMIT)
        uu = jnp.clip(gu[:, de:2 * de], -SWIGLU_LIMIT, SWIGLU_LIMIT)
        act = (uu + 1.0) * gg * _sigmoid(SWIGLU_ALPHA * gg)
        o_ref[...] = jnp.dot(act.astype(BF16), wdn_ref[0], preferred_element_type=F32) + bdn_ref[0]


def _moe_experts(xs, block_e, n_used, wgu, bgu, wdn, bdn, tm):
    cap, d = xs.shape
    n_exp, _, de2 = wgu.shape
    de = de2 // 2
    n_blocks = cap // tm
    grid_spec = pltpu.PrefetchScalarGridSpec(
        num_scalar_prefetch=2,
        grid=(n_blocks,),
        in_specs=[pl.BlockSpec((tm, d), lambda i, be, nu: (i, 0)),
                  pl.BlockSpec((1, d, de2), lambda i, be, nu: (be[i], 0, 0)),
                  pl.BlockSpec((1, 1, de2), lambda i, be, nu: (be[i], 0, 0)),
                  pl.BlockSpec((1, de, d), lambda i, be, nu: (be[i], 0, 0)),
                  pl.BlockSpec((1, 1, d), lambda i, be, nu: (be[i], 0, 0))],
        out_specs=pl.BlockSpec((tm, d), lambda i, be, nu: (i, 0)),
    )
    return pl.pallas_call(
        _moe_kernel,
        out_shape=jax.ShapeDtypeStruct((cap, d), F32),
        grid_spec=grid_spec,
        compiler_params=_cp(("arbitrary",)),
        name="moe_experts",
    )(block_e, n_used, xs, wgu, bgu.reshape(n_exp, 1, de2), wdn, bdn.reshape(n_exp, 1, d))


def _moe(h, logits, wgu, bgu, wdn, bdn, tm):
    n_tok, d = h.shape
    top_logit, top_e = lax.top_k(logits, TOP_K)
    gate = jax.nn.softmax(top_logit, axis=-1)
    gates = jnp.zeros((n_tok, LANES), F32).at[:, 0:TOP_K].set(gate)
    flat_e = top_e.reshape(-1).astype(jnp.int32)
    n_slot = n_tok * TOP_K
    order = jnp.argsort(flat_e).astype(jnp.int32)
    e_sorted = flat_e[order]
    bounds = jnp.searchsorted(e_sorted, jnp.arange(N_EXPERTS + 1, dtype=jnp.int32)).astype(jnp.int32)
    first = bounds[:-1]
    counts = bounds[1:] - first
    padded = (counts + tm - 1) // tm * tm
    pad_end = jnp.cumsum(padded)
    pad_start = pad_end - padded
    n_blocks = -(-n_slot // tm) + N_EXPERTS
    cap = n_blocks * tm
    block_e = jnp.minimum(
        jnp.searchsorted(pad_end, jnp.arange(n_blocks, dtype=jnp.int32) * tm, side='right'),
        N_EXPERTS - 1).astype(jnp.int32)
    n_used = (pad_end[-1:] // tm).astype(jnp.int32)
    row_e = jnp.repeat(block_e, tm)
    off = jnp.arange(cap, dtype=jnp.int32) - pad_start[row_e]
    src = jnp.clip(first[row_e] + off, 0, n_slot - 1)
    slot_tok = jnp.where(off < counts[row_e], order.at[src].get(mode='promise_in_bounds') // TOP_K, 0)
    rank = jnp.argsort(order).astype(jnp.int32)
    pos = (pad_start[flat_e] + rank - first[flat_e]).reshape(n_tok, TOP_K).T.reshape(-1)
    xs = h.at[slot_tok].get(mode='promise_in_bounds')
    ys = _moe_experts(xs, block_e, n_used, wgu, bgu, wdn, bdn, tm)
    yk = ys.at[pos].get(mode='promise_in_bounds').reshape(TOP_K, n_tok, d)
    return yk, gates


def _pad_rows(w, start, total):
    return jnp.zeros((total, w.shape[1]), w.dtype).at[start:start + w.shape[0]].set(w)


def kernel(x, c, ctx, c_ctx, w_ada, b_ada, w_in, w_vdown, w_fmap, ts_prev, ts_next, rw_w0, rw_w2, rw_a0, rw_a2, rw_g2, rw_kk, rw_ka, rw_rk, rw_gn_g, rw_gn_b, rw_v0, rw_v2, na_rpb, w_out, ln1_g, ln1_b, w_router, b_router, w_gu, b_gu, w_dn, b_dn, ln2_g, ln2_b):
    bsz, n_lat, d = x.shape
    c_len = ctx.shape[1]
    depth = w_in.shape[0]
    f_dim = w_fmap.shape[1]
    rw_dim = rw_kk.shape[1]
    na_dim = d - f_dim - rw_dim
    alpha = (2 * depth) ** 0.25
    n_pair = rw_dim // LANES

    tm_lat = min(512, n_lat)
    tm_ctx = min(256, c_len)
    tb_lat = min(128, n_lat)
    tb_ctx = min(128, c_len)
    tm_moe = 512
    n1_lat, n2_lat = _split_len(n_lat)
    n1_ctx, n2_ctx = _split_len(c_len)

    cond = jnp.zeros((SUBLANES, d), F32).at[0:bsz].set(c).at[bsz].set(c_ctx)
    ada = _ada_all(cond, w_ada, b_ada)

    xl, xc = x, ctx
    vf_l = vf_c = None
    o_r = f_dim
    o_lora = f_dim + 3 * rw_dim
    o_q = o_lora + 2 * DECAY_LORA + 2 * ICLR_LORA + 2 * GATE_LORA
    for l in range(depth):
        last = l == depth - 1
        mods = ada[l].reshape(SUBLANES, 6, d)
        lat = lambda k: mods[0:bsz, k][:, None, :]
        cx = lambda k: jnp.broadcast_to(mods[bsz, k][None, None, :], (bsz, 1, d))

        wi = w_in[l]
        vdown = jnp.zeros((d, LANES), F32)
        if l > 0:
            vdown = vdown.at[:, 0:VRES_LORA].set(w_vdown[l - 1])
        wf = wi[:, 0:f_dim].astype(BF16)
        wr = wi[:, o_r:o_lora].astype(BF16)
        wl = jnp.concatenate([wi[:, o_lora:o_q], vdown], axis=1).astype(BF16)
        wq = wi[:, o_q:].astype(BF16)
        f_l, rkv_l, lora_l, qkv_l = _inproj(xl, lat(1), lat(0), wf, wr, wl, wq, tm_lat)
        f_c, rkv_c, lora_c, qkv_c = _inproj(xc, cx(1), cx(0), wf, wr, wl, wq, tm_ctx)

        if l == 0:
            v0 = v2pad = None
        else:
            v0 = rw_v0[l - 1]
            v2pad = _pad_rows(rw_v2[l - 1], 0, LANES)
        prep_c, vf_c = _rwkv_prep(rkv_c, lora_c, ts_prev[l], ts_next[l], rw_kk[l], vf_c, v0, v2pad,
                                  tb_ctx)
        prep_l, vf_l = _rwkv_prep(rkv_l, lora_l, ts_prev[l], ts_next[l], rw_kk[l], vf_l, v0, v2pad,
                                  tm_lat)
        prm = (rw_w0[l],
               jnp.stack([_pad_rows(rw_w2[l, dr], dr * DECAY_LORA, LANES) for dr in range(2)]),
               rw_a0[l],
               jnp.stack([_pad_rows(rw_a2[l, dr], dr * ICLR_LORA, LANES) for dr in range(2)]),
               rw_g2[l], rw_ka[l], rw_rk[l].reshape(-1), rw_gn_g[l], rw_gn_b[l])
        s0 = jnp.zeros((2 * bsz * n_pair, LANES, LANES), F32)
        rwf_c, rwb_c, s_ctx = _rwkv_bidir(prep_c, lora_c, s0, prm)
        rwf_l, rwb_l, _ = _rwkv_bidir(prep_l, lora_l, s_ctx, prm)

        att_l = _na_attention(qkv_l, qkv_c, _na_bias_table(na_rpb[l]))
        four_l = _fourier(f_l, n1_lat, n2_lat)

        wfm = w_fmap[l].astype(BF16)
        wo = w_out[l].astype(BF16)
        wr_pad = jnp.zeros((d, LANES), F32).at[:, 0:N_EXPERTS].set(w_router[l])
        br_pad = jnp.zeros((LANES,), F32).at[0:N_EXPERTS].set(b_router[l])
        xl, hl, lg_l = _outproj(alpha, four_l, rwf_l, rwb_l, att_l, xl, lat(2), lat(4), lat(3), wfm, wo,
                                ln1_g[l], ln1_b[l], wr_pad, br_pad, tm_lat)
        wgu = w_gu[l].astype(BF16)
        wdn = w_dn[l].astype(BF16)
        if last:
            yk, gates = _moe(hl.reshape(-1, d), lg_l.reshape(-1, LANES)[:, 0:N_EXPERTS], wgu,
                             b_gu[l], wdn, b_dn[l], tm_moe)
            xl = _ln2(alpha, xl, yk, gates, 0, lat(5), ln2_g[l], ln2_b[l], tm_lat)
        else:
            att_c = _ctx_attention(qkv_c)
            four_c = _fourier(f_c, n1_ctx, n2_ctx)
            xc, hc, lg_c = _outproj(alpha, four_c, rwf_c, rwb_c, att_c, xc, cx(2), cx(4), cx(3), wfm, wo,
                                    ln1_g[l], ln1_b[l], wr_pad, br_pad, tm_ctx)
            h_all = jnp.concatenate([hl.reshape(-1, d), hc.reshape(-1, d)], axis=0)
            lg_all = jnp.concatenate([lg_l.reshape(-1, LANES), lg_c.reshape(-1, LANES)],
                                     axis=0)[:, 0:N_EXPERTS]
            yk, gates = _moe(h_all, lg_all, wgu, b_gu[l], wdn, b_dn[l], tm_moe)
            n_l = bsz * n_lat
            xl = _ln2(alpha, xl, yk, gates, 0, lat(5), ln2_g[l], ln2_b[l], tm_lat)
            xc = _ln2(alpha, xc, yk, gates, n_l, cx(5), ln2_g[l], ln2_b[l], tm_ctx)
    return xl
```

```python
import functools

import numpy as np
import jax
import jax.numpy as jnp
from jax import lax
from jax.experimental import pallas as pl
from jax.experimental.pallas import tpu as pltpu

F32 = jnp.float32
BF16 = jnp.bfloat16
HI = lax.Precision.HIGHEST

GRID_W = 64
HEAD_DIM = 64
WIN_ROWS = 8
WIN_COLS = 16
N_EXPERTS = 32
TOP_K = 4
SWIGLU_LIMIT = 7.0
SWIGLU_ALPHA = 1.702
LN_EPS = 1e-5
GN_EPS = HEAD_DIM * 1e-5
DECAY_LORA = 64
ICLR_LORA = 64
GATE_LORA = 128
VRES_LORA = 32

LANES = 128
SUBLANES = 8
VMEM_LIMIT = 56 * 1024 * 1024

CHUNK = 64
NEG_BIG = -1e30
ROUTE_TB = 512


def _cp(sem, vmem=VMEM_LIMIT):
    return pltpu.CompilerParams(dimension_semantics=sem, vmem_limit_bytes=vmem)


def _dot(a, b, prec=HI):
    return jnp.dot(a, b, precision=prec, preferred_element_type=F32)


def _bdot(a, b):
    return jnp.dot(a.astype(BF16), b.astype(BF16), preferred_element_type=F32)


def _bdot_nt(a, b):
    return lax.dot_general(a.astype(BF16), b.astype(BF16), (((1,), (1,)), ((), ())),
                           preferred_element_type=F32)


def _sigmoid(x):
    return 1.0 / (1.0 + jnp.exp(-x))


def _split2(x):
    x1 = x.astype(BF16)
    x2 = (x - x1.astype(F32)).astype(BF16)
    return x1, x2


def _lhs3(x):
    x1, x2 = _split2(x)
    return jnp.concatenate([x1, x2, x1], axis=1)


def _rhs3(y):
    y1, y2 = _split2(y)
    return jnp.concatenate([y1, y1, y2], axis=0)


def _mm3(a, b):
    return jnp.dot(_lhs3(a), _rhs3(b), preferred_element_type=F32)


def _head_sum(x, ones3):
    outs = []
    for j in range(x.shape[1] // LANES):
        xs = x[:, j * LANES:(j + 1) * LANES]
        x1 = xs.astype(BF16)
        r1 = xs - x1.astype(F32)
        x2 = r1.astype(BF16)
        x3 = (r1 - x2.astype(F32)).astype(BF16)
        outs.append(jnp.dot(jnp.concatenate([x1, x2, x3], axis=1), ones3,
                            preferred_element_type=F32))
    return jnp.concatenate(outs, axis=1)


def _ada_kernel(c_ref, w_ref, b_ref, o_ref):
    c = c_ref[...]
    o_ref[0] = _dot(c * _sigmoid(c), w_ref[0]) + b_ref[0]


def _ada_all(cond, w_ada, b_ada):
    n_layer, d, n = w_ada.shape
    tn = 1536
    return pl.pallas_call(
        _ada_kernel,
        out_shape=jax.ShapeDtypeStruct((n_layer, SUBLANES, n), F32),
        grid=(n_layer, n // tn),
        in_specs=[pl.BlockSpec((SUBLANES, d), lambda l, j: (0, 0)),
                  pl.BlockSpec((1, d, tn), lambda l, j: (l, 0, j)),
                  pl.BlockSpec((1, 1, tn), lambda l, j: (l, 0, j))],
        out_specs=pl.BlockSpec((1, SUBLANES, tn), lambda l, j: (l, 0, j)),
        compiler_params=_cp(("arbitrary", "arbitrary")),
        name="ada",
    )(cond, w_ada, b_ada.reshape(n_layer, 1, n))


def _inproj_kernel(x_ref, sc_ref, sh_ref, wf_ref, wr_ref, wl_ref, wq_ref,
                   of_ref, or_ref, ol_ref, oq_ref):
    xm = (x_ref[0] * (1.0 + sc_ref[0]) + sh_ref[0]).astype(BF16)
    of_ref[0] = jnp.dot(xm, wf_ref[...], preferred_element_type=F32)
    or_ref[0] = jnp.dot(xm, wr_ref[...], preferred_element_type=F32)
    ol_ref[0] = jnp.dot(xm, wl_ref[...], preferred_element_type=F32)
    oq_ref[0] = jnp.dot(xm, wq_ref[...], preferred_element_type=F32)


def _inproj(x, sc, sh, wf, wr, wl, wq, tm):
    g, m, d = x.shape
    ws = (wf, wr, wl, wq)
    mod = pl.BlockSpec((1, 1, d), lambda b, i: (b, 0, 0))
    return pl.pallas_call(
        _inproj_kernel,
        out_shape=[jax.ShapeDtypeStruct((g, m, w.shape[1]), F32) for w in ws],
        grid=(g, m // tm),
        in_specs=[pl.BlockSpec((1, tm, d), lambda b, i: (b, i, 0)), mod, mod]
                 + [pl.BlockSpec(w.shape, lambda b, i: (0, 0)) for w in ws],
        out_specs=[pl.BlockSpec((1, tm, w.shape[1]), lambda b, i: (b, i, 0)) for w in ws],
        compiler_params=_cp(("parallel", "parallel")),
        name="inproj",
    )(x, sc, sh, *ws)


def _dft_consts(t_len, n1, n2, tb2, width):
    groups = width // HEAD_DIM
    j = np.arange(HEAD_DIM)
    ang = 2.0 * np.pi * np.outer(j, j) / HEAD_DIM
    eye = np.eye(groups)
    cbd = np.kron(eye, np.cos(ang)) / np.sqrt(HEAD_DIM)
    sbd = np.kron(eye, np.sin(ang)) / np.sqrt(HEAD_DIM)
    a1 = 2.0 * np.pi * np.outer(np.arange(n1), np.arange(n1)) / n1
    c1, s1 = np.cos(a1), np.sin(a1)
    atw = 2.0 * np.pi * np.outer(np.arange(n1), np.arange(n2)) / t_len
    twc = np.cos(atw).reshape(n1, n2 // tb2, tb2).transpose(1, 0, 2)
    tws = np.sin(atw).reshape(n1, n2 // tb2, tb2).transpose(1, 0, 2)
    a2 = 2.0 * np.pi * np.outer(np.arange(n2), np.arange(n2)) / n2
    c2, s2 = np.cos(a2) / np.sqrt(t_len), np.sin(a2) / np.sqrt(t_len)
    return [jnp.asarray(v, F32) for v in (cbd, sbd, c1, s1, twc, tws, c2, s2)]


def _four_a_kernel(tb2, width, x_ref, cbd_ref, sbd_ref, c1_ref, s1_ref, twc_ref, tws_ref,
                   yr_ref, yi_ref):
    c1 = c1_ref[...]
    s1 = s1_ref[...]
    cbd = cbd_ref[...]
    sbd = sbd_ref[...]
    for jj in range(tb2):
        x = x_ref[0, :, width * jj:width * (jj + 1)]
        zr = _dot(x, cbd)
        zi = -_dot(x, sbd)
        ar = _dot(c1, zr) + _dot(s1, zi)
        ai = _dot(c1, zi) - _dot(s1, zr)
        tc = twc_ref[0, :, jj:jj + 1]
        ts = tws_ref[0, :, jj:jj + 1]
        yr_ref[0, jj] = tc * ar + ts * ai
        yi_ref[0, jj] = tc * ai - ts * ar


def _four_b_kernel(yr_ref, yi_ref, c2_ref, s2_ref, o_ref):
    o_ref[0] = _dot(c2_ref[...], yr_ref[0]) + _dot(s2_ref[...], yi_ref[0])


def _fourier(f, n1, n2):
    bsz, t_len, width = f.shape
    tb2 = SUBLANES
    cbd, sbd, c1, s1, twc, tws, c2, s2 = _dft_consts(t_len, n1, n2, tb2, width)
    const = lambda a: pl.BlockSpec(a.shape, lambda b, j: (0,) * a.ndim)
    yr, yi = pl.pallas_call(
        functools.partial(_four_a_kernel, tb2, width),
        out_shape=[jax.ShapeDtypeStruct((bsz, n2, n1, width), F32)] * 2,
        grid=(bsz, n2 // tb2),
        in_specs=[pl.BlockSpec((1, n1, tb2 * width), lambda b, j: (b, 0, j)),
                  const(cbd), const(sbd), const(c1), const(s1),
                  pl.BlockSpec((1, n1, tb2), lambda b, j: (j, 0, 0)),
                  pl.BlockSpec((1, n1, tb2), lambda b, j: (j, 0, 0))],
        out_specs=[pl.BlockSpec((1, tb2, n1, width), lambda b, j: (b, j, 0, 0))] * 2,
        compiler_params=_cp(("parallel", "parallel")),
        name="fourier_a",
    )(f.reshape(bsz, n1, n2 * width), cbd, sbd, c1, s1, twc, tws)
    ncol = n1 * width
    tc = min(ncol, 2048)
    out = pl.pallas_call(
        _four_b_kernel,
        out_shape=jax.ShapeDtypeStruct((bsz, n2, ncol), F32),
        grid=(bsz, ncol // tc),
        in_specs=[pl.BlockSpec((1, n2, tc), lambda b, j: (b, 0, j)),
                  pl.BlockSpec((1, n2, tc), lambda b, j: (b, 0, j)),
                  const(c2), const(s2)],
        out_specs=pl.BlockSpec((1, n2, tc), lambda b, j: (b, 0, j)),
        compiler_params=_cp(("parallel", "parallel")),
        name="fourier_b",
    )(yr.reshape(bsz, n2, ncol), yi.reshape(bsz, n2, ncol), c2, s2)
    return out.reshape(bsz, t_len, width)


def _split_len(t_len):
    n1 = 1 << ((t_len.bit_length() - 1 + 1) // 2)
    return n1, t_len // n1


def _head_ones3():
    h = np.arange(LANES) // HEAD_DIM
    one = (h[:, None] == h[None, :]).astype(np.float32)
    return jnp.asarray(np.concatenate([one, one, one], axis=0), BF16)


def _rwkv_prep_kernel(first, x_ref, xp_ref, xn_ref, lo_ref, tsp_ref, tsn_ref, kk_ref,
                      ones_ref, *rest):
    if first:
        o_ref, vf_out_ref = rest
    else:
        v0_ref, v2_ref, vf_ref, o_ref = rest
    i = pl.program_id(1)
    last = pl.num_programs(1) - 1
    x = x_ref[0]
    tb, w3 = x.shape
    rw = w3 // 3
    row = lax.broadcasted_iota(jnp.int32, (tb, 1), 0)
    prev_edge = jnp.where(i == 0, 0.0, xp_ref[0, SUBLANES - 1:SUBLANES, :])
    next_edge = jnp.where(i == last, 0.0, xn_ref[0, 0:1, :])
    prev = jnp.where(row == 0, prev_edge, pltpu.roll(x, 1, 0))
    nxt = jnp.where(row == tb - 1, next_edge, pltpu.roll(x, tb - 1, 0))
    s = x + tsp_ref[...] * (prev - x) + tsn_ref[...] * (nxt - x)
    r = s[:, 0:rw]
    k = s[:, rw:2 * rw]
    v = s[:, 2 * rw:3 * rw]
    if first:
        vf_out_ref[0] = v
    else:
        mix = _sigmoid(v0_ref[...] + _mm3(lo_ref[0], v2_ref[...]))
        v = v + (vf_ref[0] - v) * mix
    kk = k * kk_ref[...]
    ss = _head_sum(kk * kk, ones_ref[...])
    kk = kk * lax.rsqrt(jnp.maximum(ss, 1e-24))
    o_ref[0, :, 0:rw] = r
    o_ref[0, :, rw:2 * rw] = k
    o_ref[0, :, 2 * rw:3 * rw] = v
    o_ref[0, :, 3 * rw:4 * rw] = kk


def _rwkv_prep(rkv, lora, ts_prev, ts_next, k_k, v_first, v0, v2pad, tb):
    g, m, w3 = rkv.shape
    rw = w3 // 3
    first = v_first is None
    nb8 = tb // SUBLANES
    n8 = m // SUBLANES
    ones3 = _head_ones3()
    row = lambda a: a.reshape(1, -1)
    vec = lambda n: pl.BlockSpec((1, n), lambda b, i: (0, 0))
    in_specs = [
        pl.BlockSpec((1, tb, w3), lambda b, i: (b, i, 0)),
        pl.BlockSpec((1, SUBLANES, w3), lambda b, i: (b, jnp.maximum(i * nb8 - 1, 0), 0)),
        pl.BlockSpec((1, SUBLANES, w3), lambda b, i: (b, jnp.minimum((i + 1) * nb8, n8 - 1), 0)),
        pl.BlockSpec((1, tb, LANES), lambda b, i: (b, i, 4)),
        vec(w3), vec(w3), vec(rw),
        pl.BlockSpec(ones3.shape, lambda b, i: (0, 0)),
    ]
    args = [rkv, rkv, rkv, lora, row(ts_prev), row(ts_next), row(k_k), ones3]
    out_main = jax.ShapeDtypeStruct((g, m, 4 * rw), F32)
    spec_main = pl.BlockSpec((1, tb, 4 * rw), lambda b, i: (b, i, 0))
    spec_v = pl.BlockSpec((1, tb, rw), lambda b, i: (b, i, 0))
    if first:
        out_shape = [out_main, jax.ShapeDtypeStruct((g, m, rw), F32)]
        out_specs = [spec_main, spec_v]
    else:
        in_specs += [vec(rw), pl.BlockSpec((LANES, rw), lambda b, i: (0, 0)), spec_v]
        args += [row(v0), v2pad, v_first]
        out_shape = out_main
        out_specs = spec_main
    res = pl.pallas_call(
        functools.partial(_rwkv_prep_kernel, first),
        out_shape=out_shape, grid=(g, m // tb), in_specs=in_specs, out_specs=out_specs,
        compiler_params=_cp(("parallel", "parallel")),
        name="rwkv_prep",
    )(*args)
    if first:
        return res[0], res[1]
    return res, v_first


def _stack_heads(x, lo_mask):
    return jnp.concatenate([jnp.where(lo_mask, x, 0.0), jnp.where(lo_mask, 0.0, x)], axis=0)


def _scan_masks(reverse):
    L = CHUNK
    L2, L4 = 2 * L, 4 * L
    ti = lax.broadcasted_iota(jnp.int32, (L4, L4), 0)
    si = lax.broadcasted_iota(jnp.int32, (L4, L4), 1)
    tq = ti % L2
    sq = si % L2
    tt = tq % L
    st = sq % L
    before = (st > tt) if reverse else (st < tt)
    keep = ((tq // L) == (sq // L)) & (before | ((ti >= L2) & (st == tt)))
    ci = lax.broadcasted_iota(jnp.int32, (L, L), 0)
    cj = lax.broadcasted_iota(jnp.int32, (L, L), 1)
    tri = jnp.where((cj >= ci) if reverse else (cj <= ci), 1.0, 0.0).astype(BF16)
    return keep, jnp.concatenate([tri, tri, tri], axis=1)


def _rwkv_bidir_kernel(pf_ref, pb_ref, lof_ref, lob_ref, s0_ref, w0_ref, w2_ref, a0_ref, a2_ref,
                       g2_ref, ka_ref, rk_ref, gng_ref, gnb_ref, ones_ref,
                       of_ref, ob_ref, so_ref, s_scr):
    i = pl.program_id(0)
    L = CHUNK
    L2, L4 = 2 * L, 4 * L
    bsz = pf_ref.shape[0]
    rw = pf_ref.shape[2] // 4
    n_pair = rw // LANES

    @pl.when(i == 0)
    def _():
        s_scr[...] = s0_ref[...]

    ones3 = ones_ref[...]
    lo_mask = lax.broadcasted_iota(jnp.int32, (1, LANES), 1) < HEAD_DIM
    ei = lax.broadcasted_iota(jnp.int32, (L2, L2), 0)
    ej = lax.broadcasted_iota(jnp.int32, (L2, L2), 1)
    eye = jnp.where(ei == ej, 1.0, 0.0)
    masks = (_scan_masks(False), _scan_masks(True))

    streams = []
    for dr, (p_ref, lo_ref) in enumerate(((pf_ref, lof_ref), (pb_ref, lob_ref))):
        for b in range(bsz):
            p = p_ref[b]
            lo = lo_ref[b]
            r = p[:, 0:rw]
            k = p[:, rw:2 * rw]
            v = p[:, 2 * rw:3 * rw]
            kk = p[:, 3 * rw:4 * rw]
            zw = w0_ref[dr] + _bdot(jnp.tanh(lo[:, 0:LANES]), w2_ref[dr])
            log_w = -(jnp.maximum(-zw, 0.0) + jnp.log(1.0 + jnp.exp(-jnp.abs(zw)))) - 0.5
            lw = -jnp.exp(log_w)
            iclr = _sigmoid(a0_ref[dr] + _bdot(lo[:, LANES:2 * LANES], a2_ref[dr]))
            kd = k * (1.0 + (iclr - 1.0) * ka_ref[...])
            gate = _bdot(_sigmoid(lo[:, (2 + dr) * LANES:(3 + dr) * LANES]), g2_ref[dr])
            bonus = _head_sum(r * kd * rk_ref[...], ones3) * v
            streams.append(dict(dr=dr, b=b, r=r, kd=kd, v=v, av=-kk, bv=kk * iclr, lw=lw,
                                gate=gate, bonus=bonus))

    chains = [(si_, j) for si_ in range(len(streams)) for j in range(n_pair)]

    def cols(si_, j, name):
        return streams[si_][name][:, j * LANES:(j + 1) * LANES]

    ops = []
    for si_, j in chains:
        dr = streams[si_]['dr']
        keep, tri3 = masks[dr]
        lwc = cols(si_, j, 'lw')
        l1 = lwc.astype(BF16)
        lr = lwc - l1.astype(F32)
        l2 = lr.astype(BF16)
        l3 = (lr - l2.astype(F32)).astype(BF16)
        cum = jnp.dot(tri3, jnp.concatenate([l1, l2, l3], axis=0), preferred_element_type=F32)
        end_row = 0 if dr else L - 1
        tot = cum[end_row:end_row + 1, :]
        g_inv = jnp.exp(-cum)
        g_end = jnp.exp(tot - cum)
        bvc = cols(si_, j, 'bv')
        kdc = cols(si_, j, 'kd')
        stk = lambda x: _stack_heads(x, lo_mask).astype(BF16)
        ops.append(dict(
            keep=keep, tot=tot,
            ar_s=jnp.concatenate([stk(cols(si_, j, 'av') * jnp.exp(cum - lwc)),
                                  stk(cols(si_, j, 'r') * jnp.exp(cum))], axis=0),
            bk_s=jnp.concatenate([stk(bvc * g_inv), stk(kdc * g_inv)], axis=0),
            v_s=stk(cols(si_, j, 'v')),
            vt_s=_stack_heads(cols(si_, j, 'v'), lo_mask).T.astype(BF16),
            bkh_s=jnp.concatenate([stk(bvc * g_end), stk(kdc * g_end)], axis=0)))

    dot = functools.partial(jnp.dot, preferred_element_type=F32)
    grams = [jnp.where(o['keep'], _bdot_nt(o['ar_s'], o['bk_s']), 0.0) for o in ops]
    tinv = [eye + g[0:L2, 0:L2] for g in grams]
    apb = [g[0:L2, 0:L2].astype(BF16) for g in grams]
    for _ in range(5):
        apb = [dot(a, a).astype(BF16) for a in apb]
        tinv = [t + dot(t.astype(BF16), a) for t, a in zip(tinv, apb)]
    x1 = [dot(g[0:L2, L2:L4].astype(BF16), o['v_s']) for g, o in zip(grams, ops)]
    wu = [dot(t.astype(BF16), jnp.concatenate([o['ar_s'][0:L2], x.astype(BF16)], axis=1))
          for t, o, x in zip(tinv, ops, x1)]
    s_old = [s_scr[c] for c in range(len(chains))]
    hy = [_bdot_nt(jnp.concatenate([w[:, 0:L2].astype(BF16), o['ar_s'][L2:L4]], axis=0), s)
          for w, o, s in zip(wu, ops, s_old)]
    u = [h[0:L2] + w[:, L2:L4] for h, w in zip(hy, wu)]
    uv = [jnp.concatenate([uu.astype(BF16), o['v_s']], axis=0) for uu, o in zip(u, ops)]
    ys = [h[L2:L4] + dot(g[L2:L4, :].astype(BF16), x) for h, g, x in zip(hy, grams, uv)]
    for c, (s, uu, o) in enumerate(zip(s_old, u, ops)):
        s_scr[c] = s * jnp.exp(o['tot']) + dot(
            jnp.concatenate([uu.T.astype(BF16), o['vt_s']], axis=1), o['bkh_s'])

    for si_, st in enumerate(streams):
        y = jnp.concatenate([ys[si_ * n_pair + j][0:L] + ys[si_ * n_pair + j][L:L2]
                             for j in range(n_pair)], axis=1)
        mu = _head_sum(y, ones3) * (1.0 / HEAD_DIM)
        d = y - mu
        var = _head_sum(d * d, ones3) * (1.0 / HEAD_DIM)
        out = (d * lax.rsqrt(var + GN_EPS) * gng_ref[...] + gnb_ref[...] + st['bonus']) * st['gate']
        if st['dr'] == 0:
            of_ref[st['b']] = out
        else:
            ob_ref[st['b']] = out

    @pl.when(i == pl.num_programs(0) - 1)
    def _():
        so_ref[...] = s_scr[...]


def _rwkv_bidir(prep, lora, state0, prm):
    g, m, w4 = prep.shape
    rw = w4 // 4
    nblk = m // CHUNK
    w0, w2pad, a0, a2pad, g2, k_a, r_k, gn_g, gn_b = prm
    ones3 = _head_ones3()
    row = lambda a: a.reshape(1, -1)
    fmap = lambda i: (0, i, 0)
    bmap = lambda i: (0, nblk - 1 - i, 0)
    full = lambda a: pl.BlockSpec(a.shape, lambda i: (0,) * a.ndim)
    w0, a0 = w0[:, None, :], a0[:, None, :]
    consts = [w0, w2pad, a0, a2pad, g2, row(k_a), row(r_k), row(gn_g), row(gn_b), ones3]
    st_spec = pl.BlockSpec(state0.shape, lambda i: (0, 0, 0))
    y_f, y_b, s_out = pl.pallas_call(
        _rwkv_bidir_kernel,
        out_shape=[jax.ShapeDtypeStruct((g, m, rw), F32), jax.ShapeDtypeStruct((g, m, rw), F32),
                   jax.ShapeDtypeStruct(state0.shape, F32)],
        grid=(nblk,),
        in_specs=[pl.BlockSpec((g, CHUNK, w4), fmap), pl.BlockSpec((g, CHUNK, w4), bmap),
                  pl.BlockSpec((g, CHUNK, lora.shape[2]), fmap),
                  pl.BlockSpec((g, CHUNK, lora.shape[2]), bmap),
                  st_spec] + [full(a) for a in consts],
        out_specs=[pl.BlockSpec((g, CHUNK, rw), fmap), pl.BlockSpec((g, CHUNK, rw), bmap), st_spec],
        scratch_shapes=[pltpu.VMEM(state0.shape, F32)],
        compiler_params=_cp(("arbitrary",)),
        name="rwkv_scan",
    )(prep, prep, lora, lora, state0, *consts)
    return y_f, y_b, s_out


def _na_bias_table(rpb):
    n_head = rpb.shape[0]
    col = np.arange(GRID_W)
    col_start = np.clip(col - WIN_COLS // 2, 0, GRID_W - WIN_COLS)
    col_in = (col[None, :] >= col_start[:, None]) & (col[None, :] < col_start[:, None] + WIN_COLS)
    col_rel = np.clip(col[None, :] - col[:, None] + WIN_COLS - 1, 0, 2 * WIN_COLS - 2)
    n_rel = 2 * WIN_COLS - 1
    sel = (col_rel.reshape(-1)[None, :] == np.arange(n_rel)[:, None]).astype(np.float32)
    c = jnp.einsum('hrc,cn->hrn', rpb, jnp.asarray(sel), precision=HI)
    c = c.reshape(n_head, 2 * WIN_ROWS - 1, GRID_W, GRID_W)
    c = jnp.where(col_in[None, None], c, NEG_BIG)
    t = jnp.stack([c[:, d:d + WIN_ROWS] for d in range(WIN_ROWS)], axis=1)
    t = t.transpose(0, 1, 3, 2, 4).reshape(n_head // 2, 2, WIN_ROWS, GRID_W, WIN_ROWS * GRID_W)
    return t.astype(F32)


def _na_kernel(rows, q_ref, kp_ref, kc_ref, kn_ref, vp_ref, vc_ref, vn_ref, kx_ref, vx_ref,
               b_ref, o_ref, kcat, vcat):
    i = pl.program_id(2)
    w = GRID_W
    blk = WIN_ROWS * w
    scale = HEAD_DIM ** -0.5
    kcat[0:blk] = kp_ref[0].astype(BF16)
    kcat[blk:2 * blk] = kc_ref[0].astype(BF16)
    kcat[2 * blk:3 * blk] = kn_ref[0].astype(BF16)
    vcat[0:blk] = vp_ref[0].astype(BF16)
    vcat[blk:2 * blk] = vc_ref[0].astype(BF16)
    vcat[2 * blk:3 * blk] = vn_ref[0].astype(BF16)
    kx = kx_ref[0].astype(BF16)
    vx = vx_ref[0].astype(BF16)
    lo_mask = lax.broadcasted_iota(jnp.int32, (1, LANES), 1) < HEAD_DIM
    rng = range(WIN_ROWS)
    r0 = [jnp.clip(i * WIN_ROWS + rr - WIN_ROWS // 2, 0, rows - WIN_ROWS) for rr in rng]
    off = [pl.multiple_of((r0[rr] - (i - 1) * WIN_ROWS) * w, w) for rr in rng]
    dlt = [r0[rr] - (i * WIN_ROWS + rr) + WIN_ROWS - 1 for rr in rng]
    qs = [_stack_heads(q_ref[0, rr * w:(rr + 1) * w, :], lo_mask).astype(BF16) for rr in rng]
    s = [_bdot_nt(qs[rr], kcat[pl.ds(off[rr], blk), :]) * scale
         + jnp.concatenate([b_ref[0, 0, dlt[rr]], b_ref[0, 1, dlt[rr]]], axis=0) for rr in rng]
    sx = [_bdot_nt(q, kx) * scale for q in qs]
    mx = [jnp.maximum(jnp.max(a, axis=-1, keepdims=True), jnp.max(b, axis=-1, keepdims=True))
          for a, b in zip(s, sx)]
    p = [jnp.exp(a - m) for a, m in zip(s, mx)]
    px = [jnp.exp(a - m) for a, m in zip(sx, mx)]
    den = [jnp.sum(a, axis=-1, keepdims=True) + jnp.sum(b, axis=-1, keepdims=True)
           for a, b in zip(p, px)]
    o = [(_bdot(p[rr], vcat[pl.ds(off[rr], blk), :]) + _bdot(px[rr], vx)) / den[rr] for rr in rng]
    for rr in rng:
        o_ref[0, rr * w:(rr + 1) * w, :] = jnp.where(lo_mask, o[rr][0:w], o[rr][w:2 * w])


def _na_attention(qkv, qkv_ctx, bias_tab):
    bsz, t_len, w3 = qkv.shape
    na = w3 // 3
    n_pair = na // LANES
    rows = t_len // GRID_W
    blk = WIN_ROWS * GRID_W
    nblk = rows // WIN_ROWS
    c_len = qkv_ctx.shape[1]
    kv = lambda sel, shift: pl.BlockSpec(
        (1, blk, LANES), lambda b, j, i: (b, jnp.clip(i + shift, 0, nblk - 1), sel * n_pair + j))
    cx = lambda sel: pl.BlockSpec((1, c_len, LANES), lambda b, j, i: (b, 0, sel * n_pair + j))
    return pl.pallas_call(
        functools.partial(_na_kernel, rows),
        out_shape=jax.ShapeDtypeStruct((bsz, t_len, na), F32),
        grid=(bsz, n_pair, nblk),
        in_specs=[kv(0, 0), kv(1, -1), kv(1, 0), kv(1, 1), kv(2, -1), kv(2, 0), kv(2, 1),
                  cx(1), cx(2),
                  pl.BlockSpec((1, 2, WIN_ROWS, GRID_W, blk), lambda b, j, i: (j, 0, 0, 0, 0))],
        out_specs=pl.BlockSpec((1, blk, LANES), lambda b, j, i: (b, i, j)),
        scratch_shapes=[pltpu.VMEM((3 * blk, LANES), BF16), pltpu.VMEM((3 * blk, LANES), BF16)],
        compiler_params=_cp(("parallel", "parallel", "arbitrary")),
        name="na_attention",
    )(qkv, qkv, qkv, qkv, qkv, qkv, qkv, qkv_ctx, qkv_ctx, bias_tab)


def _ctx_attn_kernel(q_ref, k_ref, v_ref, o_ref):
    lo_mask = lax.broadcasted_iota(jnp.int32, (1, LANES), 1) < HEAD_DIM
    q = q_ref[0]
    c = q.shape[0]
    qs = _stack_heads(q, lo_mask)
    s = _bdot_nt(qs, k_ref[0]) * (HEAD_DIM ** -0.5)
    mx = jnp.max(s, axis=-1, keepdims=True)
    p = jnp.exp(s - mx)
    o = _bdot(p, v_ref[0]) / jnp.sum(p, axis=-1, keepdims=True)
    o_ref[0] = jnp.where(lo_mask, o[0:c], o[c:2 * c])


def _ctx_attention(qkv_ctx):
    bsz, c_len, w3 = qkv_ctx.shape
    na = w3 // 3
    n_pair = na // LANES
    sp = lambda sel: pl.BlockSpec((1, c_len, LANES), lambda b, j: (b, 0, sel * n_pair + j))
    return pl.pallas_call(
        _ctx_attn_kernel,
        out_shape=jax.ShapeDtypeStruct((bsz, c_len, na), F32),
        grid=(bsz, n_pair),
        in_specs=[sp(0), sp(1), sp(2)],
        out_specs=pl.BlockSpec((1, c_len, LANES), lambda b, j: (b, 0, j)),
        compiler_params=_cp(("parallel", "parallel")),
        name="ctx_attention",
    )(qkv_ctx, qkv_ctx, qkv_ctx)


def _layer_norm(z, g, b):
    mu = jnp.mean(z, axis=-1, keepdims=True)
    d = z - mu
    var = jnp.mean(d * d, axis=-1, keepdims=True)
    return d * lax.rsqrt(var + LN_EPS) * g + b


def _outproj_kernel(alpha, f_ref, rwf_ref, rwb_ref, at_ref, x_ref, ga_ref, sc_ref, sh_ref, wfm_ref,
                    wo_ref, g_ref, b_ref, wr_ref, br_ref, xo_ref, h_ref, lg_ref):
    nf = f_ref.shape[2]
    nr = rwf_ref.shape[2]
    fm = jnp.dot(f_ref[0].astype(BF16), wfm_ref[...], preferred_element_type=F32)
    rwo = (rwf_ref[0] + rwb_ref[0]).astype(BF16)
    mix = (jnp.dot(fm.astype(BF16), wo_ref[0:nf, :], preferred_element_type=F32)
           + jnp.dot(rwo, wo_ref[nf:nf + nr, :], preferred_element_type=F32)
           + jnp.dot(at_ref[0].astype(BF16), wo_ref[nf + nr:, :], preferred_element_type=F32))
    xn = _layer_norm(alpha * x_ref[0] + ga_ref[0] * mix, g_ref[...], b_ref[...])
    xo_ref[0] = xn
    h = xn * (1.0 + sc_ref[0]) + sh_ref[0]
    h_ref[0] = h
    lg_ref[0] = _mm3(h, wr_ref[...]) + br_ref[...]


def _outproj(alpha, four, rwo, rwb, att, x, ga, sc, sh, wfm, wo, ln_g, ln_b, wr_pad, br_pad, tm):
    g, m, d = x.shape
    blk = lambda a: pl.BlockSpec((1, tm, a.shape[2]), lambda b, i: (b, i, 0))
    mod = pl.BlockSpec((1, 1, d), lambda b, i: (b, 0, 0))
    cst = lambda a: pl.BlockSpec(a.shape, lambda b, i: (0, 0))
    row = lambda a: a.reshape(1, -1)
    ln_g, ln_b, br_pad = row(ln_g), row(ln_b), row(br_pad)
    return pl.pallas_call(
        functools.partial(_outproj_kernel, alpha),
        out_shape=[jax.ShapeDtypeStruct((g, m, d), F32), jax.ShapeDtypeStruct((g, m, d), F32),
                   jax.ShapeDtypeStruct((g, m, LANES), F32)],
        grid=(g, m // tm),
        in_specs=[blk(four), blk(rwo), blk(rwb), blk(att), blk(x), mod, mod, mod, cst(wfm), cst(wo),
                  cst(ln_g), cst(ln_b), cst(wr_pad), cst(br_pad)],
        out_specs=[pl.BlockSpec((1, tm, d), lambda b, i: (b, i, 0)),
                   pl.BlockSpec((1, tm, d), lambda b, i: (b, i, 0)),
                   pl.BlockSpec((1, tm, LANES), lambda b, i: (b, i, 0))],
        compiler_params=_cp(("parallel", "parallel")),
        name="outproj_ln",
    )(four, rwo, rwb, att, x, ga, sc, sh, wfm, wo, ln_g, ln_b, wr_pad, br_pad)


def _ln2_kernel(alpha, x_ref, y0_ref, y1_ref, y2_ref, y3_ref, gt_ref, ga_ref, g_ref, b_ref,
                o_ref):
    gt = gt_ref[...]
    y = (gt[:, 4:5] * y0_ref[0] + gt[:, 5:6] * y1_ref[0]
         + gt[:, 6:7] * y2_ref[0] + gt[:, 7:8] * y3_ref[0])
    o_ref[0] = _layer_norm(alpha * x_ref[0] + ga_ref[0] * y, g_ref[...], b_ref[...])


def _ln2(alpha, x, yk, gates, row0, ga, ln_g, ln_b, tm):
    g, m, d = x.shape
    nb = m // tm
    base = row0 // tm
    blk = pl.BlockSpec((1, tm, d), lambda b, i: (b, i, 0))
    vec = pl.BlockSpec((1, d), lambda b, i: (0, 0))
    ysp = lambda k: pl.BlockSpec((1, tm, d), lambda b, i: (k, base + b * nb + i, 0))
    return pl.pallas_call(
        functools.partial(_ln2_kernel, alpha),
        out_shape=jax.ShapeDtypeStruct((g, m, d), F32),
        grid=(g, nb),
        in_specs=[blk, ysp(0), ysp(1), ysp(2), ysp(3),
                  pl.BlockSpec((tm, LANES), lambda b, i: (base + b * nb + i, 0)),
                  pl.BlockSpec((1, 1, d), lambda b, i: (b, 0, 0)), vec, vec],
        out_specs=blk,
        compiler_params=_cp(("parallel", "parallel")),
        name="ln2",
    )(x, yk, yk, yk, yk, gates, ga, ln_g.reshape(1, -1), ln_b.reshape(1, -1))


def _moe_kernel(be_ref, nu_ref, x_ref, wgu_ref, bgu_ref, wdn_ref, bdn_ref, o_ref, wgu_s, wdn_s):
    i = pl.program_id(0)

    @pl.when((i == 0) | (be_ref[i] != be_ref[jnp.maximum(i - 1, 0)]))
    def _():
        wgu_s[...] = wgu_ref[0].astype(BF16)
        wdn_s[...] = wdn_ref[0].astype(BF16)

    @pl.when(i < nu_ref[0])
    def _():
        de = wdn_s.shape[0]
        gu = jnp.dot(x_ref[...].astype(BF16), wgu_s[...], preferred_element_type=F32) + bgu_ref[0]
        gg = jnp.minimum(gu[:, 0:de], SWIGLU_LIMIT)
        uu = jnp.clip(gu[:, de:2 * de], -SWIGLU_LIMIT, SWIGLU_LIMIT)
        act = (uu + 1.0) * gg * _sigmoid(SWIGLU_ALPHA * gg)
        o_ref[...] = jnp.dot(act.astype(BF16), wdn_s[...], preferred_element_type=F32) + bdn_ref[0]


def _moe_experts(xs, block_e, n_used, wgu, bgu, wdn, bdn, tm):
    cap, d = xs.shape
    n_exp, _, de2 = wgu.shape
    de = de2 // 2
    n_blocks = cap // tm
    grid_spec = pltpu.PrefetchScalarGridSpec(
        num_scalar_prefetch=2,
        grid=(n_blocks,),
        in_specs=[pl.BlockSpec((tm, d), lambda i, be, nu: (i, 0)),
                  pl.BlockSpec((1, d, de2), lambda i, be, nu: (be[i], 0, 0)),
                  pl.BlockSpec((1, 1, de2), lambda i, be, nu: (be[i], 0, 0)),
                  pl.BlockSpec((1, de, d), lambda i, be, nu: (be[i], 0, 0)),
                  pl.BlockSpec((1, 1, d), lambda i, be, nu: (be[i], 0, 0))],
        out_specs=pl.BlockSpec((tm, d), lambda i, be, nu: (i, 0)),
        scratch_shapes=[pltpu.VMEM((d, de2), BF16), pltpu.VMEM((de, d), BF16)],
    )
    return pl.pallas_call(
        _moe_kernel,
        out_shape=jax.ShapeDtypeStruct((cap, d), F32),
        grid_spec=grid_spec,
        compiler_params=_cp(("arbitrary",)),
        name="moe_experts",
    )(block_e, n_used, xs, wgu, bgu.reshape(n_exp, 1, de2), wdn, bdn.reshape(n_exp, 1, d))


def _router_kernel(lg_ref, tri_ref, r_ref, cnt_ref, carry):
    i = pl.program_id(0)

    @pl.when(i == 0)
    def _():
        carry[...] = jnp.zeros_like(carry)

    lane = lax.broadcasted_iota(jnp.int32, lg_ref.shape, 1)
    lanef = lane.astype(F32)
    lg = jnp.where(lane < N_EXPERTS, lg_ref[...], NEG_BIG)
    tops, idxs, hots = [], [], []
    for _ in range(TOP_K):
        m = jnp.max(lg, axis=-1, keepdims=True)
        idx = jnp.min(jnp.where(lg == m, lanef, float(LANES)), axis=-1, keepdims=True)
        hot = lanef == idx
        lg = jnp.where(hot, NEG_BIG, lg)
        tops.append(m)
        idxs.append(idx)
        hots.append(hot)
    ex = [jnp.exp(m - tops[0]) for m in tops]
    den = ex[0] + ex[1] + ex[2] + ex[3]
    oh = jnp.where(hots[0] | hots[1] | hots[2] | hots[3], 1.0, 0.0)
    before = carry[...] + jnp.dot(tri_ref[...], oh.astype(BF16), preferred_element_type=F32)
    rec = jnp.zeros(lg_ref.shape, F32)
    for k in range(TOP_K):
        rank = jnp.sum(jnp.where(hots[k], before, 0.0), axis=-1, keepdims=True)
        rec = jnp.where(lane == k, idxs[k], rec)
        rec = jnp.where(lane == TOP_K + k, ex[k] / den, rec)
        rec = jnp.where(lane == 2 * TOP_K + k, rank, rec)
    r_ref[...] = rec
    carry[...] = carry[...] + jnp.sum(oh, axis=0, keepdims=True)
    cnt_ref[...] = carry[...]


def _route(logits):
    n_tok = logits.shape[0]
    tb = min(ROUTE_TB, n_tok)
    tri = jnp.asarray(np.tril(np.ones((tb, tb), np.float32), -1), BF16)
    return pl.pallas_call(
        _router_kernel,
        out_shape=[jax.ShapeDtypeStruct((n_tok, LANES), F32), jax.ShapeDtypeStruct((1, LANES), F32)],
        grid=(n_tok // tb,),
        in_specs=[pl.BlockSpec((tb, LANES), lambda i: (i, 0)), pl.BlockSpec((tb, tb), lambda i: (0, 0))],
        out_specs=[pl.BlockSpec((tb, LANES), lambda i: (i, 0)), pl.BlockSpec((1, LANES), lambda i: (0, 0))],
        scratch_shapes=[pltpu.VMEM((1, LANES), F32)],
        compiler_params=_cp(("arbitrary",)),
        name="router",
    )(logits, tri)


def _moe(h, logits, wgu, bgu, wdn, bdn, tm):
    n_tok, d = h.shape
    n_slot = n_tok * TOP_K
    rec, cnt = _route(logits)
    top_e = rec[:, 0:TOP_K].astype(jnp.int32)
    rank = rec[:, 2 * TOP_K:3 * TOP_K].astype(jnp.int32)
    counts = cnt[0, 0:N_EXPERTS].astype(jnp.int32)
    first = jnp.cumsum(counts) - counts
    padded = (counts + tm - 1) // tm * tm
    pad_end = jnp.cumsum(padded)
    pad_start = pad_end - padded
    n_blocks = -(-n_slot // tm) + N_EXPERTS
    cap = n_blocks * tm
    blk_row = jnp.arange(n_blocks, dtype=jnp.int32) * tm
    block_e = jnp.minimum(jnp.sum(pad_end[None, :] <= blk_row[:, None], axis=1),
                          N_EXPERTS - 1).astype(jnp.int32)
    n_used = (pad_end[-1:] // tm).astype(jnp.int32)
    experts = jnp.arange(N_EXPERTS, dtype=jnp.int32)
    pos = jnp.sum(jnp.where(top_e[..., None] == experts, pad_start, 0), axis=-1) + rank
    order = jnp.argsort(pos.reshape(-1)).astype(jnp.int32)
    row_e = jnp.repeat(block_e, tm)
    off = jnp.arange(cap, dtype=jnp.int32) - pad_start[row_e]
    src = jnp.clip(first[row_e] + off, 0, n_slot - 1)
    slot_tok = jnp.where(off < counts[row_e], order.at[src].get(mode='promise_in_bounds') // TOP_K, 0)
    xs = h.at[slot_tok].get(mode='promise_in_bounds')
    ys = _moe_experts(xs, block_e, n_used, wgu, bgu, wdn, bdn, tm)
    yk = ys.at[pos.T.reshape(-1)].get(mode='promise_in_bounds').reshape(TOP_K, n_tok, d)
    return yk, rec


def _pad_rows(w, start, total):
    return jnp.zeros((total, w.shape[1]), w.dtype).at[start:start + w.shape[0]].set(w)


def kernel(x, c, ctx, c_ctx, w_ada, b_ada, w_in, w_vdown, w_fmap, ts_prev, ts_next, rw_w0, rw_w2, rw_a0, rw_a2, rw_g2, rw_kk, rw_ka, rw_rk, rw_gn_g, rw_gn_b, rw_v0, rw_v2, na_rpb, w_out, ln1_g, ln1_b, w_router, b_router, w_gu, b_gu, w_dn, b_dn, ln2_g, ln2_b):
    bsz, n_lat, d = x.shape
    c_len = ctx.shape[1]
    depth = w_in.shape[0]
    f_dim = w_fmap.shape[1]
    rw_dim = rw_kk.shape[1]
    alpha = (2 * depth) ** 0.25
    n_pair = rw_dim // LANES

    tm_lat = min(512, n_lat)
    tm_ctx = min(256, c_len)
    tb_ctx = min(128, c_len)
    tm_moe = 512
    n1_lat, n2_lat = _split_len(n_lat)
    n1_ctx, n2_ctx = _split_len(c_len)

    cond = jnp.zeros((SUBLANES, d), F32).at[0:bsz].set(c).at[bsz].set(c_ctx)
    ada = _ada_all(cond, w_ada, b_ada)

    xl, xc = x, ctx
    vf_l = vf_c = None
    o_r = f_dim
    o_lora = f_dim + 3 * rw_dim
    o_q = o_lora + 2 * DECAY_LORA + 2 * ICLR_LORA + 2 * GATE_LORA
    for l in range(depth):
        last = l == depth - 1
        mods = ada[l].reshape(SUBLANES, 6, d)
        lat = lambda k: mods[0:bsz, k][:, None, :]
        cx = lambda k: jnp.broadcast_to(mods[bsz, k][None, None, :], (bsz, 1, d))

        wi = w_in[l]
        vdown = jnp.zeros((d, LANES), F32)
        if l > 0:
            vdown = vdown.at[:, 0:VRES_LORA].set(w_vdown[l - 1])
        wf = wi[:, 0:f_dim].astype(BF16)
        wr = wi[:, o_r:o_lora].astype(BF16)
        wl = jnp.concatenate([wi[:, o_lora:o_q], vdown], axis=1).astype(BF16)
        wq = wi[:, o_q:].astype(BF16)
        f_l, rkv_l, lora_l, qkv_l = _inproj(xl, lat(1), lat(0), wf, wr, wl, wq, tm_lat)
        f_c, rkv_c, lora_c, qkv_c = _inproj(xc, cx(1), cx(0), wf, wr, wl, wq, tm_ctx)

        if l == 0:
            v0 = v2pad = None
        else:
            v0 = rw_v0[l - 1]
            v2pad = _pad_rows(rw_v2[l - 1], 0, LANES)
        prep_c, vf_c = _rwkv_prep(rkv_c, lora_c, ts_prev[l], ts_next[l], rw_kk[l], vf_c, v0, v2pad,
                                  tb_ctx)
        prep_l, vf_l = _rwkv_prep(rkv_l, lora_l, ts_prev[l], ts_next[l], rw_kk[l], vf_l, v0, v2pad,
                                  tm_lat)
        prm = (rw_w0[l],
               jnp.stack([_pad_rows(rw_w2[l, dr], dr * DECAY_LORA, LANES) for dr in range(2)]),
               rw_a0[l],
               jnp.stack([_pad_rows(rw_a2[l, dr], dr * ICLR_LORA, LANES) for dr in range(2)]),
               rw_g2[l], rw_ka[l], rw_rk[l].reshape(-1), rw_gn_g[l], rw_gn_b[l])
        s0 = jnp.zeros((2 * bsz * n_pair, LANES, LANES), F32)
        rwf_c, rwb_c, s_ctx = _rwkv_bidir(prep_c, lora_c, s0, prm)
        rwf_l, rwb_l, _ = _rwkv_bidir(prep_l, lora_l, s_ctx, prm)

        att_l = _na_attention(qkv_l, qkv_c, _na_bias_table(na_rpb[l]))
        four_l = _fourier(f_l, n1_lat, n2_lat)

        wfm = w_fmap[l].astype(BF16)
        wo = w_out[l].astype(BF16)
        wr_pad = jnp.zeros((d, LANES), F32).at[:, 0:N_EXPERTS].set(w_router[l])
        br_pad = jnp.zeros((LANES,), F32).at[0:N_EXPERTS].set(b_router[l])
        xl, hl, lg_l = _outproj(alpha, four_l, rwf_l, rwb_l, att_l, xl, lat(2), lat(4), lat(3), wfm, wo,
                                ln1_g[l], ln1_b[l], wr_pad, br_pad, tm_lat)
        if last:
            yk, gates = _moe(hl.reshape(-1, d), lg_l.reshape(-1, LANES), w_gu[l], b_gu[l], w_dn[l],
                             b_dn[l], tm_moe)
            xl = _ln2(alpha, xl, yk, gates, 0, lat(5), ln2_g[l], ln2_b[l], tm_lat)
        else:
            att_c = _ctx_attention(qkv_c)
            four_c = _fourier(f_c, n1_ctx, n2_ctx)
            xc, hc, lg_c = _outproj(alpha, four_c, rwf_c, rwb_c, att_c, xc, cx(2), cx(4), cx(3), wfm, wo,
                                    ln1_g[l], ln1_b[l], wr_pad, br_pad, tm_ctx)
            h_all = jnp.concatenate([hl.reshape(-1, d), hc.reshape(-1, d)], axis=0)
            lg_all = jnp.concatenate([lg_l.reshape(-1, LANES), lg_c.reshape(-1, LANES)], axis=0)
            yk, gates = _moe(h_all, lg_all, w_gu[l], b_gu[l], w_dn[l], b_dn[l], tm_moe)
            n_l = bsz * n_lat
            xl = _ln2(alpha, xl, yk, gates, 0, lat(5), ln2_g[l], ln2_b[l], tm_lat)
            xc = _ln2(alpha, xc, yk, gates, n_l, cx(5), ln2_g[l], ln2_b[l], tm_ctx)
    return xl
```

```python
import functools

import numpy as np
import jax
import jax.numpy as jnp
from jax import lax
from jax.experimental import pallas as pl
from jax.experimental.pallas import tpu as pltpu

F32 = jnp.float32
BF16 = jnp.bfloat16
HI = lax.Precision.HIGHEST

GRID_W = 64
HEAD_DIM = 64
WIN_ROWS = 8
WIN_COLS = 16
N_EXPERTS = 32
TOP_K = 4
SWIGLU_LIMIT = 7.0
SWIGLU_ALPHA = 1.702
LN_EPS = 1e-5
GN_EPS = HEAD_DIM * 1e-5
DECAY_LORA = 64
ICLR_LORA = 64
GATE_LORA = 128
VRES_LORA = 32

LANES = 128
SUBLANES = 8
VMEM_LIMIT = 56 * 1024 * 1024

CHUNK = 64
NEG_BIG = -1e30
ROUTE_TB = 512
MOE_HIDDEN_CHUNK = 256


def _cp(sem, vmem=VMEM_LIMIT):
    return pltpu.CompilerParams(dimension_semantics=sem, vmem_limit_bytes=vmem)


def _dot(a, b, prec=HI):
    return jnp.dot(a, b, precision=prec, preferred_element_type=F32)


def _bdot(a, b):
    return jnp.dot(a.astype(BF16), b.astype(BF16), preferred_element_type=F32)


def _bdot_nt(a, b):
    return lax.dot_general(a.astype(BF16), b.astype(BF16), (((1,), (1,)), ((), ())),
                           preferred_element_type=F32)


def _sigmoid(x):
    return 1.0 / (1.0 + jnp.exp(-x))


def _split2(x):
    x1 = x.astype(BF16)
    x2 = (x - x1.astype(F32)).astype(BF16)
    return x1, x2


def _lhs3(x):
    x1, x2 = _split2(x)
    return jnp.concatenate([x1, x2, x1], axis=1)


def _rhs3(y):
    y1, y2 = _split2(y)
    return jnp.concatenate([y1, y1, y2], axis=0)


def _mm3(a, b):
    return jnp.dot(_lhs3(a), _rhs3(b), preferred_element_type=F32)


def _head_sum(x, ones3):
    outs = []
    for j in range(x.shape[1] // LANES):
        xs = x[:, j * LANES:(j + 1) * LANES]
        x1 = xs.astype(BF16)
        r1 = xs - x1.astype(F32)
        x2 = r1.astype(BF16)
        x3 = (r1 - x2.astype(F32)).astype(BF16)
        outs.append(jnp.dot(jnp.concatenate([x1, x2, x3], axis=1), ones3,
                            preferred_element_type=F32))
    return jnp.concatenate(outs, axis=1)


def _ada_kernel(c_ref, w_ref, b_ref, o_ref):
    c = c_ref[...]
    o_ref[0] = _dot(c * _sigmoid(c), w_ref[0]) + b_ref[0]


def _ada_all(cond, w_ada, b_ada):
    n_layer, d, n = w_ada.shape
    tn = 1536
    return pl.pallas_call(
        _ada_kernel,
        out_shape=jax.ShapeDtypeStruct((n_layer, SUBLANES, n), F32),
        grid=(n_layer, n // tn),
        in_specs=[pl.BlockSpec((SUBLANES, d), lambda l, j: (0, 0)),
                  pl.BlockSpec((1, d, tn), lambda l, j: (l, 0, j)),
                  pl.BlockSpec((1, 1, tn), lambda l, j: (l, 0, j))],
        out_specs=pl.BlockSpec((1, SUBLANES, tn), lambda l, j: (l, 0, j)),
        compiler_params=_cp(("arbitrary", "arbitrary")),
        name="ada",
    )(cond, w_ada, b_ada.reshape(n_layer, 1, n))


def _inproj_kernel(x_ref, sc_ref, sh_ref, wf_ref, wr_ref, wl_ref, wq_ref,
                   of_ref, or_ref, ol_ref, oq_ref):
    xm = (x_ref[0] * (1.0 + sc_ref[0]) + sh_ref[0]).astype(BF16)
    of_ref[0] = jnp.dot(xm, wf_ref[...], preferred_element_type=F32)
    or_ref[0] = jnp.dot(xm, wr_ref[...], preferred_element_type=F32)
    ol_ref[0] = jnp.dot(xm, wl_ref[...], preferred_element_type=F32)
    oq_ref[0] = jnp.dot(xm, wq_ref[...], preferred_element_type=F32)


def _inproj(x, sc, sh, wf, wr, wl, wq, tm):
    g, m, d = x.shape
    ws = (wf, wr, wl, wq)
    mod = pl.BlockSpec((1, 1, d), lambda b, i: (b, 0, 0))
    return pl.pallas_call(
        _inproj_kernel,
        out_shape=[jax.ShapeDtypeStruct((g, m, w.shape[1]), F32) for w in ws],
        grid=(g, m // tm),
        in_specs=[pl.BlockSpec((1, tm, d), lambda b, i: (b, i, 0)), mod, mod]
                 + [pl.BlockSpec(w.shape, lambda b, i: (0, 0)) for w in ws],
        out_specs=[pl.BlockSpec((1, tm, w.shape[1]), lambda b, i: (b, i, 0)) for w in ws],
        compiler_params=_cp(("parallel", "parallel")),
        name="inproj",
    )(x, sc, sh, *ws)


def _dft_consts(t_len, n1, n2, tb2, width):
    groups = width // HEAD_DIM
    j = np.arange(HEAD_DIM)
    ang = 2.0 * np.pi * np.outer(j, j) / HEAD_DIM
    eye = np.eye(groups)
    cbd = np.kron(eye, np.cos(ang)) / np.sqrt(HEAD_DIM)
    sbd = np.kron(eye, np.sin(ang)) / np.sqrt(HEAD_DIM)
    a1 = 2.0 * np.pi * np.outer(np.arange(n1), np.arange(n1)) / n1
    c1, s1 = np.cos(a1), np.sin(a1)
    atw = 2.0 * np.pi * np.outer(np.arange(n1), np.arange(n2)) / t_len
    twc = np.cos(atw).reshape(n1, n2 // tb2, tb2).transpose(1, 0, 2)
    tws = np.sin(atw).reshape(n1, n2 // tb2, tb2).transpose(1, 0, 2)
    a2 = 2.0 * np.pi * np.outer(np.arange(n2), np.arange(n2)) / n2
    c2, s2 = np.cos(a2) / np.sqrt(t_len), np.sin(a2) / np.sqrt(t_len)
    return [jnp.asarray(v, F32) for v in (cbd, sbd, c1, s1, twc, tws, c2, s2)]


def _four_a_kernel(tb2, width, x_ref, cbd_ref, sbd_ref, c1_ref, s1_ref, twc_ref, tws_ref,
                   yr_ref, yi_ref):
    c1 = c1_ref[...]
    s1 = s1_ref[...]
    cbd = cbd_ref[...]
    sbd = sbd_ref[...]
    for jj in range(tb2):
        x = x_ref[0, :, width * jj:width * (jj + 1)]
        zr = _dot(x, cbd)
        zi = -_dot(x, sbd)
        ar = _dot(c1, zr) + _dot(s1, zi)
        ai = _dot(c1, zi) - _dot(s1, zr)
        tc = twc_ref[0, :, jj:jj + 1]
        ts = tws_ref[0, :, jj:jj + 1]
        yr_ref[0, jj] = tc * ar + ts * ai
        yi_ref[0, jj] = tc * ai - ts * ar


def _four_b_kernel(yr_ref, yi_ref, c2_ref, s2_ref, o_ref):
    o_ref[0] = _dot(c2_ref[...], yr_ref[0]) + _dot(s2_ref[...], yi_ref[0])


def _fourier(f, n1, n2):
    bsz, t_len, width = f.shape
    tb2 = SUBLANES
    cbd, sbd, c1, s1, twc, tws, c2, s2 = _dft_consts(t_len, n1, n2, tb2, width)
    const = lambda a: pl.BlockSpec(a.shape, lambda b, j: (0,) * a.ndim)
    yr, yi = pl.pallas_call(
        functools.partial(_four_a_kernel, tb2, width),
        out_shape=[jax.ShapeDtypeStruct((bsz, n2, n1, width), F32)] * 2,
        grid=(bsz, n2 // tb2),
        in_specs=[pl.BlockSpec((1, n1, tb2 * width), lambda b, j: (b, 0, j)),
                  const(cbd), const(sbd), const(c1), const(s1),
                  pl.BlockSpec((1, n1, tb2), lambda b, j: (j, 0, 0)),
                  pl.BlockSpec((1, n1, tb2), lambda b, j: (j, 0, 0))],
        out_specs=[pl.BlockSpec((1, tb2, n1, width), lambda b, j: (b, j, 0, 0))] * 2,
        compiler_params=_cp(("parallel", "parallel")),
        name="fourier_a",
    )(f.reshape(bsz, n1, n2 * width), cbd, sbd, c1, s1, twc, tws)
    ncol = n1 * width
    tc = min(ncol, 2048)
    out = pl.pallas_call(
        _four_b_kernel,
        out_shape=jax.ShapeDtypeStruct((bsz, n2, ncol), F32),
        grid=(bsz, ncol // tc),
        in_specs=[pl.BlockSpec((1, n2, tc), lambda b, j: (b, 0, j)),
                  pl.BlockSpec((1, n2, tc), lambda b, j: (b, 0, j)),
                  const(c2), const(s2)],
        out_specs=pl.BlockSpec((1, n2, tc), lambda b, j: (b, 0, j)),
        compiler_params=_cp(("parallel", "parallel")),
        name="fourier_b",
    )(yr.reshape(bsz, n2, ncol), yi.reshape(bsz, n2, ncol), c2, s2)
    return out.reshape(bsz, t_len, width)


def _split_len(t_len):
    n1 = 1 << ((t_len.bit_length() - 1 + 1) // 2)
    return n1, t_len // n1


def _head_ones3():
    h = np.arange(LANES) // HEAD_DIM
    one = (h[:, None] == h[None, :]).astype(np.float32)
    return jnp.asarray(np.concatenate([one, one, one], axis=0), BF16)


def _rwkv_prep_kernel(first, x_ref, xp_ref, xn_ref, lo_ref, tsp_ref, tsn_ref, kk_ref,
                      ones_ref, *rest):
    if first:
        o_ref, vf_out_ref = rest
    else:
        v0_ref, v2_ref, vf_ref, o_ref = rest
    i = pl.program_id(1)
    last = pl.num_programs(1) - 1
    x = x_ref[0]
    tb, w3 = x.shape
    rw = w3 // 3
    row = lax.broadcasted_iota(jnp.int32, (tb, 1), 0)
    prev_edge = jnp.where(i == 0, 0.0, xp_ref[0, SUBLANES - 1:SUBLANES, :])
    next_edge = jnp.where(i == last, 0.0, xn_ref[0, 0:1, :])
    prev = jnp.where(row == 0, prev_edge, pltpu.roll(x, 1, 0))
    nxt = jnp.where(row == tb - 1, next_edge, pltpu.roll(x, tb - 1, 0))
    s = x + tsp_ref[...] * (prev - x) + tsn_ref[...] * (nxt - x)
    r = s[:, 0:rw]
    k = s[:, rw:2 * rw]
    v = s[:, 2 * rw:3 * rw]
    if first:
        vf_out_ref[0] = v
    else:
        mix = _sigmoid(v0_ref[...] + _mm3(lo_ref[0], v2_ref[...]))
        v = v + (vf_ref[0] - v) * mix
    kk = k * kk_ref[...]
    ss = _head_sum(kk * kk, ones_ref[...])
    kk = kk * lax.rsqrt(jnp.maximum(ss, 1e-24))
    o_ref[0, :, 0:rw] = r
    o_ref[0, :, rw:2 * rw] = k
    o_ref[0, :, 2 * rw:3 * rw] = v
    o_ref[0, :, 3 * rw:4 * rw] = kk


def _rwkv_prep(rkv, lora, ts_prev, ts_next, k_k, v_first, v0, v2pad, tb):
    g, m, w3 = rkv.shape
    rw = w3 // 3
    first = v_first is None
    nb8 = tb // SUBLANES
    n8 = m // SUBLANES
    ones3 = _head_ones3()
    row = lambda a: a.reshape(1, -1)
    vec = lambda n: pl.BlockSpec((1, n), lambda b, i: (0, 0))
    in_specs = [
        pl.BlockSpec((1, tb, w3), lambda b, i: (b, i, 0)),
        pl.BlockSpec((1, SUBLANES, w3), lambda b, i: (b, jnp.maximum(i * nb8 - 1, 0), 0)),
        pl.BlockSpec((1, SUBLANES, w3), lambda b, i: (b, jnp.minimum((i + 1) * nb8, n8 - 1), 0)),
        pl.BlockSpec((1, tb, LANES), lambda b, i: (b, i, 4)),
        vec(w3), vec(w3), vec(rw),
        pl.BlockSpec(ones3.shape, lambda b, i: (0, 0)),
    ]
    args = [rkv, rkv, rkv, lora, row(ts_prev), row(ts_next), row(k_k), ones3]
    out_main = jax.ShapeDtypeStruct((g, m, 4 * rw), F32)
    spec_main = pl.BlockSpec((1, tb, 4 * rw), lambda b, i: (b, i, 0))
    spec_v = pl.BlockSpec((1, tb, rw), lambda b, i: (b, i, 0))
    if first:
        out_shape = [out_main, jax.ShapeDtypeStruct((g, m, rw), F32)]
        out_specs = [spec_main, spec_v]
    else:
        in_specs += [vec(rw), pl.BlockSpec((LANES, rw), lambda b, i: (0, 0)), spec_v]
        args += [row(v0), v2pad, v_first]
        out_shape = out_main
        out_specs = spec_main
    res = pl.pallas_call(
        functools.partial(_rwkv_prep_kernel, first),
        out_shape=out_shape, grid=(g, m // tb), in_specs=in_specs, out_specs=out_specs,
        compiler_params=_cp(("parallel", "parallel")),
        name="rwkv_prep",
    )(*args)
    if first:
        return res[0], res[1]
    return res, v_first


def _stack_heads(x, lo_mask):
    return jnp.concatenate([jnp.where(lo_mask, x, 0.0), jnp.where(lo_mask, 0.0, x)], axis=0)


def _scan_masks(reverse):
    L = CHUNK
    L2, L4 = 2 * L, 4 * L
    ti = lax.broadcasted_iota(jnp.int32, (L4, L4), 0)
    si = lax.broadcasted_iota(jnp.int32, (L4, L4), 1)
    tq = ti % L2
    sq = si % L2
    tt = tq % L
    st = sq % L
    before = (st > tt) if reverse else (st < tt)
    keep = ((tq // L) == (sq // L)) & (before | ((ti >= L2) & (st == tt)))
    ci = lax.broadcasted_iota(jnp.int32, (L, L), 0)
    cj = lax.broadcasted_iota(jnp.int32, (L, L), 1)
    tri = jnp.where((cj >= ci) if reverse else (cj <= ci), 1.0, 0.0).astype(BF16)
    return keep, jnp.concatenate([tri, tri, tri], axis=1)


def _rwkv_bidir_kernel(pf_ref, pb_ref, lof_ref, lob_ref, s0_ref, w0_ref, w2_ref, a0_ref, a2_ref,
                       g2_ref, ka_ref, rk_ref, gng_ref, gnb_ref, ones_ref,
                       of_ref, ob_ref, so_ref, s_scr):
    i = pl.program_id(0)
    L = CHUNK
    L2, L4 = 2 * L, 4 * L
    bsz = pf_ref.shape[0]
    rw = pf_ref.shape[2] // 4
    n_pair = rw // LANES

    @pl.when(i == 0)
    def _():
        s_scr[...] = s0_ref[...]

    ones3 = ones_ref[...]
    lo_mask = lax.broadcasted_iota(jnp.int32, (1, LANES), 1) < HEAD_DIM
    ei = lax.broadcasted_iota(jnp.int32, (L2, L2), 0)
    ej = lax.broadcasted_iota(jnp.int32, (L2, L2), 1)
    eye = jnp.where(ei == ej, 1.0, 0.0)
    masks = (_scan_masks(False), _scan_masks(True))

    streams = []
    for dr, (p_ref, lo_ref) in enumerate(((pf_ref, lof_ref), (pb_ref, lob_ref))):
        for b in range(bsz):
            p = p_ref[b]
            lo = lo_ref[b]
            r = p[:, 0:rw]
            k = p[:, rw:2 * rw]
            v = p[:, 2 * rw:3 * rw]
            kk = p[:, 3 * rw:4 * rw]
            zw = w0_ref[dr] + _bdot(jnp.tanh(lo[:, 0:LANES]), w2_ref[dr])
            log_w = -(jnp.maximum(-zw, 0.0) + jnp.log(1.0 + jnp.exp(-jnp.abs(zw)))) - 0.5
            lw = -jnp.exp(log_w)
            iclr = _sigmoid(a0_ref[dr] + _bdot(lo[:, LANES:2 * LANES], a2_ref[dr]))
            kd = k * (1.0 + (iclr - 1.0) * ka_ref[...])
            gate = _bdot(_sigmoid(lo[:, (2 + dr) * LANES:(3 + dr) * LANES]), g2_ref[dr])
            bonus = _head_sum(r * kd * rk_ref[...], ones3) * v
            streams.append(dict(dr=dr, b=b, r=r, kd=kd, v=v, av=-kk, bv=kk * iclr, lw=lw,
                                gate=gate, bonus=bonus))

    chains = [(si_, j) for si_ in range(len(streams)) for j in range(n_pair)]

    def cols(si_, j, name):
        return streams[si_][name][:, j * LANES:(j + 1) * LANES]

    ops = []
    for si_, j in chains:
        dr = streams[si_]['dr']
        keep, tri3 = masks[dr]
        lwc = cols(si_, j, 'lw')
        l1 = lwc.astype(BF16)
        lr = lwc - l1.astype(F32)
        l2 = lr.astype(BF16)
        l3 = (lr - l2.astype(F32)).astype(BF16)
        cum = jnp.dot(tri3, jnp.concatenate([l1, l2, l3], axis=0), preferred_element_type=F32)
        end_row = 0 if dr else L - 1
        tot = cum[end_row:end_row + 1, :]
        g_inv = jnp.exp(-cum)
        g_end = jnp.exp(tot - cum)
        bvc = cols(si_, j, 'bv')
        kdc = cols(si_, j, 'kd')
        stk = lambda x: _stack_heads(x, lo_mask).astype(BF16)
        ops.append(dict(
            keep=keep, tot=tot,
            ar_s=jnp.concatenate([stk(cols(si_, j, 'av') * jnp.exp(cum - lwc)),
                                  stk(cols(si_, j, 'r') * jnp.exp(cum))], axis=0),
            bk_s=jnp.concatenate([stk(bvc * g_inv), stk(kdc * g_inv)], axis=0),
            v_s=stk(cols(si_, j, 'v')),
            vt_s=_stack_heads(cols(si_, j, 'v'), lo_mask).T.astype(BF16),
            bkh_s=jnp.concatenate([stk(bvc * g_end), stk(kdc * g_end)], axis=0)))

    dot = functools.partial(jnp.dot, preferred_element_type=F32)
    grams = [jnp.where(o['keep'], _bdot_nt(o['ar_s'], o['bk_s']), 0.0) for o in ops]
    tinv = [eye + g[0:L2, 0:L2] for g in grams]
    pw = [g[0:L2, 0:L2].astype(BF16) for g in grams]
    pw = [dot(a, a).astype(BF16) for a in pw]
    for _ in range(4):
        both = [dot(jnp.concatenate([p, t.astype(BF16)], axis=0), p) for p, t in zip(pw, tinv)]
        pw = [b[0:L2].astype(BF16) for b in both]
        tinv = [t + b[L2:L4] for t, b in zip(tinv, both)]
    tinv = [t + dot(t.astype(BF16), p) for t, p in zip(tinv, pw)]
    x1 = [dot(g[0:L2, L2:L4].astype(BF16), o['v_s']) for g, o in zip(grams, ops)]
    wu = [dot(t.astype(BF16), jnp.concatenate([o['ar_s'][0:L2], x.astype(BF16)], axis=1))
          for t, o, x in zip(tinv, ops, x1)]
    s_old = [s_scr[c] for c in range(len(chains))]
    hy = [_bdot_nt(jnp.concatenate([w[:, 0:L2].astype(BF16), o['ar_s'][L2:L4]], axis=0), s)
          for w, o, s in zip(wu, ops, s_old)]
    u = [h[0:L2] + w[:, L2:L4] for h, w in zip(hy, wu)]
    uv = [jnp.concatenate([uu.astype(BF16), o['v_s']], axis=0) for uu, o in zip(u, ops)]
    ys = [h[L2:L4] + dot(g[L2:L4, :].astype(BF16), x) for h, g, x in zip(hy, grams, uv)]
    for c, (s, uu, o) in enumerate(zip(s_old, u, ops)):
        s_scr[c] = s * jnp.exp(o['tot']) + dot(
            jnp.concatenate([uu.T.astype(BF16), o['vt_s']], axis=1), o['bkh_s'])

    for si_, st in enumerate(streams):
        y = jnp.concatenate([ys[si_ * n_pair + j][0:L] + ys[si_ * n_pair + j][L:L2]
                             for j in range(n_pair)], axis=1)
        mu = _head_sum(y, ones3) * (1.0 / HEAD_DIM)
        d = y - mu
        var = _head_sum(d * d, ones3) * (1.0 / HEAD_DIM)
        out = (d * lax.rsqrt(var + GN_EPS) * gng_ref[...] + gnb_ref[...] + st['bonus']) * st['gate']
        if st['dr'] == 0:
            of_ref[st['b']] = out
        else:
            ob_ref[st['b']] = out

    @pl.when(i == pl.num_programs(0) - 1)
    def _():
        so_ref[...] = s_scr[...]


def _rwkv_bidir(prep, lora, state0, prm):
    g, m, w4 = prep.shape
    rw = w4 // 4
    nblk = m // CHUNK
    w0, w2pad, a0, a2pad, g2, k_a, r_k, gn_g, gn_b = prm
    ones3 = _head_ones3()
    row = lambda a: a.reshape(1, -1)
    fmap = lambda i: (0, i, 0)
    bmap = lambda i: (0, nblk - 1 - i, 0)
    full = lambda a: pl.BlockSpec(a.shape, lambda i: (0,) * a.ndim)
    w0, a0 = w0[:, None, :], a0[:, None, :]
    consts = [w0, w2pad, a0, a2pad, g2, row(k_a), row(r_k), row(gn_g), row(gn_b), ones3]
    st_spec = pl.BlockSpec(state0.shape, lambda i: (0, 0, 0))
    y_f, y_b, s_out = pl.pallas_call(
        _rwkv_bidir_kernel,
        out_shape=[jax.ShapeDtypeStruct((g, m, rw), F32), jax.ShapeDtypeStruct((g, m, rw), F32),
                   jax.ShapeDtypeStruct(state0.shape, F32)],
        grid=(nblk,),
        in_specs=[pl.BlockSpec((g, CHUNK, w4), fmap), pl.BlockSpec((g, CHUNK, w4), bmap),
                  pl.BlockSpec((g, CHUNK, lora.shape[2]), fmap),
                  pl.BlockSpec((g, CHUNK, lora.shape[2]), bmap),
                  st_spec] + [full(a) for a in consts],
        out_specs=[pl.BlockSpec((g, CHUNK, rw), fmap), pl.BlockSpec((g, CHUNK, rw), bmap), st_spec],
        scratch_shapes=[pltpu.VMEM(state0.shape, F32)],
        compiler_params=_cp(("arbitrary",)),
        name="rwkv_scan",
    )(prep, prep, lora, lora, state0, *consts)
    return y_f, y_b, s_out


def _na_bias_table(rpb):
    n_head = rpb.shape[0]
    col = np.arange(GRID_W)
    col_start = np.clip(col - WIN_COLS // 2, 0, GRID_W - WIN_COLS)
    col_in = (col[None, :] >= col_start[:, None]) & (col[None, :] < col_start[:, None] + WIN_COLS)
    col_rel = np.clip(col[None, :] - col[:, None] + WIN_COLS - 1, 0, 2 * WIN_COLS - 2)
    n_rel = 2 * WIN_COLS - 1
    sel = (col_rel.reshape(-1)[None, :] == np.arange(n_rel)[:, None]).astype(np.float32)
    c = jnp.einsum('hrc,cn->hrn', rpb, jnp.asarray(sel), precision=HI)
    c = c.reshape(n_head, 2 * WIN_ROWS - 1, GRID_W, GRID_W)
    c = jnp.where(col_in[None, None], c, NEG_BIG)
    t = jnp.stack([c[:, d:d + WIN_ROWS] for d in range(WIN_ROWS)], axis=1)
    t = t.transpose(0, 1, 3, 2, 4).reshape(n_head // 2, 2, WIN_ROWS, GRID_W, WIN_ROWS * GRID_W)
    return t.astype(F32)


def _na_kernel(rows, q_ref, kp_ref, kc_ref, kn_ref, vp_ref, vc_ref, vn_ref, kx_ref, vx_ref,
               b_ref, o_ref, kcat, vcat):
    i = pl.program_id(2)
    w = GRID_W
    blk = WIN_ROWS * w
    scale = HEAD_DIM ** -0.5
    kcat[0:blk] = kp_ref[0].astype(BF16)
    kcat[blk:2 * blk] = kc_ref[0].astype(BF16)
    kcat[2 * blk:3 * blk] = kn_ref[0].astype(BF16)
    vcat[0:blk] = vp_ref[0].astype(BF16)
    vcat[blk:2 * blk] = vc_ref[0].astype(BF16)
    vcat[2 * blk:3 * blk] = vn_ref[0].astype(BF16)
    kx = kx_ref[0].astype(BF16)
    vx = vx_ref[0].astype(BF16)
    lo_mask = lax.broadcasted_iota(jnp.int32, (1, LANES), 1) < HEAD_DIM
    rng = range(WIN_ROWS)
    r0 = [jnp.clip(i * WIN_ROWS + rr - WIN_ROWS // 2, 0, rows - WIN_ROWS) for rr in rng]
    off = [pl.multiple_of((r0[rr] - (i - 1) * WIN_ROWS) * w, w) for rr in rng]
    dlt = [r0[rr] - (i * WIN_ROWS + rr) + WIN_ROWS - 1 for rr in rng]
    qs = [_stack_heads(q_ref[0, rr * w:(rr + 1) * w, :], lo_mask).astype(BF16) for rr in rng]
    s = [_bdot_nt(qs[rr], kcat[pl.ds(off[rr], blk), :]) * scale
         + jnp.concatenate([b_ref[0, 0, dlt[rr]], b_ref[0, 1, dlt[rr]]], axis=0) for rr in rng]
    sx = [_bdot_nt(q, kx) * scale for q in qs]
    mx = [jnp.maximum(jnp.max(a, axis=-1, keepdims=True), jnp.max(b, axis=-1, keepdims=True))
          for a, b in zip(s, sx)]
    p = [jnp.exp(a - m) for a, m in zip(s, mx)]
    px = [jnp.exp(a - m) for a, m in zip(sx, mx)]
    den = [jnp.sum(a, axis=-1, keepdims=True) + jnp.sum(b, axis=-1, keepdims=True)
           for a, b in zip(p, px)]
    o = [(_bdot(p[rr], vcat[pl.ds(off[rr], blk), :]) + _bdot(px[rr], vx)) / den[rr] for rr in rng]
    for rr in rng:
        o_ref[0, rr * w:(rr + 1) * w, :] = jnp.where(lo_mask, o[rr][0:w], o[rr][w:2 * w])


def _na_attention(qkv, qkv_ctx, bias_tab):
    bsz, t_len, w3 = qkv.shape
    na = w3 // 3
    n_pair = na // LANES
    rows = t_len // GRID_W
    blk = WIN_ROWS * GRID_W
    nblk = rows // WIN_ROWS
    c_len = qkv_ctx.shape[1]
    kv = lambda sel, shift: pl.BlockSpec(
        (1, blk, LANES), lambda b, j, i: (b, jnp.clip(i + shift, 0, nblk - 1), sel * n_pair + j))
    cx = lambda sel: pl.BlockSpec((1, c_len, LANES), lambda b, j, i: (b, 0, sel * n_pair + j))
    return pl.pallas_call(
        functools.partial(_na_kernel, rows),
        out_shape=jax.ShapeDtypeStruct((bsz, t_len, na), F32),
        grid=(bsz, n_pair, nblk),
        in_specs=[kv(0, 0), kv(1, -1), kv(1, 0), kv(1, 1), kv(2, -1), kv(2, 0), kv(2, 1),
                  cx(1), cx(2),
                  pl.BlockSpec((1, 2, WIN_ROWS, GRID_W, blk), lambda b, j, i: (j, 0, 0, 0, 0))],
        out_specs=pl.BlockSpec((1, blk, LANES), lambda b, j, i: (b, i, j)),
        scratch_shapes=[pltpu.VMEM((3 * blk, LANES), BF16), pltpu.VMEM((3 * blk, LANES), BF16)],
        compiler_params=_cp(("parallel", "parallel", "arbitrary")),
        name="na_attention",
    )(qkv, qkv, qkv, qkv, qkv, qkv, qkv, qkv_ctx, qkv_ctx, bias_tab)


def _ctx_attn_kernel(q_ref, k_ref, v_ref, o_ref):
    lo_mask = lax.broadcasted_iota(jnp.int32, (1, LANES), 1) < HEAD_DIM
    q = q_ref[0]
    c = q.shape[0]
    qs = _stack_heads(q, lo_mask)
    s = _bdot_nt(qs, k_ref[0]) * (HEAD_DIM ** -0.5)
    mx = jnp.max(s, axis=-1, keepdims=True)
    p = jnp.exp(s - mx)
    o = _bdot(p, v_ref[0]) / jnp.sum(p, axis=-1, keepdims=True)
    o_ref[0] = jnp.where(lo_mask, o[0:c], o[c:2 * c])


def _ctx_attention(qkv_ctx):
    bsz, c_len, w3 = qkv_ctx.shape
    na = w3 // 3
    n_pair = na // LANES
    sp = lambda sel: pl.BlockSpec((1, c_len, LANES), lambda b, j: (b, 0, sel * n_pair + j))
    return pl.pallas_call(
        _ctx_attn_kernel,
        out_shape=jax.ShapeDtypeStruct((bsz, c_len, na), F32),
        grid=(bsz, n_pair),
        in_specs=[sp(0), sp(1), sp(2)],
        out_specs=pl.BlockSpec((1, c_len, LANES), lambda b, j: (b, 0, j)),
        compiler_params=_cp(("parallel", "parallel")),
        name="ctx_attention",
    )(qkv_ctx, qkv_ctx, qkv_ctx)


def _layer_norm(z, g, b):
    mu = jnp.mean(z, axis=-1, keepdims=True)
    d = z - mu
    var = jnp.mean(d * d, axis=-1, keepdims=True)
    return d * lax.rsqrt(var + LN_EPS) * g + b


def _outproj_kernel(alpha, f_ref, rwf_ref, rwb_ref, at_ref, x_ref, ga_ref, sc_ref, sh_ref, wfm_ref,
                    wo_ref, g_ref, b_ref, wr_ref, br_ref, xo_ref, h_ref, lg_ref):
    nf = f_ref.shape[2]
    nr = rwf_ref.shape[2]
    fm = jnp.dot(f_ref[0].astype(BF16), wfm_ref[...], preferred_element_type=F32)
    rwo = (rwf_ref[0] + rwb_ref[0]).astype(BF16)
    mix = (jnp.dot(fm.astype(BF16), wo_ref[0:nf, :], preferred_element_type=F32)
           + jnp.dot(rwo, wo_ref[nf:nf + nr, :], preferred_element_type=F32)
           + jnp.dot(at_ref[0].astype(BF16), wo_ref[nf + nr:, :], preferred_element_type=F32))
    xn = _layer_norm(alpha * x_ref[0] + ga_ref[0] * mix, g_ref[...], b_ref[...])
    xo_ref[0] = xn
    h = xn * (1.0 + sc_ref[0]) + sh_ref[0]
    h_ref[0] = h.astype(BF16)
    lg_ref[0] = _mm3(h, wr_ref[...]) + br_ref[...]


def _outproj(alpha, four, rwo, rwb, att, x, ga, sc, sh, wfm, wo, ln_g, ln_b, wr_pad, br_pad, tm):
    g, m, d = x.shape
    blk = lambda a: pl.BlockSpec((1, tm, a.shape[2]), lambda b, i: (b, i, 0))
    mod = pl.BlockSpec((1, 1, d), lambda b, i: (b, 0, 0))
    cst = lambda a: pl.BlockSpec(a.shape, lambda b, i: (0, 0))
    row = lambda a: a.reshape(1, -1)
    ln_g, ln_b, br_pad = row(ln_g), row(ln_b), row(br_pad)
    return pl.pallas_call(
        functools.partial(_outproj_kernel, alpha),
        out_shape=[jax.ShapeDtypeStruct((g, m, d), F32), jax.ShapeDtypeStruct((g, m, d), BF16),
                   jax.ShapeDtypeStruct((g, m, LANES), F32)],
        grid=(g, m // tm),
        in_specs=[blk(four), blk(rwo), blk(rwb), blk(att), blk(x), mod, mod, mod, cst(wfm), cst(wo),
                  cst(ln_g), cst(ln_b), cst(wr_pad), cst(br_pad)],
        out_specs=[pl.BlockSpec((1, tm, d), lambda b, i: (b, i, 0)),
                   pl.BlockSpec((1, tm, d), lambda b, i: (b, i, 0)),
                   pl.BlockSpec((1, tm, LANES), lambda b, i: (b, i, 0))],
        compiler_params=_cp(("parallel", "parallel")),
        name="outproj_ln",
    )(four, rwo, rwb, att, x, ga, sc, sh, wfm, wo, ln_g, ln_b, wr_pad, br_pad)


def _ln2_kernel(alpha, x_ref, y0_ref, y1_ref, y2_ref, y3_ref, gt_ref, ga_ref, g_ref, b_ref,
                o_ref):
    gt = gt_ref[...]
    y = (gt[:, 4:5] * y0_ref[0] + gt[:, 5:6] * y1_ref[0]
         + gt[:, 6:7] * y2_ref[0] + gt[:, 7:8] * y3_ref[0])
    o_ref[0] = _layer_norm(alpha * x_ref[0] + ga_ref[0] * y, g_ref[...], b_ref[...])


def _ln2(alpha, x, yk, gates, row0, ga, ln_g, ln_b, tm):
    g, m, d = x.shape
    nb = m // tm
    base = row0 // tm
    blk = pl.BlockSpec((1, tm, d), lambda b, i: (b, i, 0))
    vec = pl.BlockSpec((1, d), lambda b, i: (0, 0))
    ysp = lambda k: pl.BlockSpec((1, tm, d), lambda b, i: (k, base + b * nb + i, 0))
    return pl.pallas_call(
        functools.partial(_ln2_kernel, alpha),
        out_shape=jax.ShapeDtypeStruct((g, m, d), F32),
        grid=(g, nb),
        in_specs=[blk, ysp(0), ysp(1), ysp(2), ysp(3),
                  pl.BlockSpec((tm, LANES), lambda b, i: (base + b * nb + i, 0)),
                  pl.BlockSpec((1, 1, d), lambda b, i: (b, 0, 0)), vec, vec],
        out_specs=blk,
        compiler_params=_cp(("parallel", "parallel")),
        name="ln2",
    )(x, yk, yk, yk, yk, gates, ga, ln_g.reshape(1, -1), ln_b.reshape(1, -1))


def _moe_kernel(be_ref, nu_ref, x_ref, wgu_ref, bgu_ref, wdn_ref, bdn_ref, o_ref, wgu_s, wdn_s):
    i = pl.program_id(0)

    @pl.when((i == 0) | (be_ref[i] != be_ref[jnp.maximum(i - 1, 0)]))
    def _():
        wgu_s[...] = wgu_ref[0, 0].astype(BF16)
        wdn_s[...] = wdn_ref[0, 0].astype(BF16)

    @pl.when(i < nu_ref[0])
    def _():
        de = wdn_s.shape[0]
        x = x_ref[...]
        acc = None
        for c0 in range(0, de, MOE_HIDDEN_CHUNK):
            c1 = c0 + MOE_HIDDEN_CHUNK
            gg = jnp.dot(x, wgu_s[:, c0:c1], preferred_element_type=F32) + bgu_ref[0, 0, :, c0:c1]
            uu = (jnp.dot(x, wgu_s[:, de + c0:de + c1], preferred_element_type=F32)
                  + bgu_ref[0, 0, :, de + c0:de + c1])
            gg = jnp.minimum(gg, SWIGLU_LIMIT)
            uu = jnp.clip(uu, -SWIGLU_LIMIT, SWIGLU_LIMIT)
            act = ((uu + 1.0) * gg * _sigmoid(SWIGLU_ALPHA * gg)).astype(BF16)
            part = jnp.dot(act, wdn_s[c0:c1, :], preferred_element_type=F32)
            acc = part if acc is None else acc + part
        o_ref[...] = acc + bdn_ref[0, 0]


def _moe_experts(xs, block_e, n_used, layer, w_gu, b_gu, w_dn, b_dn, tm):
    cap, d = xs.shape
    n_layer, n_exp, _, de2 = w_gu.shape
    de = de2 // 2
    n_blocks = cap // tm
    wmap = lambda i, be, nu: (layer, be[i], 0, 0)
    grid_spec = pltpu.PrefetchScalarGridSpec(
        num_scalar_prefetch=2,
        grid=(n_blocks,),
        in_specs=[pl.BlockSpec((tm, d), lambda i, be, nu: (i, 0)),
                  pl.BlockSpec((1, 1, d, de2), wmap),
                  pl.BlockSpec((1, 1, 1, de2), wmap),
                  pl.BlockSpec((1, 1, de, d), wmap),
                  pl.BlockSpec((1, 1, 1, d), wmap)],
        out_specs=pl.BlockSpec((tm, d), lambda i, be, nu: (i, 0)),
        scratch_shapes=[pltpu.VMEM((d, de2), BF16), pltpu.VMEM((de, d), BF16)],
    )
    return pl.pallas_call(
        _moe_kernel,
        out_shape=jax.ShapeDtypeStruct((cap, d), F32),
        grid_spec=grid_spec,
        compiler_params=_cp(("arbitrary",)),
        name="moe_experts",
    )(block_e, n_used, xs, w_gu, b_gu.reshape(n_layer, n_exp, 1, de2), w_dn,
      b_dn.reshape(n_layer, n_exp, 1, d))


def _router_kernel(lg_ref, tri_ref, r_ref, cnt_ref, carry):
    i = pl.program_id(0)

    @pl.when(i == 0)
    def _():
        carry[...] = jnp.zeros_like(carry)

    lane = lax.broadcasted_iota(jnp.int32, lg_ref.shape, 1)
    lanef = lane.astype(F32)
    lg = jnp.where(lane < N_EXPERTS, lg_ref[...], NEG_BIG)
    tops, idxs, hots = [], [], []
    for _ in range(TOP_K):
        m = jnp.max(lg, axis=-1, keepdims=True)
        idx = jnp.min(jnp.where(lg == m, lanef, float(LANES)), axis=-1, keepdims=True)
        hot = lanef == idx
        lg = jnp.where(hot, NEG_BIG, lg)
        tops.append(m)
        idxs.append(idx)
        hots.append(hot)
    ex = [jnp.exp(m - tops[0]) for m in tops]
    den = ex[0] + ex[1] + ex[2] + ex[3]
    oh = jnp.where(hots[0] | hots[1] | hots[2] | hots[3], 1.0, 0.0)
    before = carry[...] + jnp.dot(tri_ref[...], oh.astype(BF16), preferred_element_type=F32)
    rec = jnp.zeros(lg_ref.shape, F32)
    for k in range(TOP_K):
        rank = jnp.sum(jnp.where(hots[k], before, 0.0), axis=-1, keepdims=True)
        rec = jnp.where(lane == k, idxs[k], rec)
        rec = jnp.where(lane == TOP_K + k, ex[k] / den, rec)
        rec = jnp.where(lane == 2 * TOP_K + k, rank, rec)
    r_ref[...] = rec
    carry[...] = carry[...] + jnp.sum(oh, axis=0, keepdims=True)
    cnt_ref[...] = carry[...]


def _route(logits):
    n_tok = logits.shape[0]
    tb = min(ROUTE_TB, n_tok)
    tri = jnp.asarray(np.tril(np.ones((tb, tb), np.float32), -1), BF16)
    return pl.pallas_call(
        _router_kernel,
        out_shape=[jax.ShapeDtypeStruct((n_tok, LANES), F32), jax.ShapeDtypeStruct((1, LANES), F32)],
        grid=(n_tok // tb,),
        in_specs=[pl.BlockSpec((tb, LANES), lambda i: (i, 0)), pl.BlockSpec((tb, tb), lambda i: (0, 0))],
        out_specs=[pl.BlockSpec((tb, LANES), lambda i: (i, 0)), pl.BlockSpec((1, LANES), lambda i: (0, 0))],
        scratch_shapes=[pltpu.VMEM((1, LANES), F32)],
        compiler_params=_cp(("arbitrary",)),
        name="router",
    )(logits, tri)


def _moe(h, logits, layer, wgu, bgu, wdn, bdn, tm):
    n_tok, d = h.shape
    n_slot = n_tok * TOP_K
    rec, cnt = _route(logits)
    top_e = rec[:, 0:TOP_K].astype(jnp.int32)
    rank = rec[:, 2 * TOP_K:3 * TOP_K].astype(jnp.int32)
    counts = cnt[0, 0:N_EXPERTS].astype(jnp.int32)
    first = jnp.cumsum(counts) - counts
    padded = (counts + tm - 1) // tm * tm
    pad_end = jnp.cumsum(padded)
    pad_start = pad_end - padded
    n_blocks = -(-n_slot // tm) + N_EXPERTS
    cap = n_blocks * tm
    blk_row = jnp.arange(n_blocks, dtype=jnp.int32) * tm
    block_e = jnp.minimum(jnp.sum(pad_end[None, :] <= blk_row[:, None], axis=1),
                          N_EXPERTS - 1).astype(jnp.int32)
    n_used = (pad_end[-1:] // tm).astype(jnp.int32)
    experts = jnp.arange(N_EXPERTS, dtype=jnp.int32)
    pos = jnp.sum(jnp.where(top_e[..., None] == experts, pad_start, 0), axis=-1) + rank
    order = jnp.argsort(pos.reshape(-1)).astype(jnp.int32)
    row_e = jnp.repeat(block_e, tm)
    off = jnp.arange(cap, dtype=jnp.int32) - pad_start[row_e]
    src = jnp.clip(first[row_e] + off, 0, n_slot - 1)
    slot_tok = jnp.where(off < counts[row_e], order.at[src].get(mode='promise_in_bounds') // TOP_K, 0)
    xs = h.at[slot_tok].get(mode='promise_in_bounds')
    ys = _moe_experts(xs, block_e, n_used, layer, wgu, bgu, wdn, bdn, tm)
    yk = ys.at[pos.T.reshape(-1)].get(mode='promise_in_bounds').reshape(TOP_K, n_tok, d)
    return yk, rec


def _pad_rows(w, start, total):
    return jnp.zeros((total, w.shape[1]), w.dtype).at[start:start + w.shape[0]].set(w)


def kernel(x, c, ctx, c_ctx, w_ada, b_ada, w_in, w_vdown, w_fmap, ts_prev, ts_next, rw_w0, rw_w2, rw_a0, rw_a2, rw_g2, rw_kk, rw_ka, rw_rk, rw_gn_g, rw_gn_b, rw_v0, rw_v2, na_rpb, w_out, ln1_g, ln1_b, w_router, b_router, w_gu, b_gu, w_dn, b_dn, ln2_g, ln2_b):
    bsz, n_lat, d = x.shape
    c_len = ctx.shape[1]
    depth = w_in.shape[0]
    f_dim = w_fmap.shape[1]
    rw_dim = rw_kk.shape[1]
    alpha = (2 * depth) ** 0.25
    n_pair = rw_dim // LANES

    tm_lat = min(512, n_lat)
    tm_ctx = min(256, c_len)
    tb_ctx = min(128, c_len)
    tm_moe = 512
    n1_lat, n2_lat = _split_len(n_lat)
    n1_ctx, n2_ctx = _split_len(c_len)

    cond = jnp.zeros((SUBLANES, d), F32).at[0:bsz].set(c).at[bsz].set(c_ctx)
    ada = _ada_all(cond, w_ada, b_ada)

    xl, xc = x, ctx
    vf_l = vf_c = None
    o_r = f_dim
    o_lora = f_dim + 3 * rw_dim
    o_q = o_lora + 2 * DECAY_LORA + 2 * ICLR_LORA + 2 * GATE_LORA
    for l in range(depth):
        last = l == depth - 1
        mods = ada[l].reshape(SUBLANES, 6, d)
        lat = lambda k: mods[0:bsz, k][:, None, :]
        cx = lambda k: jnp.broadcast_to(mods[bsz, k][None, None, :], (bsz, 1, d))

        wi = w_in[l]
        vdown = jnp.zeros((d, LANES), F32)
        if l > 0:
            vdown = vdown.at[:, 0:VRES_LORA].set(w_vdown[l - 1])
        wf = wi[:, 0:f_dim].astype(BF16)
        wr = wi[:, o_r:o_lora].astype(BF16)
        wl = jnp.concatenate([wi[:, o_lora:o_q], vdown], axis=1).astype(BF16)
        wq = wi[:, o_q:].astype(BF16)
        f_l, rkv_l, lora_l, qkv_l = _inproj(xl, lat(1), lat(0), wf, wr, wl, wq, tm_lat)
        f_c, rkv_c, lora_c, qkv_c = _inproj(xc, cx(1), cx(0), wf, wr, wl, wq, tm_ctx)

        if l == 0:
            v0 = v2pad = None
        else:
            v0 = rw_v0[l - 1]
            v2pad = _pad_rows(rw_v2[l - 1], 0, LANES)
        prep_c, vf_c = _rwkv_prep(rkv_c, lora_c, ts_prev[l], ts_next[l], rw_kk[l], vf_c, v0, v2pad,
                                  tb_ctx)
        prep_l, vf_l = _rwkv_prep(rkv_l, lora_l, ts_prev[l], ts_next[l], rw_kk[l], vf_l, v0, v2pad,
                                  tm_lat)
        prm = (rw_w0[l],
               jnp.stack([_pad_rows(rw_w2[l, dr], dr * DECAY_LORA, LANES) for dr in range(2)]),
               rw_a0[l],
               jnp.stack([_pad_rows(rw_a2[l, dr], dr * ICLR_LORA, LANES) for dr in range(2)]),
               rw_g2[l], rw_ka[l], rw_rk[l].reshape(-1), rw_gn_g[l], rw_gn_b[l])
        s0 = jnp.zeros((2 * bsz * n_pair, LANES, LANES), F32)
        rwf_c, rwb_c, s_ctx = _rwkv_bidir(prep_c, lora_c, s0, prm)
        rwf_l, rwb_l, _ = _rwkv_bidir(prep_l, lora_l, s_ctx, prm)

        att_l = _na_attention(qkv_l, qkv_c, _na_bias_table(na_rpb[l]))
        four_l = _fourier(f_l, n1_lat, n2_lat)

        wfm = w_fmap[l].astype(BF16)
        wo = w_out[l].astype(BF16)
        wr_pad = jnp.zeros((d, LANES), F32).at[:, 0:N_EXPERTS].set(w_router[l])
        br_pad = jnp.zeros((LANES,), F32).at[0:N_EXPERTS].set(b_router[l])
        xl, hl, lg_l = _outproj(alpha, four_l, rwf_l, rwb_l, att_l, xl, lat(2), lat(4), lat(3), wfm, wo,
                                ln1_g[l], ln1_b[l], wr_pad, br_pad, tm_lat)
        if last:
            yk, gates = _moe(hl.reshape(-1, d), lg_l.reshape(-1, LANES), l, w_gu, b_gu, w_dn, b_dn,
                             tm_moe)
            xl = _ln2(alpha, xl, yk, gates, 0, lat(5), ln2_g[l], ln2_b[l], tm_lat)
        else:
            att_c = _ctx_attention(qkv_c)
            four_c = _fourier(f_c, n1_ctx, n2_ctx)
            xc, hc, lg_c = _outproj(alpha, four_c, rwf_c, rwb_c, att_c, xc, cx(2), cx(4), cx(3), wfm, wo,
                                    ln1_g[l], ln1_b[l], wr_pad, br_pad, tm_ctx)
            h_all = jnp.concatenate([hl.reshape(-1, d), hc.reshape(-1, d)], axis=0)
            lg_all = jnp.concatenate([lg_l.reshape(-1, LANES), lg_c.reshape(-1, LANES)], axis=0)
            yk, gates = _moe(h_all, lg_all, l, w_gu, b_gu, w_dn, b_dn, tm_moe)
            n_l = bsz * n_lat
            xl = _ln2(alpha, xl, yk, gates, 0, lat(5), ln2_g[l], ln2_b[l], tm_lat)
            xc = _ln2(alpha, xc, yk, gates, n_l, cx(5), ln2_g[l], ln2_b[l], tm_ctx)
    return xl
```

```python
import functools

import numpy as np
import jax
import jax.numpy as jnp
from jax import lax
from jax.experimental import pallas as pl
from jax.experimental.pallas import tpu as pltpu

F32 = jnp.float32
BF16 = jnp.bfloat16
HI = lax.Precision.HIGHEST

GRID_W = 64
HEAD_DIM = 64
WIN_ROWS = 8
WIN_COLS = 16
N_EXPERTS = 32
TOP_K = 4
SWIGLU_LIMIT = 7.0
SWIGLU_ALPHA = 1.702
LN_EPS = 1e-5
GN_EPS = HEAD_DIM * 1e-5
DECAY_LORA = 64
ICLR_LORA = 64
GATE_LORA = 128
VRES_LORA = 32

LANES = 128
SUBLANES = 8
VMEM_LIMIT = 56 * 1024 * 1024

CHUNK = 64
NEG_BIG = -1e30
ROUTE_TB = 512


def _cp(sem, vmem=VMEM_LIMIT):
    return pltpu.CompilerParams(dimension_semantics=sem, vmem_limit_bytes=vmem)


def _dot(a, b, prec=HI):
    return jnp.dot(a, b, precision=prec, preferred_element_type=F32)


def _bdot(a, b):
    return jnp.dot(a.astype(BF16), b.astype(BF16), preferred_element_type=F32)


def _bdot_nt(a, b):
    return lax.dot_general(a.astype(BF16), b.astype(BF16), (((1,), (1,)), ((), ())),
                           preferred_element_type=F32)


def _sigmoid(x):
    return 1.0 / (1.0 + jnp.exp(-x))


def _split2(x):
    x1 = x.astype(BF16)
    x2 = (x - x1.astype(F32)).astype(BF16)
    return x1, x2


def _lhs3(x):
    x1, x2 = _split2(x)
    return jnp.concatenate([x1, x2, x1], axis=1)


def _rhs3(y):
    y1, y2 = _split2(y)
    return jnp.concatenate([y1, y1, y2], axis=0)


def _mm3(a, b):
    return jnp.dot(_lhs3(a), _rhs3(b), preferred_element_type=F32)


def _head_sum(x, ones3):
    outs = []
    for j in range(x.shape[1] // LANES):
        xs = x[:, j * LANES:(j + 1) * LANES]
        x1 = xs.astype(BF16)
        r1 = xs - x1.astype(F32)
        x2 = r1.astype(BF16)
        x3 = (r1 - x2.astype(F32)).astype(BF16)
        outs.append(jnp.dot(jnp.concatenate([x1, x2, x3], axis=1), ones3,
                            preferred_element_type=F32))
    return jnp.concatenate(outs, axis=1)


def _ada_kernel(c_ref, w_ref, b_ref, o_ref):
    c = c_ref[...]
    o_ref[0] = _dot(c * _sigmoid(c), w_ref[0]) + b_ref[0]


def _ada_all(cond, w_ada, b_ada):
    n_layer, d, n = w_ada.shape
    tn = 1536
    return pl.pallas_call(
        _ada_kernel,
        out_shape=jax.ShapeDtypeStruct((n_layer, SUBLANES, n), F32),
        grid=(n_layer, n // tn),
        in_specs=[pl.BlockSpec((SUBLANES, d), lambda l, j: (0, 0)),
                  pl.BlockSpec((1, d, tn), lambda l, j: (l, 0, j)),
                  pl.BlockSpec((1, 1, tn), lambda l, j: (l, 0, j))],
        out_specs=pl.BlockSpec((1, SUBLANES, tn), lambda l, j: (l, 0, j)),
        compiler_params=_cp(("arbitrary", "arbitrary")),
        name="ada",
    )(cond, w_ada, b_ada.reshape(n_layer, 1, n))


def _inproj_kernel(x_ref, sc_ref, sh_ref, wf_ref, wr_ref, wl_ref, wq_ref,
                   of_ref, or_ref, ol_ref, oq_ref):
    xm = (x_ref[0] * (1.0 + sc_ref[0]) + sh_ref[0]).astype(BF16)
    of_ref[0] = jnp.dot(xm, wf_ref[...], preferred_element_type=F32)
    or_ref[0] = jnp.dot(xm, wr_ref[...], preferred_element_type=F32)
    ol_ref[0] = jnp.dot(xm, wl_ref[...], preferred_element_type=F32)
    oq_ref[0] = jnp.dot(xm, wq_ref[...], preferred_element_type=F32)


def _inproj(x, sc, sh, wf, wr, wl, wq, tm):
    g, m, d = x.shape
    ws = (wf, wr, wl, wq)
    mod = pl.BlockSpec((1, 1, d), lambda b, i: (b, 0, 0))
    return pl.pallas_call(
        _inproj_kernel,
        out_shape=[jax.ShapeDtypeStruct((g, m, w.shape[1]), F32) for w in ws],
        grid=(g, m // tm),
        in_specs=[pl.BlockSpec((1, tm, d), lambda b, i: (b, i, 0)), mod, mod]
                 + [pl.BlockSpec(w.shape, lambda b, i: (0, 0)) for w in ws],
        out_specs=[pl.BlockSpec((1, tm, w.shape[1]), lambda b, i: (b, i, 0)) for w in ws],
        compiler_params=_cp(("parallel", "parallel")),
        name="inproj",
    )(x, sc, sh, *ws)


def _dft_consts(t_len, n1, n2, tb2, width):
    groups = width // HEAD_DIM
    j = np.arange(HEAD_DIM)
    ang = 2.0 * np.pi * np.outer(j, j) / HEAD_DIM
    eye = np.eye(groups)
    cbd = np.kron(eye, np.cos(ang)) / np.sqrt(HEAD_DIM)
    sbd = np.kron(eye, np.sin(ang)) / np.sqrt(HEAD_DIM)
    a1 = 2.0 * np.pi * np.outer(np.arange(n1), np.arange(n1)) / n1
    c1, s1 = np.cos(a1), np.sin(a1)
    atw = 2.0 * np.pi * np.outer(np.arange(n1), np.arange(n2)) / t_len
    twc = np.cos(atw).reshape(n1, n2 // tb2, tb2).transpose(1, 0, 2)
    tws = np.sin(atw).reshape(n1, n2 // tb2, tb2).transpose(1, 0, 2)
    a2 = 2.0 * np.pi * np.outer(np.arange(n2), np.arange(n2)) / n2
    c2, s2 = np.cos(a2) / np.sqrt(t_len), np.sin(a2) / np.sqrt(t_len)
    return [jnp.asarray(v, F32) for v in (cbd, sbd, c1, s1, twc, tws, c2, s2)]


def _four_a_kernel(tb2, width, x_ref, cbd_ref, sbd_ref, c1_ref, s1_ref, twc_ref, tws_ref,
                   yr_ref, yi_ref):
    c1 = c1_ref[...]
    s1 = s1_ref[...]
    cbd = cbd_ref[...]
    sbd = sbd_ref[...]
    for jj in range(tb2):
        x = x_ref[0, :, width * jj:width * (jj + 1)]
        zr = _dot(x, cbd)
        zi = -_dot(x, sbd)
        ar = _dot(c1, zr) + _dot(s1, zi)
        ai = _dot(c1, zi) - _dot(s1, zr)
        tc = twc_ref[0, :, jj:jj + 1]
        ts = tws_ref[0, :, jj:jj + 1]
        yr_ref[0, jj] = tc * ar + ts * ai
        yi_ref[0, jj] = tc * ai - ts * ar


def _four_b_kernel(yr_ref, yi_ref, c2_ref, s2_ref, o_ref):
    o_ref[0] = _dot(c2_ref[...], yr_ref[0]) + _dot(s2_ref[...], yi_ref[0])


def _fourier(f, n1, n2):
    bsz, t_len, width = f.shape
    tb2 = SUBLANES
    cbd, sbd, c1, s1, twc, tws, c2, s2 = _dft_consts(t_len, n1, n2, tb2, width)
    const = lambda a: pl.BlockSpec(a.shape, lambda b, j: (0,) * a.ndim)
    yr, yi = pl.pallas_call(
        functools.partial(_four_a_kernel, tb2, width),
        out_shape=[jax.ShapeDtypeStruct((bsz, n2, n1, width), F32)] * 2,
        grid=(bsz, n2 // tb2),
        in_specs=[pl.BlockSpec((1, n1, tb2 * width), lambda b, j: (b, 0, j)),
                  const(cbd), const(sbd), const(c1), const(s1),
                  pl.BlockSpec((1, n1, tb2), lambda b, j: (j, 0, 0)),
                  pl.BlockSpec((1, n1, tb2), lambda b, j: (j, 0, 0))],
        out_specs=[pl.BlockSpec((1, tb2, n1, width), lambda b, j: (b, j, 0, 0))] * 2,
        compiler_params=_cp(("parallel", "parallel")),
        name="fourier_a",
    )(f.reshape(bsz, n1, n2 * width), cbd, sbd, c1, s1, twc, tws)
    ncol = n1 * width
    tc = min(ncol, 2048)
    out = pl.pallas_call(
        _four_b_kernel,
        out_shape=jax.ShapeDtypeStruct((bsz, n2, ncol), F32),
        grid=(bsz, ncol // tc),
        in_specs=[pl.BlockSpec((1, n2, tc), lambda b, j: (b, 0, j)),
                  pl.BlockSpec((1, n2, tc), lambda b, j: (b, 0, j)),
                  const(c2), const(s2)],
        out_specs=pl.BlockSpec((1, n2, tc), lambda b, j: (b, 0, j)),
        compiler_params=_cp(("parallel", "parallel")),
        name="fourier_b",
    )(yr.reshape(bsz, n2, ncol), yi.reshape(bsz, n2, ncol), c2, s2)
    return out.reshape(bsz, t_len, width)


def _split_len(t_len):
    n1 = 1 << ((t_len.bit_length() - 1 + 1) // 2)
    return n1, t_len // n1


def _head_ones3():
    h = np.arange(LANES) // HEAD_DIM
    one = (h[:, None] == h[None, :]).astype(np.float32)
    return jnp.asarray(np.concatenate([one, one, one], axis=0), BF16)


def _rwkv_prep_kernel(first, x_ref, xp_ref, xn_ref, lo_ref, tsp_ref, tsn_ref, kk_ref,
                      ones_ref, *rest):
    if first:
        o_ref, vf_out_ref = rest
    else:
        v0_ref, v2_ref, vf_ref, o_ref = rest
    i = pl.program_id(1)
    last = pl.num_programs(1) - 1
    x = x_ref[0]
    tb, w3 = x.shape
    rw = w3 // 3
    row = lax.broadcasted_iota(jnp.int32, (tb, 1), 0)
    prev_edge = jnp.where(i == 0, 0.0, xp_ref[0, SUBLANES - 1:SUBLANES, :])
    next_edge = jnp.where(i == last, 0.0, xn_ref[0, 0:1, :])
    prev = jnp.where(row == 0, prev_edge, pltpu.roll(x, 1, 0))
    nxt = jnp.where(row == tb - 1, next_edge, pltpu.roll(x, tb - 1, 0))
    s = x + tsp_ref[...] * (prev - x) + tsn_ref[...] * (nxt - x)
    r = s[:, 0:rw]
    k = s[:, rw:2 * rw]
    v = s[:, 2 * rw:3 * rw]
    if first:
        vf_out_ref[0] = v
    else:
        mix = _sigmoid(v0_ref[...] + _mm3(lo_ref[0], v2_ref[...]))
        v = v + (vf_ref[0] - v) * mix
    kk = k * kk_ref[...]
    ss = _head_sum(kk * kk, ones_ref[...])
    kk = kk * lax.rsqrt(jnp.maximum(ss, 1e-24))
    o_ref[0, :, 0:rw] = r
    o_ref[0, :, rw:2 * rw] = k
    o_ref[0, :, 2 * rw:3 * rw] = v
    o_ref[0, :, 3 * rw:4 * rw] = kk


def _rwkv_prep(rkv, lora, ts_prev, ts_next, k_k, v_first, v0, v2pad, tb):
    g, m, w3 = rkv.shape
    rw = w3 // 3
    first = v_first is None
    nb8 = tb // SUBLANES
    n8 = m // SUBLANES
    ones3 = _head_ones3()
    row = lambda a: a.reshape(1, -1)
    vec = lambda n: pl.BlockSpec((1, n), lambda b, i: (0, 0))
    in_specs = [
        pl.BlockSpec((1, tb, w3), lambda b, i: (b, i, 0)),
        pl.BlockSpec((1, SUBLANES, w3), lambda b, i: (b, jnp.maximum(i * nb8 - 1, 0), 0)),
        pl.BlockSpec((1, SUBLANES, w3), lambda b, i: (b, jnp.minimum((i + 1) * nb8, n8 - 1), 0)),
        pl.BlockSpec((1, tb, LANES), lambda b, i: (b, i, 4)),
        vec(w3), vec(w3), vec(rw),
        pl.BlockSpec(ones3.shape, lambda b, i: (0, 0)),
    ]
    args = [rkv, rkv, rkv, lora, row(ts_prev), row(ts_next), row(k_k), ones3]
    out_main = jax.ShapeDtypeStruct((g, m, 4 * rw), F32)
    spec_main = pl.BlockSpec((1, tb, 4 * rw), lambda b, i: (b, i, 0))
    spec_v = pl.BlockSpec((1, tb, rw), lambda b, i: (b, i, 0))
    if first:
        out_shape = [out_main, jax.ShapeDtypeStruct((g, m, rw), F32)]
        out_specs = [spec_main, spec_v]
    else:
        in_specs += [vec(rw), pl.BlockSpec((LANES, rw), lambda b, i: (0, 0)), spec_v]
        args += [row(v0), v2pad, v_first]
        out_shape = out_main
        out_specs = spec_main
    res = pl.pallas_call(
        functools.partial(_rwkv_prep_kernel, first),
        out_shape=out_shape, grid=(g, m // tb), in_specs=in_specs, out_specs=out_specs,
        compiler_params=_cp(("parallel", "parallel")),
        name="rwkv_prep",
    )(*args)
    if first:
        return res[0], res[1]
    return res, v_first


def _stack_heads(x, lo_mask):
    return jnp.concatenate([jnp.where(lo_mask, x, 0.0), jnp.where(lo_mask, 0.0, x)], axis=0)


def _scan_masks(reverse):
    L = CHUNK
    L2, L4 = 2 * L, 4 * L
    ti = lax.broadcasted_iota(jnp.int32, (L4, L4), 0)
    si = lax.broadcasted_iota(jnp.int32, (L4, L4), 1)
    tq = ti % L2
    sq = si % L2
    tt = tq % L
    st = sq % L
    before = (st > tt) if reverse else (st < tt)
    keep = ((tq // L) == (sq // L)) & (before | ((ti >= L2) & (st == tt)))
    ci = lax.broadcasted_iota(jnp.int32, (L, L), 0)
    cj = lax.broadcasted_iota(jnp.int32, (L, L), 1)
    tri = jnp.where((cj >= ci) if reverse else (cj <= ci), 1.0, 0.0).astype(BF16)
    return keep, jnp.concatenate([tri, tri, tri], axis=1)


def _rwkv_bidir_kernel(pf_ref, pb_ref, lof_ref, lob_ref, s0_ref, w0_ref, w2_ref, a0_ref, a2_ref,
                       g2_ref, ka_ref, rk_ref, gng_ref, gnb_ref, ones_ref,
                       of_ref, ob_ref, so_ref, s_scr):
    i = pl.program_id(0)
    L = CHUNK
    L2, L4 = 2 * L, 4 * L
    bsz = pf_ref.shape[0]
    rw = pf_ref.shape[2] // 4
    n_pair = rw // LANES

    @pl.when(i == 0)
    def _():
        s_scr[...] = s0_ref[...]

    ones3 = ones_ref[...]
    lo_mask = lax.broadcasted_iota(jnp.int32, (1, LANES), 1) < HEAD_DIM
    ei = lax.broadcasted_iota(jnp.int32, (L2, L2), 0)
    ej = lax.broadcasted_iota(jnp.int32, (L2, L2), 1)
    eye = jnp.where(ei == ej, 1.0, 0.0)
    masks = (_scan_masks(False), _scan_masks(True))

    streams = []
    for dr, (p_ref, lo_ref) in enumerate(((pf_ref, lof_ref), (pb_ref, lob_ref))):
        for b in range(bsz):
            p = p_ref[b]
            lo = lo_ref[b]
            r = p[:, 0:rw]
            k = p[:, rw:2 * rw]
            v = p[:, 2 * rw:3 * rw]
            kk = p[:, 3 * rw:4 * rw]
            zw = w0_ref[dr] + _bdot(jnp.tanh(lo[:, 0:LANES]), w2_ref[dr])
            log_w = -(jnp.maximum(-zw, 0.0) + jnp.log(1.0 + jnp.exp(-jnp.abs(zw)))) - 0.5
            lw = -jnp.exp(log_w)
            iclr = _sigmoid(a0_ref[dr] + _bdot(lo[:, LANES:2 * LANES], a2_ref[dr]))
            kd = k * (1.0 + (iclr - 1.0) * ka_ref[...])
            gate = _bdot(_sigmoid(lo[:, (2 + dr) * LANES:(3 + dr) * LANES]), g2_ref[dr])
            bonus = _head_sum(r * kd * rk_ref[...], ones3) * v
            streams.append(dict(dr=dr, b=b, r=r, kd=kd, v=v, av=-kk, bv=kk * iclr, lw=lw,
                                gate=gate, bonus=bonus))

    chains = [(si_, j) for si_ in range(len(streams)) for j in range(n_pair)]

    def cols(si_, j, name):
        return streams[si_][name][:, j * LANES:(j + 1) * LANES]

    ops = []
    for si_, j in chains:
        dr = streams[si_]['dr']
        keep, tri3 = masks[dr]
        lwc = cols(si_, j, 'lw')
        l1 = lwc.astype(BF16)
        lr = lwc - l1.astype(F32)
        l2 = lr.astype(BF16)
        l3 = (lr - l2.astype(F32)).astype(BF16)
        cum = jnp.dot(tri3, jnp.concatenate([l1, l2, l3], axis=0), preferred_element_type=F32)
        end_row = 0 if dr else L - 1
        tot = cum[end_row:end_row + 1, :]
        g_inv = jnp.exp(-cum)
        g_end = jnp.exp(tot - cum)
        bvc = cols(si_, j, 'bv')
        kdc = cols(si_, j, 'kd')
        stk = lambda x: _stack_heads(x, lo_mask).astype(BF16)
        ops.append(dict(
            keep=keep, tot=tot,
            ar_s=jnp.concatenate([stk(cols(si_, j, 'av') * jnp.exp(cum - lwc)),
                                  stk(cols(si_, j, 'r') * jnp.exp(cum))], axis=0),
            bk_s=jnp.concatenate([stk(bvc * g_inv), stk(kdc * g_inv)], axis=0),
            v_s=stk(cols(si_, j, 'v')),
            vt_s=_stack_heads(cols(si_, j, 'v'), lo_mask).T.astype(BF16),
            bkh_s=jnp.concatenate([stk(bvc * g_end), stk(kdc * g_end)], axis=0)))

    dot = functools.partial(jnp.dot, preferred_element_type=F32)
    grams = [jnp.where(o['keep'], _bdot_nt(o['ar_s'], o['bk_s']), 0.0) for o in ops]
    tinv = [eye + g[0:L2, 0:L2] for g in grams]
    pw = [g[0:L2, 0:L2].astype(BF16) for g in grams]
    pw = [dot(a, a).astype(BF16) for a in pw]
    for _ in range(4):
        both = [dot(jnp.concatenate([p, t.astype(BF16)], axis=0), p) for p, t in zip(pw, tinv)]
        pw = [b[0:L2].astype(BF16) for b in both]
        tinv = [t + b[L2:L4] for t, b in zip(tinv, both)]
    tinv = [t + dot(t.astype(BF16), p) for t, p in zip(tinv, pw)]
    x1 = [dot(g[0:L2, L2:L4].astype(BF16), o['v_s']) for g, o in zip(grams, ops)]
    wu = [dot(t.astype(BF16), jnp.concatenate([o['ar_s'][0:L2], x.astype(BF16)], axis=1))
          for t, o, x in zip(tinv, ops, x1)]
    s_old = [s_scr[c] for c in range(len(chains))]
    hy = [_bdot_nt(jnp.concatenate([w[:, 0:L2].astype(BF16), o['ar_s'][L2:L4]], axis=0), s)
          for w, o, s in zip(wu, ops, s_old)]
    u = [h[0:L2] + w[:, L2:L4] for h, w in zip(hy, wu)]
    uv = [jnp.concatenate([uu.astype(BF16), o['v_s']], axis=0) for uu, o in zip(u, ops)]
    ys = [h[L2:L4] + dot(g[L2:L4, :].astype(BF16), x) for h, g, x in zip(hy, grams, uv)]
    for c, (s, uu, o) in enumerate(zip(s_old, u, ops)):
        s_scr[c] = s * jnp.exp(o['tot']) + dot(
            jnp.concatenate([uu.T.astype(BF16), o['vt_s']], axis=1), o['bkh_s'])

    for si_, st in enumerate(streams):
        y = jnp.concatenate([ys[si_ * n_pair + j][0:L] + ys[si_ * n_pair + j][L:L2]
                             for j in range(n_pair)], axis=1)
        mu = _head_sum(y, ones3) * (1.0 / HEAD_DIM)
        d = y - mu
        var = _head_sum(d * d, ones3) * (1.0 / HEAD_DIM)
        out = (d * lax.rsqrt(var + GN_EPS) * gng_ref[...] + gnb_ref[...] + st['bonus']) * st['gate']
        if st['dr'] == 0:
            of_ref[st['b']] = out
        else:
            ob_ref[st['b']] = out

    @pl.when(i == pl.num_programs(0) - 1)
    def _():
        so_ref[...] = s_scr[...]


def _rwkv_bidir(prep, lora, state0, prm):
    g, m, w4 = prep.shape
    rw = w4 // 4
    nblk = m // CHUNK
    w0, w2pad, a0, a2pad, g2, k_a, r_k, gn_g, gn_b = prm
    ones3 = _head_ones3()
    row = lambda a: a.reshape(1, -1)
    fmap = lambda i: (0, i, 0)
    bmap = lambda i: (0, nblk - 1 - i, 0)
    full = lambda a: pl.BlockSpec(a.shape, lambda i: (0,) * a.ndim)
    w0, a0 = w0[:, None, :], a0[:, None, :]
    consts = [w0, w2pad, a0, a2pad, g2, row(k_a), row(r_k), row(gn_g), row(gn_b), ones3]
    st_spec = pl.BlockSpec(state0.shape, lambda i: (0, 0, 0))
    y_f, y_b, s_out = pl.pallas_call(
        _rwkv_bidir_kernel,
        out_shape=[jax.ShapeDtypeStruct((g, m, rw), F32), jax.ShapeDtypeStruct((g, m, rw), F32),
                   jax.ShapeDtypeStruct(state0.shape, F32)],
        grid=(nblk,),
        in_specs=[pl.BlockSpec((g, CHUNK, w4), fmap), pl.BlockSpec((g, CHUNK, w4), bmap),
                  pl.BlockSpec((g, CHUNK, lora.shape[2]), fmap),
                  pl.BlockSpec((g, CHUNK, lora.shape[2]), bmap),
                  st_spec] + [full(a) for a in consts],
        out_specs=[pl.BlockSpec((g, CHUNK, rw), fmap), pl.BlockSpec((g, CHUNK, rw), bmap), st_spec],
        scratch_shapes=[pltpu.VMEM(state0.shape, F32)],
        compiler_params=_cp(("arbitrary",)),
        name="rwkv_scan",
    )(prep, prep, lora, lora, state0, *consts)
    return y_f, y_b, s_out


def _na_bias_table(rpb):
    n_head = rpb.shape[0]
    col = np.arange(GRID_W)
    col_start = np.clip(col - WIN_COLS // 2, 0, GRID_W - WIN_COLS)
    col_in = (col[None, :] >= col_start[:, None]) & (col[None, :] < col_start[:, None] + WIN_COLS)
    col_rel = np.clip(col[None, :] - col[:, None] + WIN_COLS - 1, 0, 2 * WIN_COLS - 2)
    n_rel = 2 * WIN_COLS - 1
    sel = (col_rel.reshape(-1)[None, :] == np.arange(n_rel)[:, None]).astype(np.float32)
    c = jnp.einsum('hrc,cn->hrn', rpb, jnp.asarray(sel), precision=HI)
    c = c.reshape(n_head, 2 * WIN_ROWS - 1, GRID_W, GRID_W)
    c = jnp.where(col_in[None, None], c, NEG_BIG)
    t = jnp.stack([c[:, d:d + WIN_ROWS] for d in range(WIN_ROWS)], axis=1)
    t = t.transpose(0, 1, 3, 2, 4).reshape(n_head // 2, 2, WIN_ROWS, GRID_W, WIN_ROWS * GRID_W)
    return t.astype(F32)


def _na_kernel(rows, q_ref, kp_ref, kc_ref, kn_ref, vp_ref, vc_ref, vn_ref, kx_ref, vx_ref,
               b_ref, o_ref, kcat, vcat):
    i = pl.program_id(2)
    w = GRID_W
    blk = WIN_ROWS * w
    scale = HEAD_DIM ** -0.5
    kcat[0:blk] = kp_ref[0].astype(BF16)
    kcat[blk:2 * blk] = kc_ref[0].astype(BF16)
    kcat[2 * blk:3 * blk] = kn_ref[0].astype(BF16)
    vcat[0:blk] = vp_ref[0].astype(BF16)
    vcat[blk:2 * blk] = vc_ref[0].astype(BF16)
    vcat[2 * blk:3 * blk] = vn_ref[0].astype(BF16)
    kx = kx_ref[0].astype(BF16)
    vx = vx_ref[0].astype(BF16)
    lo_mask = lax.broadcasted_iota(jnp.int32, (1, LANES), 1) < HEAD_DIM
    rng = range(WIN_ROWS)
    r0 = [jnp.clip(i * WIN_ROWS + rr - WIN_ROWS // 2, 0, rows - WIN_ROWS) for rr in rng]
    off = [pl.multiple_of((r0[rr] - (i - 1) * WIN_ROWS) * w, w) for rr in rng]
    dlt = [r0[rr] - (i * WIN_ROWS + rr) + WIN_ROWS - 1 for rr in rng]
    qs = [_stack_heads(q_ref[0, rr * w:(rr + 1) * w, :], lo_mask).astype(BF16) for rr in rng]
    s = [_bdot_nt(qs[rr], kcat[pl.ds(off[rr], blk), :]) * scale
         + jnp.concatenate([b_ref[0, 0, dlt[rr]], b_ref[0, 1, dlt[rr]]], axis=0) for rr in rng]
    sx = [_bdot_nt(q, kx) * scale for q in qs]
    mx = [jnp.maximum(jnp.max(a, axis=-1, keepdims=True), jnp.max(b, axis=-1, keepdims=True))
          for a, b in zip(s, sx)]
    p = [jnp.exp(a - m) for a, m in zip(s, mx)]
    px = [jnp.exp(a - m) for a, m in zip(sx, mx)]
    den = [jnp.sum(a, axis=-1, keepdims=True) + jnp.sum(b, axis=-1, keepdims=True)
           for a, b in zip(p, px)]
    o = [(_bdot(p[rr], vcat[pl.ds(off[rr], blk), :]) + _bdot(px[rr], vx)) / den[rr] for rr in rng]
    for rr in rng:
        o_ref[0, rr * w:(rr + 1) * w, :] = jnp.where(lo_mask, o[rr][0:w], o[rr][w:2 * w])


def _na_attention(qkv, qkv_ctx, bias_tab):
    bsz, t_len, w3 = qkv.shape
    na = w3 // 3
    n_pair = na // LANES
    rows = t_len // GRID_W
    blk = WIN_ROWS * GRID_W
    nblk = rows // WIN_ROWS
    c_len = qkv_ctx.shape[1]
    kv = lambda sel, shift: pl.BlockSpec(
        (1, blk, LANES), lambda b, j, i: (b, jnp.clip(i + shift, 0, nblk - 1), sel * n_pair + j))
    cx = lambda sel: pl.BlockSpec((1, c_len, LANES), lambda b, j, i: (b, 0, sel * n_pair + j))
    return pl.pallas_call(
        functools.partial(_na_kernel, rows),
        out_shape=jax.ShapeDtypeStruct((bsz, t_len, na), F32),
        grid=(bsz, n_pair, nblk),
        in_specs=[kv(0, 0), kv(1, -1), kv(1, 0), kv(1, 1), kv(2, -1), kv(2, 0), kv(2, 1),
                  cx(1), cx(2),
                  pl.BlockSpec((1, 2, WIN_ROWS, GRID_W, blk), lambda b, j, i: (j, 0, 0, 0, 0))],
        out_specs=pl.BlockSpec((1, blk, LANES), lambda b, j, i: (b, i, j)),
        scratch_shapes=[pltpu.VMEM((3 * blk, LANES), BF16), pltpu.VMEM((3 * blk, LANES), BF16)],
        compiler_params=_cp(("parallel", "parallel", "arbitrary")),
        name="na_attention",
    )(qkv, qkv, qkv, qkv, qkv, qkv, qkv, qkv_ctx, qkv_ctx, bias_tab)


def _ctx_attn_kernel(q_ref, k_ref, v_ref, o_ref):
    lo_mask = lax.broadcasted_iota(jnp.int32, (1, LANES), 1) < HEAD_DIM
    q = q_ref[0]
    c = q.shape[0]
    qs = _stack_heads(q, lo_mask)
    s = _bdot_nt(qs, k_ref[0]) * (HEAD_DIM ** -0.5)
    mx = jnp.max(s, axis=-1, keepdims=True)
    p = jnp.exp(s - mx)
    o = _bdot(p, v_ref[0]) / jnp.sum(p, axis=-1, keepdims=True)
    o_ref[0] = jnp.where(lo_mask, o[0:c], o[c:2 * c])


def _ctx_attention(qkv_ctx):
    bsz, c_len, w3 = qkv_ctx.shape
    na = w3 // 3
    n_pair = na // LANES
    sp = lambda sel: pl.BlockSpec((1, c_len, LANES), lambda b, j: (b, 0, sel * n_pair + j))
    return pl.pallas_call(
        _ctx_attn_kernel,
        out_shape=jax.ShapeDtypeStruct((bsz, c_len, na), F32),
        grid=(bsz, n_pair),
        in_specs=[sp(0), sp(1), sp(2)],
        out_specs=pl.BlockSpec((1, c_len, LANES), lambda b, j: (b, 0, j)),
        compiler_params=_cp(("parallel", "parallel")),
        name="ctx_attention",
    )(qkv_ctx, qkv_ctx, qkv_ctx)


def _layer_norm(z, g, b):
    mu = jnp.mean(z, axis=-1, keepdims=True)
    d = z - mu
    var = jnp.mean(d * d, axis=-1, keepdims=True)
    return d * lax.rsqrt(var + LN_EPS) * g + b


def _outproj_kernel(alpha, f_ref, rwf_ref, rwb_ref, at_ref, x_ref, ga_ref, sc_ref, sh_ref, wfm_ref,
                    wo_ref, g_ref, b_ref, wr_ref, br_ref, xo_ref, h_ref, lg_ref):
    nf = f_ref.shape[2]
    nr = rwf_ref.shape[2]
    fm = jnp.dot(f_ref[0].astype(BF16), wfm_ref[...], preferred_element_type=F32)
    rwo = (rwf_ref[0] + rwb_ref[0]).astype(BF16)
    mix = (jnp.dot(fm.astype(BF16), wo_ref[0:nf, :], preferred_element_type=F32)
           + jnp.dot(rwo, wo_ref[nf:nf + nr, :], preferred_element_type=F32)
           + jnp.dot(at_ref[0].astype(BF16), wo_ref[nf + nr:, :], preferred_element_type=F32))
    xn = _layer_norm(alpha * x_ref[0] + ga_ref[0] * mix, g_ref[...], b_ref[...])
    xo_ref[0] = xn
    h = xn * (1.0 + sc_ref[0]) + sh_ref[0]
    h_ref[0] = h
    lg_ref[0] = _mm3(h, wr_ref[...]) + br_ref[...]


def _outproj(alpha, four, rwo, rwb, att, x, ga, sc, sh, wfm, wo, ln_g, ln_b, wr_pad, br_pad, tm):
    g, m, d = x.shape
    blk = lambda a: pl.BlockSpec((1, tm, a.shape[2]), lambda b, i: (b, i, 0))
    mod = pl.BlockSpec((1, 1, d), lambda b, i: (b, 0, 0))
    cst = lambda a: pl.BlockSpec(a.shape, lambda b, i: (0, 0))
    row = lambda a: a.reshape(1, -1)
    ln_g, ln_b, br_pad = row(ln_g), row(ln_b), row(br_pad)
    return pl.pallas_call(
        functools.partial(_outproj_kernel, alpha),
        out_shape=[jax.ShapeDtypeStruct((g, m, d), F32), jax.ShapeDtypeStruct((g, m, d), F32),
                   jax.ShapeDtypeStruct((g, m, LANES), F32)],
        grid=(g, m // tm),
        in_specs=[blk(four), blk(rwo), blk(rwb), blk(att), blk(x), mod, mod, mod, cst(wfm), cst(wo),
                  cst(ln_g), cst(ln_b), cst(wr_pad), cst(br_pad)],
        out_specs=[pl.BlockSpec((1, tm, d), lambda b, i: (b, i, 0)),
                   pl.BlockSpec((1, tm, d), lambda b, i: (b, i, 0)),
                   pl.BlockSpec((1, tm, LANES), lambda b, i: (b, i, 0))],
        compiler_params=_cp(("parallel", "parallel")),
        name="outproj_ln",
    )(four, rwo, rwb, att, x, ga, sc, sh, wfm, wo, ln_g, ln_b, wr_pad, br_pad)


def _ln2_kernel(alpha, x_ref, y0_ref, y1_ref, y2_ref, y3_ref, gt_ref, ga_ref, g_ref, b_ref,
                o_ref):
    gt = gt_ref[...]
    y = (gt[:, 4:5] * y0_ref[0] + gt[:, 5:6] * y1_ref[0]
         + gt[:, 6:7] * y2_ref[0] + gt[:, 7:8] * y3_ref[0])
    o_ref[0] = _layer_norm(alpha * x_ref[0] + ga_ref[0] * y, g_ref[...], b_ref[...])


def _ln2(alpha, x, yk, gates, row0, ga, ln_g, ln_b, tm):
    g, m, d = x.shape
    nb = m // tm
    base = row0 // tm
    blk = pl.BlockSpec((1, tm, d), lambda b, i: (b, i, 0))
    vec = pl.BlockSpec((1, d), lambda b, i: (0, 0))
    ysp = lambda k: pl.BlockSpec((1, tm, d), lambda b, i: (k, base + b * nb + i, 0))
    return pl.pallas_call(
        functools.partial(_ln2_kernel, alpha),
        out_shape=jax.ShapeDtypeStruct((g, m, d), F32),
        grid=(g, nb),
        in_specs=[blk, ysp(0), ysp(1), ysp(2), ysp(3),
                  pl.BlockSpec((tm, LANES), lambda b, i: (base + b * nb + i, 0)),
                  pl.BlockSpec((1, 1, d), lambda b, i: (b, 0, 0)), vec, vec],
        out_specs=blk,
        compiler_params=_cp(("parallel", "parallel")),
        name="ln2",
    )(x, yk, yk, yk, yk, gates, ga, ln_g.reshape(1, -1), ln_b.reshape(1, -1))


def _moe_kernel(be_ref, nu_ref, x_ref, wgu_ref, bgu_ref, wdn_ref, bdn_ref, o_ref, wgu_s, wdn_s):
    i = pl.program_id(0)

    @pl.when((i == 0) | (be_ref[i] != be_ref[jnp.maximum(i - 1, 0)]))
    def _():
        wgu_s[...] = wgu_ref[0, 0].astype(BF16)
        wdn_s[...] = wdn_ref[0, 0].astype(BF16)

    @pl.when(i < nu_ref[0])
    def _():
        de = wdn_s.shape[0]
        gu = (jnp.dot(x_ref[...].astype(BF16), wgu_s[...], preferred_element_type=F32)
              + bgu_ref[0, 0])
        gg = jnp.minimum(gu[:, 0:de], SWIGLU_LIMIT)
        uu = jnp.clip(gu[:, de:2 * de], -SWIGLU_LIMIT, SWIGLU_LIMIT)
        act = (uu + 1.0) * gg * _sigmoid(SWIGLU_ALPHA * gg)
        o_ref[...] = (jnp.dot(act.astype(BF16), wdn_s[...], preferred_element_type=F32)
                      + bdn_ref[0, 0])


def _moe_experts(xs, block_e, n_used, layer, w_gu, b_gu, w_dn, b_dn, tm):
    cap, d = xs.shape
    n_layer, n_exp, _, de2 = w_gu.shape
    de = de2 // 2
    n_blocks = cap // tm
    wmap = lambda i, be, nu: (layer, be[i], 0, 0)
    grid_spec = pltpu.PrefetchScalarGridSpec(
        num_scalar_prefetch=2,
        grid=(n_blocks,),
        in_specs=[pl.BlockSpec((tm, d), lambda i, be, nu: (i, 0)),
                  pl.BlockSpec((1, 1, d, de2), wmap),
                  pl.BlockSpec((1, 1, 1, de2), wmap),
                  pl.BlockSpec((1, 1, de, d), wmap),
                  pl.BlockSpec((1, 1, 1, d), wmap)],
        out_specs=pl.BlockSpec((tm, d), lambda i, be, nu: (i, 0)),
        scratch_shapes=[pltpu.VMEM((d, de2), BF16), pltpu.VMEM((de, d), BF16)],
    )
    return pl.pallas_call(
        _moe_kernel,
        out_shape=jax.ShapeDtypeStruct((cap, d), F32),
        grid_spec=grid_spec,
        compiler_params=_cp(("arbitrary",)),
        name="moe_experts",
    )(block_e, n_used, xs, w_gu, b_gu.reshape(n_layer, n_exp, 1, de2), w_dn,
      b_dn.reshape(n_layer, n_exp, 1, d))


def _router_kernel(lg_ref, tri_ref, r_ref, cnt_ref, carry):
    i = pl.program_id(0)

    @pl.when(i == 0)
    def _():
        carry[...] = jnp.zeros_like(carry)

    lane = lax.broadcasted_iota(jnp.int32, lg_ref.shape, 1)
    lanef = lane.astype(F32)
    lg = jnp.where(lane < N_EXPERTS, lg_ref[...], NEG_BIG)
    tops, idxs, hots = [], [], []
    for _ in range(TOP_K):
        m = jnp.max(lg, axis=-1, keepdims=True)
        idx = jnp.min(jnp.where(lg == m, lanef, float(LANES)), axis=-1, keepdims=True)
        hot = lanef == idx
        lg = jnp.where(hot, NEG_BIG, lg)
        tops.append(m)
        idxs.append(idx)
        hots.append(hot)
    ex = [jnp.exp(m - tops[0]) for m in tops]
    den = ex[0] + ex[1] + ex[2] + ex[3]
    oh = jnp.where(hots[0] | hots[1] | hots[2] | hots[3], 1.0, 0.0)
    before = carry[...] + jnp.dot(tri_ref[...], oh.astype(BF16), preferred_element_type=F32)
    rec = jnp.zeros(lg_ref.shape, F32)
    for k in range(TOP_K):
        rank = jnp.sum(jnp.where(hots[k], before, 0.0), axis=-1, keepdims=True)
        rec = jnp.where(lane == k, idxs[k], rec)
        rec = jnp.where(lane == TOP_K + k, ex[k] / den, rec)
        rec = jnp.where(lane == 2 * TOP_K + k, rank, rec)
    r_ref[...] = rec
    carry[...] = carry[...] + jnp.sum(oh, axis=0, keepdims=True)
    cnt_ref[...] = carry[...]


def _route(logits):
    n_tok = logits.shape[0]
    tb = min(ROUTE_TB, n_tok)
    tri = jnp.asarray(np.tril(np.ones((tb, tb), np.float32), -1), BF16)
    return pl.pallas_call(
        _router_kernel,
        out_shape=[jax.ShapeDtypeStruct((n_tok, LANES), F32), jax.ShapeDtypeStruct((1, LANES), F32)],
        grid=(n_tok // tb,),
        in_specs=[pl.BlockSpec((tb, LANES), lambda i: (i, 0)), pl.BlockSpec((tb, tb), lambda i: (0, 0))],
        out_specs=[pl.BlockSpec((tb, LANES), lambda i: (i, 0)), pl.BlockSpec((1, LANES), lambda i: (0, 0))],
        scratch_shapes=[pltpu.VMEM((1, LANES), F32)],
        compiler_params=_cp(("arbitrary",)),
        name="router",
    )(logits, tri)


def _moe(h, logits, layer, wgu, bgu, wdn, bdn, tm):
    n_tok, d = h.shape
    n_slot = n_tok * TOP_K
    rec, cnt = _route(logits)
    top_e = rec[:, 0:TOP_K].astype(jnp.int32)
    rank = rec[:, 2 * TOP_K:3 * TOP_K].astype(jnp.int32)
    counts = cnt[0, 0:N_EXPERTS].astype(jnp.int32)
    first = jnp.cumsum(counts) - counts
    padded = (counts + tm - 1) // tm * tm
    pad_end = jnp.cumsum(padded)
    pad_start = pad_end - padded
    n_blocks = -(-n_slot // tm) + N_EXPERTS
    cap = n_blocks * tm
    blk_row = jnp.arange(n_blocks, dtype=jnp.int32) * tm
    block_e = jnp.minimum(jnp.sum(pad_end[None, :] <= blk_row[:, None], axis=1),
                          N_EXPERTS - 1).astype(jnp.int32)
    n_used = (pad_end[-1:] // tm).astype(jnp.int32)
    experts = jnp.arange(N_EXPERTS, dtype=jnp.int32)
    pos = jnp.sum(jnp.where(top_e[..., None] == experts, pad_start, 0), axis=-1) + rank
    order = jnp.argsort(pos.reshape(-1)).astype(jnp.int32)
    row_e = jnp.repeat(block_e, tm)
    off = jnp.arange(cap, dtype=jnp.int32) - pad_start[row_e]
    src = jnp.clip(first[row_e] + off, 0, n_slot - 1)
    slot_tok = jnp.where(off < counts[row_e], order.at[src].get(mode='promise_in_bounds') // TOP_K, 0)
    xs = h.at[slot_tok].get(mode='promise_in_bounds')
    ys = _moe_experts(xs, block_e, n_used, layer, wgu, bgu, wdn, bdn, tm)
    yk = ys.at[pos.T.reshape(-1)].get(mode='promise_in_bounds').reshape(TOP_K, n_tok, d)
    return yk, rec


def _pad_rows(w, start, total):
    return jnp.zeros((total, w.shape[1]), w.dtype).at[start:start + w.shape[0]].set(w)


def kernel(x, c, ctx, c_ctx, w_ada, b_ada, w_in, w_vdown, w_fmap, ts_prev, ts_next, rw_w0, rw_w2, rw_a0, rw_a2, rw_g2, rw_kk, rw_ka, rw_rk, rw_gn_g, rw_gn_b, rw_v0, rw_v2, na_rpb, w_out, ln1_g, ln1_b, w_router, b_router, w_gu, b_gu, w_dn, b_dn, ln2_g, ln2_b):
    bsz, n_lat, d = x.shape
    c_len = ctx.shape[1]
    depth = w_in.shape[0]
    f_dim = w_fmap.shape[1]
    rw_dim = rw_kk.shape[1]
    alpha = (2 * depth) ** 0.25
    n_pair = rw_dim // LANES

    tm_lat = min(512, n_lat)
    tm_ctx = min(256, c_len)
    tb_ctx = min(128, c_len)
    tm_moe = 512
    n1_lat, n2_lat = _split_len(n_lat)
    n1_ctx, n2_ctx = _split_len(c_len)

    cond = jnp.zeros((SUBLANES, d), F32).at[0:bsz].set(c).at[bsz].set(c_ctx)
    ada = _ada_all(cond, w_ada, b_ada)

    xl, xc = x, ctx
    vf_l = vf_c = None
    o_r = f_dim
    o_lora = f_dim + 3 * rw_dim
    o_q = o_lora + 2 * DECAY_LORA + 2 * ICLR_LORA + 2 * GATE_LORA
    for l in range(depth):
        last = l == depth - 1
        mods = ada[l].reshape(SUBLANES, 6, d)
        lat = lambda k: mods[0:bsz, k][:, None, :]
        cx = lambda k: jnp.broadcast_to(mods[bsz, k][None, None, :], (bsz, 1, d))

        wi = w_in[l]
        vdown = jnp.zeros((d, LANES), F32)
        if l > 0:
            vdown = vdown.at[:, 0:VRES_LORA].set(w_vdown[l - 1])
        wf = wi[:, 0:f_dim].astype(BF16)
        wr = wi[:, o_r:o_lora].astype(BF16)
        wl = jnp.concatenate([wi[:, o_lora:o_q], vdown], axis=1).astype(BF16)
        wq = wi[:, o_q:].astype(BF16)
        f_l, rkv_l, lora_l, qkv_l = _inproj(xl, lat(1), lat(0), wf, wr, wl, wq, tm_lat)
        f_c, rkv_c, lora_c, qkv_c = _inproj(xc, cx(1), cx(0), wf, wr, wl, wq, tm_ctx)

        if l == 0:
            v0 = v2pad = None
        else:
            v0 = rw_v0[l - 1]
            v2pad = _pad_rows(rw_v2[l - 1], 0, LANES)
        prep_c, vf_c = _rwkv_prep(rkv_c, lora_c, ts_prev[l], ts_next[l], rw_kk[l], vf_c, v0, v2pad,
                                  tb_ctx)
        prep_l, vf_l = _rwkv_prep(rkv_l, lora_l, ts_prev[l], ts_next[l], rw_kk[l], vf_l, v0, v2pad,
                                  tm_lat)
        prm = (rw_w0[l],
               jnp.stack([_pad_rows(rw_w2[l, dr], dr * DECAY_LORA, LANES) for dr in range(2)]),
               rw_a0[l],
               jnp.stack([_pad_rows(rw_a2[l, dr], dr * ICLR_LORA, LANES) for dr in range(2)]),
               rw_g2[l], rw_ka[l], rw_rk[l].reshape(-1), rw_gn_g[l], rw_gn_b[l])
        s0 = jnp.zeros((2 * bsz * n_pair, LANES, LANES), F32)
        rwf_c, rwb_c, s_ctx = _rwkv_bidir(prep_c, lora_c, s0, prm)
        rwf_l, rwb_l, _ = _rwkv_bidir(prep_l, lora_l, s_ctx, prm)

        att_l = _na_attention(qkv_l, qkv_c, _na_bias_table(na_rpb[l]))
        four_l = _fourier(f_l, n1_lat, n2_lat)

        wfm = w_fmap[l].astype(BF16)
        wo = w_out[l].astype(BF16)
        wr_pad = jnp.zeros((d, LANES), F32).at[:, 0:N_EXPERTS].set(w_router[l])
        br_pad = jnp.zeros((LANES,), F32).at[0:N_EXPERTS].set(b_router[l])
        xl, hl, lg_l = _outproj(alpha, four_l, rwf_l, rwb_l, att_l, xl, lat(2), lat(4), lat(3), wfm, wo,
                                ln1_g[l], ln1_b[l], wr_pad, br_pad, tm_lat)
        if last:
            yk, gates = _moe(hl.reshape(-1, d), lg_l.reshape(-1, LANES), l, w_gu, b_gu, w_dn, b_dn,
                             tm_moe)
            xl = _ln2(alpha, xl, yk, gates, 0, lat(5), ln2_g[l], ln2_b[l], tm_lat)
        else:
            att_c = _ctx_attention(qkv_c)
            four_c = _fourier(f_c, n1_ctx, n2_ctx)
            xc, hc, lg_c = _outproj(alpha, four_c, rwf_c, rwb_c, att_c, xc, cx(2), cx(4), cx(3), wfm, wo,
                                    ln1_g[l], ln1_b[l], wr_pad, br_pad, tm_ctx)
            h_all = jnp.concatenate([hl.reshape(-1, d), hc.reshape(-1, d)], axis=0)
            lg_all = jnp.concatenate([lg_l.reshape(-1, LANES), lg_c.reshape(-1, LANES)], axis=0)
            yk, gates = _moe(h_all, lg_all, l, w_gu, b_gu, w_dn, b_dn, tm_moe)
            n_l = bsz * n_lat
            xl = _ln2(alpha, xl, yk, gates, 0, lat(5), ln2_g[l], ln2_b[l], tm_lat)
            xc = _ln2(alpha, xc, yk, gates, n_l, cx(5), ln2_g[l], ln2_b[l], tm_ctx)
    return xl
```

```python
import functools

import numpy as np
import jax
import jax.numpy as jnp
from jax import lax
from jax.experimental import pallas as pl
from jax.experimental.pallas import tpu as pltpu

F32 = jnp.float32
BF16 = jnp.bfloat16
HI = lax.Precision.HIGHEST

GRID_W = 64
HEAD_DIM = 64
WIN_ROWS = 8
WIN_COLS = 16
N_EXPERTS = 32
TOP_K = 4
SWIGLU_LIMIT = 7.0
SWIGLU_ALPHA = 1.702
LN_EPS = 1e-5
GN_EPS = HEAD_DIM * 1e-5
DECAY_LORA = 64
ICLR_LORA = 64
GATE_LORA = 128
VRES_LORA = 32

LANES = 128
SUBLANES = 8
VMEM_LIMIT = 56 * 1024 * 1024

CHUNK = 64
NEG_BIG = -1e30
ROUTE_TB = 512


def _cp(sem, vmem=VMEM_LIMIT):
    return pltpu.CompilerParams(dimension_semantics=sem, vmem_limit_bytes=vmem)


def _dot(a, b, prec=HI):
    return jnp.dot(a, b, precision=prec, preferred_element_type=F32)


def _bdot(a, b):
    return jnp.dot(a.astype(BF16), b.astype(BF16), preferred_element_type=F32)


def _bdot_nt(a, b):
    return lax.dot_general(a.astype(BF16), b.astype(BF16), (((1,), (1,)), ((), ())),
                           preferred_element_type=F32)


def _sigmoid(x):
    return 1.0 / (1.0 + jnp.exp(-x))


def _split2(x):
    x1 = x.astype(BF16)
    x2 = (x - x1.astype(F32)).astype(BF16)
    return x1, x2


def _lhs3(x):
    x1, x2 = _split2(x)
    return jnp.concatenate([x1, x2, x1], axis=1)


def _rhs3(y):
    y1, y2 = _split2(y)
    return jnp.concatenate([y1, y1, y2], axis=0)


def _mm3(a, b):
    return jnp.dot(_lhs3(a), _rhs3(b), preferred_element_type=F32)


def _head_sum(x, ones3):
    outs = []
    for j in range(x.shape[1] // LANES):
        xs = x[:, j * LANES:(j + 1) * LANES]
        x1 = xs.astype(BF16)
        r1 = xs - x1.astype(F32)
        x2 = r1.astype(BF16)
        x3 = (r1 - x2.astype(F32)).astype(BF16)
        outs.append(jnp.dot(jnp.concatenate([x1, x2, x3], axis=1), ones3,
                            preferred_element_type=F32))
    return jnp.concatenate(outs, axis=1)


def _ada_kernel(c_ref, w_ref, b_ref, o_ref):
    c = c_ref[...]
    o_ref[0] = _dot(c * _sigmoid(c), w_ref[0]) + b_ref[0]


def _ada_all(cond, w_ada, b_ada):
    n_layer, d, n = w_ada.shape
    tn = 1536
    return pl.pallas_call(
        _ada_kernel,
        out_shape=jax.ShapeDtypeStruct((n_layer, SUBLANES, n), F32),
        grid=(n_layer, n // tn),
        in_specs=[pl.BlockSpec((SUBLANES, d), lambda l, j: (0, 0)),
                  pl.BlockSpec((1, d, tn), lambda l, j: (l, 0, j)),
                  pl.BlockSpec((1, 1, tn), lambda l, j: (l, 0, j))],
        out_specs=pl.BlockSpec((1, SUBLANES, tn), lambda l, j: (l, 0, j)),
        compiler_params=_cp(("arbitrary", "arbitrary")),
        name="ada",
    )(cond, w_ada, b_ada.reshape(n_layer, 1, n))


def _inproj_kernel(x_ref, sc_ref, sh_ref, wf_ref, wr_ref, wl_ref, wq_ref,
                   of_ref, or_ref, ol_ref, oq_ref):
    xm = (x_ref[0] * (1.0 + sc_ref[0]) + sh_ref[0]).astype(BF16)
    of_ref[0] = jnp.dot(xm, wf_ref[...], preferred_element_type=F32)
    or_ref[0] = jnp.dot(xm, wr_ref[...], preferred_element_type=F32)
    ol_ref[0] = jnp.dot(xm, wl_ref[...], preferred_element_type=F32)
    oq_ref[0] = jnp.dot(xm, wq_ref[...], preferred_element_type=F32)


def _inproj(x, sc, sh, wf, wr, wl, wq, tm):
    g, m, d = x.shape
    ws = (wf, wr, wl, wq)
    mod = pl.BlockSpec((1, 1, d), lambda b, i: (b, 0, 0))
    return pl.pallas_call(
        _inproj_kernel,
        out_shape=[jax.ShapeDtypeStruct((g, m, w.shape[1]), F32) for w in ws],
        grid=(g, m // tm),
        in_specs=[pl.BlockSpec((1, tm, d), lambda b, i: (b, i, 0)), mod, mod]
                 + [pl.BlockSpec(w.shape, lambda b, i: (0, 0)) for w in ws],
        out_specs=[pl.BlockSpec((1, tm, w.shape[1]), lambda b, i: (b, i, 0)) for w in ws],
        compiler_params=_cp(("parallel", "parallel")),
        name="inproj",
    )(x, sc, sh, *ws)


def _dft_consts(t_len, n1, n2, tb2, width):
    groups = width // HEAD_DIM
    j = np.arange(HEAD_DIM)
    ang = 2.0 * np.pi * np.outer(j, j) / HEAD_DIM
    eye = np.eye(groups)
    cbd = np.kron(eye, np.cos(ang)) / np.sqrt(HEAD_DIM)
    sbd = np.kron(eye, np.sin(ang)) / np.sqrt(HEAD_DIM)
    a1 = 2.0 * np.pi * np.outer(np.arange(n1), np.arange(n1)) / n1
    c1, s1 = np.cos(a1), np.sin(a1)
    atw = 2.0 * np.pi * np.outer(np.arange(n1), np.arange(n2)) / t_len
    twc = np.cos(atw).reshape(n1, n2 // tb2, tb2).transpose(1, 0, 2)
    tws = np.sin(atw).reshape(n1, n2 // tb2, tb2).transpose(1, 0, 2)
    a2 = 2.0 * np.pi * np.outer(np.arange(n2), np.arange(n2)) / n2
    c2, s2 = np.cos(a2) / np.sqrt(t_len), np.sin(a2) / np.sqrt(t_len)
    return [jnp.asarray(v, F32) for v in (cbd, sbd, c1, s1, twc, tws, c2, s2)]


def _four_a_kernel(tb2, width, x_ref, cbd_ref, sbd_ref, c1_ref, s1_ref, twc_ref, tws_ref,
                   yr_ref, yi_ref):
    dot = functools.partial(jnp.dot, preferred_element_type=F32)
    c1 = _lhs3(c1_ref[...])
    s1 = _lhs3(s1_ref[...])
    cbd = _rhs3(cbd_ref[...])
    sbd = _rhs3(sbd_ref[...])
    for jj in range(tb2):
        x = _lhs3(x_ref[0, :, width * jj:width * (jj + 1)])
        zr = _rhs3(dot(x, cbd))
        zi = _rhs3(-dot(x, sbd))
        ar = dot(c1, zr) + dot(s1, zi)
        ai = dot(c1, zi) - dot(s1, zr)
        tc = twc_ref[0, :, jj:jj + 1]
        ts = tws_ref[0, :, jj:jj + 1]
        yr_ref[0, jj] = tc * ar + ts * ai
        yi_ref[0, jj] = tc * ai - ts * ar


def _four_b_kernel(yr_ref, yi_ref, c2_ref, s2_ref, o_ref):
    o_ref[0] = _mm3(c2_ref[...], yr_ref[0]) + _mm3(s2_ref[...], yi_ref[0])


def _fourier(f, n1, n2):
    bsz, t_len, width = f.shape
    tb2 = SUBLANES
    cbd, sbd, c1, s1, twc, tws, c2, s2 = _dft_consts(t_len, n1, n2, tb2, width)
    const = lambda a: pl.BlockSpec(a.shape, lambda b, j: (0,) * a.ndim)
    yr, yi = pl.pallas_call(
        functools.partial(_four_a_kernel, tb2, width),
        out_shape=[jax.ShapeDtypeStruct((bsz, n2, n1, width), F32)] * 2,
        grid=(bsz, n2 // tb2),
        in_specs=[pl.BlockSpec((1, n1, tb2 * width), lambda b, j: (b, 0, j)),
                  const(cbd), const(sbd), const(c1), const(s1),
                  pl.BlockSpec((1, n1, tb2), lambda b, j: (j, 0, 0)),
                  pl.BlockSpec((1, n1, tb2), lambda b, j: (j, 0, 0))],
        out_specs=[pl.BlockSpec((1, tb2, n1, width), lambda b, j: (b, j, 0, 0))] * 2,
        compiler_params=_cp(("parallel", "parallel")),
        name="fourier_a",
    )(f.reshape(bsz, n1, n2 * width), cbd, sbd, c1, s1, twc, tws)
    ncol = n1 * width
    tc = min(ncol, 2048)
    out = pl.pallas_call(
        _four_b_kernel,
        out_shape=jax.ShapeDtypeStruct((bsz, n2, ncol), F32),
        grid=(bsz, ncol // tc),
        in_specs=[pl.BlockSpec((1, n2, tc), lambda b, j: (b, 0, j)),
                  pl.BlockSpec((1, n2, tc), lambda b, j: (b, 0, j)),
                  const(c2), const(s2)],
        out_specs=pl.BlockSpec((1, n2, tc), lambda b, j: (b, 0, j)),
        compiler_params=_cp(("parallel", "parallel")),
        name="fourier_b",
    )(yr.reshape(bsz, n2, ncol), yi.reshape(bsz, n2, ncol), c2, s2)
    return out.reshape(bsz, t_len, width)


def _split_len(t_len):
    n1 = 1 << ((t_len.bit_length() - 1 + 1) // 2)
    return n1, t_len // n1


def _head_ones3():
    h = np.arange(LANES) // HEAD_DIM
    one = (h[:, None] == h[None, :]).astype(np.float32)
    return jnp.asarray(np.concatenate([one, one, one], axis=0), BF16)


def _rwkv_prep_kernel(first, x_ref, xp_ref, xn_ref, lo_ref, tsp_ref, tsn_ref, kk_ref,
                      ones_ref, *rest):
    if first:
        o_ref, vf_out_ref = rest
    else:
        v0_ref, v2_ref, vf_ref, o_ref = rest
    i = pl.program_id(1)
    last = pl.num_programs(1) - 1
    x = x_ref[0]
    tb, w3 = x.shape
    rw = w3 // 3
    row = lax.broadcasted_iota(jnp.int32, (tb, 1), 0)
    prev_edge = jnp.where(i == 0, 0.0, xp_ref[0, SUBLANES - 1:SUBLANES, :])
    next_edge = jnp.where(i == last, 0.0, xn_ref[0, 0:1, :])
    prev = jnp.where(row == 0, prev_edge, pltpu.roll(x, 1, 0))
    nxt = jnp.where(row == tb - 1, next_edge, pltpu.roll(x, tb - 1, 0))
    s = x + tsp_ref[...] * (prev - x) + tsn_ref[...] * (nxt - x)
    r = s[:, 0:rw]
    k = s[:, rw:2 * rw]
    v = s[:, 2 * rw:3 * rw]
    if first:
        vf_out_ref[0] = v
    else:
        mix = _sigmoid(v0_ref[...] + _mm3(lo_ref[0], v2_ref[...]))
        v = v + (vf_ref[0] - v) * mix
    kk = k * kk_ref[...]
    ss = _head_sum(kk * kk, ones_ref[...])
    kk = kk * lax.rsqrt(jnp.maximum(ss, 1e-24))
    o_ref[0, :, 0:rw] = r
    o_ref[0, :, rw:2 * rw] = k
    o_ref[0, :, 2 * rw:3 * rw] = v
    o_ref[0, :, 3 * rw:4 * rw] = kk


def _rwkv_prep(rkv, lora, ts_prev, ts_next, k_k, v_first, v0, v2pad, tb):
    g, m, w3 = rkv.shape
    rw = w3 // 3
    first = v_first is None
    nb8 = tb // SUBLANES
    n8 = m // SUBLANES
    ones3 = _head_ones3()
    row = lambda a: a.reshape(1, -1)
    vec = lambda n: pl.BlockSpec((1, n), lambda b, i: (0, 0))
    in_specs = [
        pl.BlockSpec((1, tb, w3), lambda b, i: (b, i, 0)),
        pl.BlockSpec((1, SUBLANES, w3), lambda b, i: (b, jnp.maximum(i * nb8 - 1, 0), 0)),
        pl.BlockSpec((1, SUBLANES, w3), lambda b, i: (b, jnp.minimum((i + 1) * nb8, n8 - 1), 0)),
        pl.BlockSpec((1, tb, LANES), lambda b, i: (b, i, 4)),
        vec(w3), vec(w3), vec(rw),
        pl.BlockSpec(ones3.shape, lambda b, i: (0, 0)),
    ]
    args = [rkv, rkv, rkv, lora, row(ts_prev), row(ts_next), row(k_k), ones3]
    out_main = jax.ShapeDtypeStruct((g, m, 4 * rw), F32)
    spec_main = pl.BlockSpec((1, tb, 4 * rw), lambda b, i: (b, i, 0))
    spec_v = pl.BlockSpec((1, tb, rw), lambda b, i: (b, i, 0))
    if first:
        out_shape = [out_main, jax.ShapeDtypeStruct((g, m, rw), F32)]
        out_specs = [spec_main, spec_v]
    else:
        in_specs += [vec(rw), pl.BlockSpec((LANES, rw), lambda b, i: (0, 0)), spec_v]
        args += [row(v0), v2pad, v_first]
        out_shape = out_main
        out_specs = spec_main
    res = pl.pallas_call(
        functools.partial(_rwkv_prep_kernel, first),
        out_shape=out_shape, grid=(g, m // tb), in_specs=in_specs, out_specs=out_specs,
        compiler_params=_cp(("parallel", "parallel")),
        name="rwkv_prep",
    )(*args)
    if first:
        return res[0], res[1]
    return res, v_first


def _stack_heads(x, lo_mask):
    return jnp.concatenate([jnp.where(lo_mask, x, 0.0), jnp.where(lo_mask, 0.0, x)], axis=0)


def _scan_masks(reverse):
    L = CHUNK
    L2, L4 = 2 * L, 4 * L
    ti = lax.broadcasted_iota(jnp.int32, (L4, L4), 0)
    si = lax.broadcasted_iota(jnp.int32, (L4, L4), 1)
    tq = ti % L2
    sq = si % L2
    tt = tq % L
    st = sq % L
    before = (st > tt) if reverse else (st < tt)
    keep = ((tq // L) == (sq // L)) & (before | ((ti >= L2) & (st == tt)))
    ci = lax.broadcasted_iota(jnp.int32, (L, L), 0)
    cj = lax.broadcasted_iota(jnp.int32, (L, L), 1)
    tri = jnp.where((cj >= ci) if reverse else (cj <= ci), 1.0, 0.0).astype(BF16)
    return keep, jnp.concatenate([tri, tri, tri], axis=1)


def _rwkv_bidir_kernel(pf_ref, pb_ref, lof_ref, lob_ref, s0_ref, w0_ref, w2_ref, a0_ref, a2_ref,
                       g2_ref, ka_ref, rk_ref, gng_ref, gnb_ref, ones_ref,
                       of_ref, ob_ref, so_ref, s_scr):
    i = pl.program_id(0)
    L = CHUNK
    L2, L4 = 2 * L, 4 * L
    bsz = pf_ref.shape[0]
    rw = pf_ref.shape[2] // 4
    n_pair = rw // LANES

    @pl.when(i == 0)
    def _():
        s_scr[...] = s0_ref[...]

    ones3 = ones_ref[...]
    lo_mask = lax.broadcasted_iota(jnp.int32, (1, LANES), 1) < HEAD_DIM
    ei = lax.broadcasted_iota(jnp.int32, (L2, L2), 0)
    ej = lax.broadcasted_iota(jnp.int32, (L2, L2), 1)
    eye = jnp.where(ei == ej, 1.0, 0.0)
    masks = (_scan_masks(False), _scan_masks(True))

    streams = []
    for dr, (p_ref, lo_ref) in enumerate(((pf_ref, lof_ref), (pb_ref, lob_ref))):
        for b in range(bsz):
            p = p_ref[b]
            lo = lo_ref[b]
            r = p[:, 0:rw]
            k = p[:, rw:2 * rw]
            v = p[:, 2 * rw:3 * rw]
            kk = p[:, 3 * rw:4 * rw]
            zw = w0_ref[dr] + _bdot(jnp.tanh(lo[:, 0:LANES]), w2_ref[dr])
            log_w = -(jnp.maximum(-zw, 0.0) + jnp.log(1.0 + jnp.exp(-jnp.abs(zw)))) - 0.5
            lw = -jnp.exp(log_w)
            iclr = _sigmoid(a0_ref[dr] + _bdot(lo[:, LANES:2 * LANES], a2_ref[dr]))
            kd = k * (1.0 + (iclr - 1.0) * ka_ref[...])
            gate = _bdot(_sigmoid(lo[:, (2 + dr) * LANES:(3 + dr) * LANES]), g2_ref[dr])
            bonus = _head_sum(r * kd * rk_ref[...], ones3) * v
            streams.append(dict(dr=dr, b=b, r=r, kd=kd, v=v, av=-kk, bv=kk * iclr, lw=lw,
                                gate=gate, bonus=bonus))

    chains = [(si_, j) for si_ in range(len(streams)) for j in range(n_pair)]

    def cols(si_, j, name):
        return streams[si_][name][:, j * LANES:(j + 1) * LANES]

    ops = []
    for si_, j in chains:
        dr = streams[si_]['dr']
        keep, tri3 = masks[dr]
        lwc = cols(si_, j, 'lw')
        l1 = lwc.astype(BF16)
        lr = lwc - l1.astype(F32)
        l2 = lr.astype(BF16)
        l3 = (lr - l2.astype(F32)).astype(BF16)
        cum = jnp.dot(tri3, jnp.concatenate([l1, l2, l3], axis=0), preferred_element_type=F32)
        end_row = 0 if dr else L - 1
        tot = cum[end_row:end_row + 1, :]
        g_inv = jnp.exp(-cum)
        g_end = jnp.exp(tot - cum)
        bvc = cols(si_, j, 'bv')
        kdc = cols(si_, j, 'kd')
        stk = lambda x: _stack_heads(x, lo_mask).astype(BF16)
        ops.append(dict(
            keep=keep, tot=tot,
            ar_s=jnp.concatenate([stk(cols(si_, j, 'av') * jnp.exp(cum - lwc)),
                                  stk(cols(si_, j, 'r') * jnp.exp(cum))], axis=0),
            bk_s=jnp.concatenate([stk(bvc * g_inv), stk(kdc * g_inv)], axis=0),
            v_s=stk(cols(si_, j, 'v')),
            vt_s=_stack_heads(cols(si_, j, 'v'), lo_mask).T.astype(BF16),
            bkh_s=jnp.concatenate([stk(bvc * g_end), stk(kdc * g_end)], axis=0)))

    dot = functools.partial(jnp.dot, preferred_element_type=F32)
    grams = [jnp.where(o['keep'], _bdot_nt(o['ar_s'], o['bk_s']), 0.0) for o in ops]
    cat = lambda a, b: jnp.concatenate([a, b], axis=1)

    def bdiag(a, b):
        return jnp.concatenate([cat(a, jnp.zeros_like(b)), cat(jnp.zeros_like(a), b)], axis=0)

    halves = lambda x: (x[:, 0:L2], x[:, L2:L4])
    duo = [(c, c + n_pair) for s2 in range(0, len(streams), 2)
           for c in range(s2 * n_pair, (s2 + 1) * n_pair)]
    ga = [grams[a] for a, _ in duo]
    gb = [grams[b] for _, b in duo]
    oa = [ops[a] for a, _ in duo]
    ob = [ops[b] for _, b in duo]
    tinv = [cat(eye + x[0:L2, 0:L2], eye + y[0:L2, 0:L2]) for x, y in zip(ga, gb)]
    pw = [cat(x[0:L2, 0:L2], y[0:L2, 0:L2]).astype(BF16) for x, y in zip(ga, gb)]
    pw = [dot(p, bdiag(*halves(p))).astype(BF16) for p in pw]
    for _ in range(4):
        both = [dot(jnp.concatenate([p, t.astype(BF16)], axis=0), bdiag(*halves(p)))
                for p, t in zip(pw, tinv)]
        pw = [b[0:L2].astype(BF16) for b in both]
        tinv = [t + b[L2:L4] for t, b in zip(tinv, both)]
    tinv = [t + dot(t.astype(BF16), bdiag(*halves(p))) for t, p in zip(tinv, pw)]
    x1 = [dot(cat(x[0:L2, L2:L4], y[0:L2, L2:L4]).astype(BF16), bdiag(p['v_s'], q['v_s']))
          for x, y, p, q in zip(ga, gb, oa, ob)]
    wu_a = [dot(t[:, 0:L2].astype(BF16), cat(p['ar_s'][0:L2], x[:, 0:L2].astype(BF16)))
            for t, p, x in zip(tinv, oa, x1)]
    wu_b = [dot(t[:, L2:L4].astype(BF16), cat(q['ar_s'][0:L2], x[:, L2:L4].astype(BF16)))
            for t, q, x in zip(tinv, ob, x1)]
    s_a = [s_scr[a] for a, _ in duo]
    s_b = [s_scr[b] for _, b in duo]
    hy = [_bdot_nt(jnp.concatenate([cat(wa[:, 0:L2], wb[:, 0:L2]).astype(BF16),
                                    cat(p['ar_s'][L2:L4], q['ar_s'][L2:L4])], axis=0),
                   bdiag(sa, sb))
          for wa, wb, p, q, sa, sb in zip(wu_a, wu_b, oa, ob, s_a, s_b)]
    u = [h[0:L2] + cat(wa[:, L2:L4], wb[:, L2:L4]) for h, wa, wb in zip(hy, wu_a, wu_b)]
    ub = [uu.astype(BF16) for uu in u]
    yd = [h[L2:L4] + dot(cat(x[L2:L4, :], y[L2:L4, :]).astype(BF16),
                         bdiag(jnp.concatenate([uu[:, 0:L2], p['v_s']], axis=0),
                               jnp.concatenate([uu[:, L2:L4], q['v_s']], axis=0)))
          for h, x, y, uu, p, q in zip(hy, ga, gb, ub, oa, ob)]
    for (a, b), sa, sb, uu, p, q in zip(duo, s_a, s_b, u, oa, ob):
        ds = dot(cat(cat(uu[:, 0:L2].T.astype(BF16), p['vt_s']),
                     cat(uu[:, L2:L4].T.astype(BF16), q['vt_s'])),
                 bdiag(p['bkh_s'], q['bkh_s']))
        s_scr[a] = sa * jnp.exp(p['tot']) + ds[:, 0:L2]
        s_scr[b] = sb * jnp.exp(q['tot']) + ds[:, L2:L4]
    ys = [None] * len(chains)
    for (a, b), y2 in zip(duo, yd):
        ys[a], ys[b] = y2[:, 0:L2], y2[:, L2:L4]

    for si_, st in enumerate(streams):
        y = jnp.concatenate([ys[si_ * n_pair + j][0:L] + ys[si_ * n_pair + j][L:L2]
                             for j in range(n_pair)], axis=1)
        mu = _head_sum(y, ones3) * (1.0 / HEAD_DIM)
        d = y - mu
        var = _head_sum(d * d, ones3) * (1.0 / HEAD_DIM)
        out = (d * lax.rsqrt(var + GN_EPS) * gng_ref[...] + gnb_ref[...] + st['bonus']) * st['gate']
        if st['dr'] == 0:
            of_ref[st['b']] = out
        else:
            ob_ref[st['b']] = out

    @pl.when(i == pl.num_programs(0) - 1)
    def _():
        so_ref[...] = s_scr[...]


def _rwkv_bidir(prep, lora, state0, prm):
    g, m, w4 = prep.shape
    assert g % 2 == 0 and m % CHUNK == 0
    rw = w4 // 4
    nblk = m // CHUNK
    w0, w2pad, a0, a2pad, g2, k_a, r_k, gn_g, gn_b = prm
    ones3 = _head_ones3()
    row = lambda a: a.reshape(1, -1)
    fmap = lambda i: (0, i, 0)
    bmap = lambda i: (0, nblk - 1 - i, 0)
    full = lambda a: pl.BlockSpec(a.shape, lambda i: (0,) * a.ndim)
    w0, a0 = w0[:, None, :], a0[:, None, :]
    consts = [w0, w2pad, a0, a2pad, g2, row(k_a), row(r_k), row(gn_g), row(gn_b), ones3]
    st_spec = pl.BlockSpec(state0.shape, lambda i: (0, 0, 0))
    y_f, y_b, s_out = pl.pallas_call(
        _rwkv_bidir_kernel,
        out_shape=[jax.ShapeDtypeStruct((g, m, rw), F32), jax.ShapeDtypeStruct((g, m, rw), F32),
                   jax.ShapeDtypeStruct(state0.shape, F32)],
        grid=(nblk,),
        in_specs=[pl.BlockSpec((g, CHUNK, w4), fmap), pl.BlockSpec((g, CHUNK, w4), bmap),
                  pl.BlockSpec((g, CHUNK, lora.shape[2]), fmap),
                  pl.BlockSpec((g, CHUNK, lora.shape[2]), bmap),
                  st_spec] + [full(a) for a in consts],
        out_specs=[pl.BlockSpec((g, CHUNK, rw), fmap), pl.BlockSpec((g, CHUNK, rw), bmap), st_spec],
        scratch_shapes=[pltpu.VMEM(state0.shape, F32)],
        compiler_params=_cp(("arbitrary",)),
        name="rwkv_scan",
    )(prep, prep, lora, lora, state0, *consts)
    return y_f, y_b, s_out


def _na_bias_table(rpb):
    n_head = rpb.shape[0]
    col = np.arange(GRID_W)
    col_start = np.clip(col - WIN_COLS // 2, 0, GRID_W - WIN_COLS)
    col_in = (col[None, :] >= col_start[:, None]) & (col[None, :] < col_start[:, None] + WIN_COLS)
    col_rel = np.clip(col[None, :] - col[:, None] + WIN_COLS - 1, 0, 2 * WIN_COLS - 2)
    n_rel = 2 * WIN_COLS - 1
    sel = (col_rel.reshape(-1)[None, :] == np.arange(n_rel)[:, None]).astype(np.float32)
    c = jnp.einsum('hrc,cn->hrn', rpb, jnp.asarray(sel), precision=HI)
    c = c.reshape(n_head, 2 * WIN_ROWS - 1, GRID_W, GRID_W)
    c = jnp.where(col_in[None, None], c, NEG_BIG)
    t = jnp.stack([c[:, d:d + WIN_ROWS] for d in range(WIN_ROWS)], axis=1)
    t = t.transpose(0, 1, 3, 2, 4).reshape(n_head // 2, 2, WIN_ROWS, GRID_W, WIN_ROWS * GRID_W)
    return t.astype(F32)


def _na_kernel(rows, q_ref, kp_ref, kc_ref, kn_ref, vp_ref, vc_ref, vn_ref, kx_ref, vx_ref,
               b_ref, o_ref, kcat, vcat):
    i = pl.program_id(2)
    w = GRID_W
    blk = WIN_ROWS * w
    scale = HEAD_DIM ** -0.5
    kcat[0:blk] = kp_ref[0].astype(BF16)
    kcat[blk:2 * blk] = kc_ref[0].astype(BF16)
    kcat[2 * blk:3 * blk] = kn_ref[0].astype(BF16)
    vcat[0:blk] = vp_ref[0].astype(BF16)
    vcat[blk:2 * blk] = vc_ref[0].astype(BF16)
    vcat[2 * blk:3 * blk] = vn_ref[0].astype(BF16)
    kx = kx_ref[0].astype(BF16)
    vx = vx_ref[0].astype(BF16)
    lo_mask = lax.broadcasted_iota(jnp.int32, (1, LANES), 1) < HEAD_DIM
    rng = range(WIN_ROWS)
    r0 = [jnp.clip(i * WIN_ROWS + rr - WIN_ROWS // 2, 0, rows - WIN_ROWS) for rr in rng]
    off = [pl.multiple_of((r0[rr] - (i - 1) * WIN_ROWS) * w, w) for rr in rng]
    dlt = [r0[rr] - (i * WIN_ROWS + rr) + WIN_ROWS - 1 for rr in rng]
    qs = [_stack_heads(q_ref[0, rr * w:(rr + 1) * w, :], lo_mask).astype(BF16) for rr in rng]
    s = [_bdot_nt(qs[rr], kcat[pl.ds(off[rr], blk), :]) * scale
         + jnp.concatenate([b_ref[0, 0, dlt[rr]], b_ref[0, 1, dlt[rr]]], axis=0) for rr in rng]
    sx = [_bdot_nt(q, kx) * scale for q in qs]
    mx = [jnp.maximum(jnp.max(a, axis=-1, keepdims=True), jnp.max(b, axis=-1, keepdims=True))
          for a, b in zip(s, sx)]
    p = [jnp.exp(a - m) for a, m in zip(s, mx)]
    px = [jnp.exp(a - m) for a, m in zip(sx, mx)]
    den = [jnp.sum(a, axis=-1, keepdims=True) + jnp.sum(b, axis=-1, keepdims=True)
           for a, b in zip(p, px)]
    o = [(_bdot(p[rr], vcat[pl.ds(off[rr], blk), :]) + _bdot(px[rr], vx)) / den[rr] for rr in rng]
    for rr in rng:
        o_ref[0, rr * w:(rr + 1) * w, :] = jnp.where(lo_mask, o[rr][0:w], o[rr][w:2 * w])


def _na_attention(qkv, qkv_ctx, bias_tab):
    bsz, t_len, w3 = qkv.shape
    na = w3 // 3
    n_pair = na // LANES
    rows = t_len // GRID_W
    blk = WIN_ROWS * GRID_W
    nblk = rows // WIN_ROWS
    c_len = qkv_ctx.shape[1]
    kv = lambda sel, shift: pl.BlockSpec(
        (1, blk, LANES), lambda b, j, i: (b, jnp.clip(i + shift, 0, nblk - 1), sel * n_pair + j))
    cx = lambda sel: pl.BlockSpec((1, c_len, LANES), lambda b, j, i: (b, 0, sel * n_pair + j))
    return pl.pallas_call(
        functools.partial(_na_kernel, rows),
        out_shape=jax.ShapeDtypeStruct((bsz, t_len, na), F32),
        grid=(bsz, n_pair, nblk),
        in_specs=[kv(0, 0), kv(1, -1), kv(1, 0), kv(1, 1), kv(2, -1), kv(2, 0), kv(2, 1),
                  cx(1), cx(2),
                  pl.BlockSpec((1, 2, WIN_ROWS, GRID_W, blk), lambda b, j, i: (j, 0, 0, 0, 0))],
        out_specs=pl.BlockSpec((1, blk, LANES), lambda b, j, i: (b, i, j)),
        scratch_shapes=[pltpu.VMEM((3 * blk, LANES), BF16), pltpu.VMEM((3 * blk, LANES), BF16)],
        compiler_params=_cp(("parallel", "parallel", "arbitrary")),
        name="na_attention",
    )(qkv, qkv, qkv, qkv, qkv, qkv, qkv, qkv_ctx, qkv_ctx, bias_tab)


def _ctx_attn_kernel(q_ref, k_ref, v_ref, o_ref):
    lo_mask = lax.broadcasted_iota(jnp.int32, (1, LANES), 1) < HEAD_DIM
    q = q_ref[0]
    c = q.shape[0]
    qs = _stack_heads(q, lo_mask)
    s = _bdot_nt(qs, k_ref[0]) * (HEAD_DIM ** -0.5)
    mx = jnp.max(s, axis=-1, keepdims=True)
    p = jnp.exp(s - mx)
    o = _bdot(p, v_ref[0]) / jnp.sum(p, axis=-1, keepdims=True)
    o_ref[0] = jnp.where(lo_mask, o[0:c], o[c:2 * c])


def _ctx_attention(qkv_ctx):
    bsz, c_len, w3 = qkv_ctx.shape
    na = w3 // 3
    n_pair = na // LANES
    sp = lambda sel: pl.BlockSpec((1, c_len, LANES), lambda b, j: (b, 0, sel * n_pair + j))
    return pl.pallas_call(
        _ctx_attn_kernel,
        out_shape=jax.ShapeDtypeStruct((bsz, c_len, na), F32),
        grid=(bsz, n_pair),
        in_specs=[sp(0), sp(1), sp(2)],
        out_specs=pl.BlockSpec((1, c_len, LANES), lambda b, j: (b, 0, j)),
        compiler_params=_cp(("parallel", "parallel")),
        name="ctx_attention",
    )(qkv_ctx, qkv_ctx, qkv_ctx)


def _layer_norm(z, g, b):
    mu = jnp.mean(z, axis=-1, keepdims=True)
    d = z - mu
    var = jnp.mean(d * d, axis=-1, keepdims=True)
    return d * lax.rsqrt(var + LN_EPS) * g + b


def _outproj_kernel(alpha, f_ref, rwf_ref, rwb_ref, at_ref, x_ref, ga_ref, sc_ref, sh_ref, wfm_ref,
                    wo_ref, g_ref, b_ref, wr_ref, br_ref, xo_ref, h_ref, lg_ref):
    nf = f_ref.shape[2]
    nr = rwf_ref.shape[2]
    fm = jnp.dot(f_ref[0].astype(BF16), wfm_ref[...], preferred_element_type=F32)
    rwo = (rwf_ref[0] + rwb_ref[0]).astype(BF16)
    mix = (jnp.dot(fm.astype(BF16), wo_ref[0:nf, :], preferred_element_type=F32)
           + jnp.dot(rwo, wo_ref[nf:nf + nr, :], preferred_element_type=F32)
           + jnp.dot(at_ref[0].astype(BF16), wo_ref[nf + nr:, :], preferred_element_type=F32))
    xn = _layer_norm(alpha * x_ref[0] + ga_ref[0] * mix, g_ref[...], b_ref[...])
    xo_ref[0] = xn
    h = xn * (1.0 + sc_ref[0]) + sh_ref[0]
    h_ref[0] = h
    lg_ref[0] = _mm3(h, wr_ref[...]) + br_ref[...]


def _outproj(alpha, four, rwo, rwb, att, x, ga, sc, sh, wfm, wo, ln_g, ln_b, wr_pad, br_pad, tm):
    g, m, d = x.shape
    blk = lambda a: pl.BlockSpec((1, tm, a.shape[2]), lambda b, i: (b, i, 0))
    mod = pl.BlockSpec((1, 1, d), lambda b, i: (b, 0, 0))
    cst = lambda a: pl.BlockSpec(a.shape, lambda b, i: (0, 0))
    row = lambda a: a.reshape(1, -1)
    ln_g, ln_b, br_pad = row(ln_g), row(ln_b), row(br_pad)
    return pl.pallas_call(
        functools.partial(_outproj_kernel, alpha),
        out_shape=[jax.ShapeDtypeStruct((g, m, d), F32), jax.ShapeDtypeStruct((g, m, d), F32),
                   jax.ShapeDtypeStruct((g, m, LANES), F32)],
        grid=(g, m // tm),
        in_specs=[blk(four), blk(rwo), blk(rwb), blk(att), blk(x), mod, mod, mod, cst(wfm), cst(wo),
                  cst(ln_g), cst(ln_b), cst(wr_pad), cst(br_pad)],
        out_specs=[pl.BlockSpec((1, tm, d), lambda b, i: (b, i, 0)),
                   pl.BlockSpec((1, tm, d), lambda b, i: (b, i, 0)),
                   pl.BlockSpec((1, tm, LANES), lambda b, i: (b, i, 0))],
        compiler_params=_cp(("parallel", "parallel")),
        name="outproj_ln",
    )(four, rwo, rwb, att, x, ga, sc, sh, wfm, wo, ln_g, ln_b, wr_pad, br_pad)


def _ln2_kernel(alpha, x_ref, y0_ref, y1_ref, y2_ref, y3_ref, gt_ref, ga_ref, g_ref, b_ref,
                o_ref):
    gt = gt_ref[...]
    y = (gt[:, 4:5] * y0_ref[0] + gt[:, 5:6] * y1_ref[0]
         + gt[:, 6:7] * y2_ref[0] + gt[:, 7:8] * y3_ref[0])
    o_ref[0] = _layer_norm(alpha * x_ref[0] + ga_ref[0] * y, g_ref[...], b_ref[...])


def _ln2(alpha, x, yk, gates, row0, ga, ln_g, ln_b, tm):
    g, m, d = x.shape
    nb = m // tm
    base = row0 // tm
    blk = pl.BlockSpec((1, tm, d), lambda b, i: (b, i, 0))
    vec = pl.BlockSpec((1, d), lambda b, i: (0, 0))
    ysp = lambda k: pl.BlockSpec((1, tm, d), lambda b, i: (k, base + b * nb + i, 0))
    return pl.pallas_call(
        functools.partial(_ln2_kernel, alpha),
        out_shape=jax.ShapeDtypeStruct((g, m, d), F32),
        grid=(g, nb),
        in_specs=[blk, ysp(0), ysp(1), ysp(2), ysp(3),
                  pl.BlockSpec((tm, LANES), lambda b, i: (base + b * nb + i, 0)),
                  pl.BlockSpec((1, 1, d), lambda b, i: (b, 0, 0)), vec, vec],
        out_specs=blk,
        compiler_params=_cp(("parallel", "parallel")),
        name="ln2",
    )(x, yk, yk, yk, yk, gates, ga, ln_g.reshape(1, -1), ln_b.reshape(1, -1))


def _moe_kernel(be_ref, nu_ref, x_ref, wgu_ref, bgu_ref, wdn_ref, bdn_ref, o_ref, wgu_s, wdn_s):
    i = pl.program_id(0)

    @pl.when((i == 0) | (be_ref[i] != be_ref[jnp.maximum(i - 1, 0)]))
    def _():
        wgu_s[...] = wgu_ref[0, 0].astype(BF16)
        wdn_s[...] = wdn_ref[0, 0].astype(BF16)

    @pl.when(i < nu_ref[0])
    def _():
        de = wdn_s.shape[0]
        gu = (jnp.dot(x_ref[...].astype(BF16), wgu_s[...], preferred_element_type=F32)
              + bgu_ref[0, 0])
        gg = jnp.minimum(gu[:, 0:de], SWIGLU_LIMIT)
        uu = jnp.clip(gu[:, de:2 * de], -SWIGLU_LIMIT, SWIGLU_LIMIT)
        act = (uu + 1.0) * gg * _sigmoid(SWIGLU_ALPHA * gg)
        o_ref[...] = (jnp.dot(act.astype(BF16), wdn_s[...], preferred_element_type=F32)
                      + bdn_ref[0, 0])


def _moe_experts(xs, block_e, n_used, layer, w_gu, b_gu, w_dn, b_dn, tm):
    cap, d = xs.shape
    n_layer, n_exp, _, de2 = w_gu.shape
    de = de2 // 2
    n_blocks = cap // tm
    wmap = lambda i, be, nu: (layer, be[i], 0, 0)
    grid_spec = pltpu.PrefetchScalarGridSpec(
        num_scalar_prefetch=2,
        grid=(n_blocks,),
        in_specs=[pl.BlockSpec((tm, d), lambda i, be, nu: (i, 0)),
                  pl.BlockSpec((1, 1, d, de2), wmap),
                  pl.BlockSpec((1, 1, 1, de2), wmap),
                  pl.BlockSpec((1, 1, de, d), wmap),
                  pl.BlockSpec((1, 1, 1, d), wmap)],
        out_specs=pl.BlockSpec((tm, d), lambda i, be, nu: (i, 0)),
        scratch_shapes=[pltpu.VMEM((d, de2), BF16), pltpu.VMEM((de, d), BF16)],
    )
    return pl.pallas_call(
        _moe_kernel,
        out_shape=jax.ShapeDtypeStruct((cap, d), F32),
        grid_spec=grid_spec,
        compiler_params=_cp(("arbitrary",)),
        name="moe_experts",
    )(block_e, n_used, xs, w_gu, b_gu.reshape(n_layer, n_exp, 1, de2), w_dn,
      b_dn.reshape(n_layer, n_exp, 1, d))


def _router_kernel(lg_ref, tri_ref, r_ref, cnt_ref, carry):
    i = pl.program_id(0)

    @pl.when(i == 0)
    def _():
        carry[...] = jnp.zeros_like(carry)

    lane = lax.broadcasted_iota(jnp.int32, lg_ref.shape, 1)
    lanef = lane.astype(F32)
    lg = jnp.where(lane < N_EXPERTS, lg_ref[...], NEG_BIG)
    tops, idxs, hots = [], [], []
    for _ in range(TOP_K):
        m = jnp.max(lg, axis=-1, keepdims=True)
        idx = jnp.min(jnp.where(lg == m, lanef, float(LANES)), axis=-1, keepdims=True)
        hot = lanef == idx
        lg = jnp.where(hot, NEG_BIG, lg)
        tops.append(m)
        idxs.append(idx)
        hots.append(hot)
    ex = [jnp.exp(m - tops[0]) for m in tops]
    den = ex[0] + ex[1] + ex[2] + ex[3]
    oh = jnp.where(hots[0] | hots[1] | hots[2] | hots[3], 1.0, 0.0)
    before = carry[...] + jnp.dot(tri_ref[...], oh.astype(BF16), preferred_element_type=F32)
    rec = jnp.zeros(lg_ref.shape, F32)
    for k in range(TOP_K):
        rank = jnp.sum(jnp.where(hots[k], before, 0.0), axis=-1, keepdims=True)
        rec = jnp.where(lane == k, idxs[k], rec)
        rec = jnp.where(lane == TOP_K + k, ex[k] / den, rec)
        rec = jnp.where(lane == 2 * TOP_K + k, rank, rec)
    r_ref[...] = rec
    carry[...] = carry[...] + jnp.sum(oh, axis=0, keepdims=True)
    cnt_ref[...] = carry[...]


def _route(logits):
    n_tok = logits.shape[0]
    tb = min(ROUTE_TB, n_tok)
    tri = jnp.asarray(np.tril(np.ones((tb, tb), np.float32), -1), BF16)
    return pl.pallas_call(
        _router_kernel,
        out_shape=[jax.ShapeDtypeStruct((n_tok, LANES), F32), jax.ShapeDtypeStruct((1, LANES), F32)],
        grid=(n_tok // tb,),
        in_specs=[pl.BlockSpec((tb, LANES), lambda i: (i, 0)), pl.BlockSpec((tb, tb), lambda i: (0, 0))],
        out_specs=[pl.BlockSpec((tb, LANES), lambda i: (i, 0)), pl.BlockSpec((1, LANES), lambda i: (0, 0))],
        scratch_shapes=[pltpu.VMEM((1, LANES), F32)],
        compiler_params=_cp(("arbitrary",)),
        name="router",
    )(logits, tri)


def _moe(h, logits, layer, wgu, bgu, wdn, bdn, tm):
    n_tok, d = h.shape
    n_slot = n_tok * TOP_K
    rec, cnt = _route(logits)
    top_e = rec[:, 0:TOP_K].astype(jnp.int32)
    rank = rec[:, 2 * TOP_K:3 * TOP_K].astype(jnp.int32)
    counts = cnt[0, 0:N_EXPERTS].astype(jnp.int32)
    first = jnp.cumsum(counts) - counts
    padded = (counts + tm - 1) // tm * tm
    pad_end = jnp.cumsum(padded)
    pad_start = pad_end - padded
    n_blocks = -(-n_slot // tm) + N_EXPERTS
    cap = n_blocks * tm
    blk_row = jnp.arange(n_blocks, dtype=jnp.int32) * tm
    block_e = jnp.minimum(jnp.sum(pad_end[None, :] <= blk_row[:, None], axis=1),
                          N_EXPERTS - 1).astype(jnp.int32)
    n_used = (pad_end[-1:] // tm).astype(jnp.int32)
    experts = jnp.arange(N_EXPERTS, dtype=jnp.int32)
    pos = jnp.sum(jnp.where(top_e[..., None] == experts, pad_start, 0), axis=-1) + rank
    order = jnp.argsort(pos.reshape(-1)).astype(jnp.int32)
    row_e = jnp.repeat(block_e, tm)
    off = jnp.arange(cap, dtype=jnp.int32) - pad_start[row_e]
    src = jnp.clip(first[row_e] + off, 0, n_slot - 1)
    slot_tok = jnp.where(off < counts[row_e], order.at[src].get(mode='promise_in_bounds') // TOP_K, 0)
    xs = h.at[slot_tok].get(mode='promise_in_bounds')
    ys = _moe_experts(xs, block_e, n_used, layer, wgu, bgu, wdn, bdn, tm)
    yk = ys.at[pos.T.reshape(-1)].get(mode='promise_in_bounds').reshape(TOP_K, n_tok, d)
    return yk, rec


def _pad_rows(w, start, total):
    return jnp.zeros((total, w.shape[1]), w.dtype).at[start:start + w.shape[0]].set(w)


def kernel(x, c, ctx, c_ctx, w_ada, b_ada, w_in, w_vdown, w_fmap, ts_prev, ts_next, rw_w0, rw_w2, rw_a0, rw_a2, rw_g2, rw_kk, rw_ka, rw_rk, rw_gn_g, rw_gn_b, rw_v0, rw_v2, na_rpb, w_out, ln1_g, ln1_b, w_router, b_router, w_gu, b_gu, w_dn, b_dn, ln2_g, ln2_b):
    bsz, n_lat, d = x.shape
    c_len = ctx.shape[1]
    depth = w_in.shape[0]
    f_dim = w_fmap.shape[1]
    rw_dim = rw_kk.shape[1]
    alpha = (2 * depth) ** 0.25
    n_pair = rw_dim // LANES

    tm_lat = min(512, n_lat)
    tm_ctx = min(256, c_len)
    tb_ctx = min(128, c_len)
    tm_moe = 512
    n1_lat, n2_lat = _split_len(n_lat)
    n1_ctx, n2_ctx = _split_len(c_len)

    cond = jnp.zeros((SUBLANES, d), F32).at[0:bsz].set(c).at[bsz].set(c_ctx)
    ada = _ada_all(cond, w_ada, b_ada)

    xl, xc = x, ctx
    vf_l = vf_c = None
    o_r = f_dim
    o_lora = f_dim + 3 * rw_dim
    o_q = o_lora + 2 * DECAY_LORA + 2 * ICLR_LORA + 2 * GATE_LORA
    for l in range(depth):
        last = l == depth - 1
        mods = ada[l].reshape(SUBLANES, 6, d)
        lat = lambda k: mods[0:bsz, k][:, None, :]
        cx = lambda k: jnp.broadcast_to(mods[bsz, k][None, None, :], (bsz, 1, d))

        wi = w_in[l]
        vdown = jnp.zeros((d, LANES), F32)
        if l > 0:
            vdown = vdown.at[:, 0:VRES_LORA].set(w_vdown[l - 1])
        wf = wi[:, 0:f_dim].astype(BF16)
        wr = wi[:, o_r:o_lora].astype(BF16)
        wl = jnp.concatenate([wi[:, o_lora:o_q], vdown], axis=1).astype(BF16)
        wq = wi[:, o_q:].astype(BF16)
        f_l, rkv_l, lora_l, qkv_l = _inproj(xl, lat(1), lat(0), wf, wr, wl, wq, tm_lat)
        f_c, rkv_c, lora_c, qkv_c = _inproj(xc, cx(1), cx(0), wf, wr, wl, wq, tm_ctx)

        if l == 0:
            v0 = v2pad = None
        else:
            v0 = rw_v0[l - 1]
            v2pad = _pad_rows(rw_v2[l - 1], 0, LANES)
        prep_c, vf_c = _rwkv_prep(rkv_c, lora_c, ts_prev[l], ts_next[l], rw_kk[l], vf_c, v0, v2pad,
                                  tb_ctx)
        prep_l, vf_l = _rwkv_prep(rkv_l, lora_l, ts_prev[l], ts_next[l], rw_kk[l], vf_l, v0, v2pad,
                                  tm_lat)
        prm = (rw_w0[l],
               jnp.stack([_pad_rows(rw_w2[l, dr], dr * DECAY_LORA, LANES) for dr in range(2)]),
               rw_a0[l],
               jnp.stack([_pad_rows(rw_a2[l, dr], dr * ICLR_LORA, LANES) for dr in range(2)]),
               rw_g2[l], rw_ka[l], rw_rk[l].reshape(-1), rw_gn_g[l], rw_gn_b[l])
        s0 = jnp.zeros((2 * bsz * n_pair, LANES, LANES), F32)
        rwf_c, rwb_c, s_ctx = _rwkv_bidir(prep_c, lora_c, s0, prm)
        rwf_l, rwb_l, _ = _rwkv_bidir(prep_l, lora_l, s_ctx, prm)

        att_l = _na_attention(qkv_l, qkv_c, _na_bias_table(na_rpb[l]))
        four_l = _fourier(f_l, n1_lat, n2_lat)

        wfm = w_fmap[l].astype(BF16)
        wo = w_out[l].astype(BF16)
        wr_pad = jnp.zeros((d, LANES), F32).at[:, 0:N_EXPERTS].set(w_router[l])
        br_pad = jnp.zeros((LANES,), F32).at[0:N_EXPERTS].set(b_router[l])
        xl, hl, lg_l = _outproj(alpha, four_l, rwf_l, rwb_l, att_l, xl, lat(2), lat(4), lat(3), wfm, wo,
                                ln1_g[l], ln1_b[l], wr_pad, br_pad, tm_lat)
        if last:
            yk, gates = _moe(hl.reshape(-1, d), lg_l.reshape(-1, LANES), l, w_gu, b_gu, w_dn, b_dn,
                             tm_moe)
            xl = _ln2(alpha, xl, yk, gates, 0, lat(5), ln2_g[l], ln2_b[l], tm_lat)
        else:
            att_c = _ctx_attention(qkv_c)
            four_c = _fourier(f_c, n1_ctx, n2_ctx)
            xc, hc, lg_c = _outproj(alpha, four_c, rwf_c, rwb_c, att_c, xc, cx(2), cx(4), cx(3), wfm, wo,
                                    ln1_g[l], ln1_b[l], wr_pad, br_pad, tm_ctx)
            h_all = jnp.concatenate([hl.reshape(-1, d), hc.reshape(-1, d)], axis=0)
            lg_all = jnp.concatenate([lg_l.reshape(-1, LANES), lg_c.reshape(-1, LANES)], axis=0)
            yk, gates = _moe(h_all, lg_all, l, w_gu, b_gu, w_dn, b_dn, tm_moe)
            n_l = bsz * n_lat
            xl = _ln2(alpha, xl, yk, gates, 0, lat(5), ln2_g[l], ln2_b[l], tm_lat)
            xc = _ln2(alpha, xc, yk, gates, n_l, cx(5), ln2_g[l], ln2_b[l], tm_ctx)
    return xl
```

```python
import functools

import numpy as np
import jax
import jax.numpy as jnp
from jax import lax
from jax.experimental import pallas as pl
from jax.experimental.pallas import tpu as pltpu

F32 = jnp.float32
BF16 = jnp.bfloat16
HI = lax.Precision.HIGHEST

GRID_W = 64
HEAD_DIM = 64
WIN_ROWS = 8
WIN_COLS = 16
N_EXPERTS = 32
TOP_K = 4
SWIGLU_LIMIT = 7.0
SWIGLU_ALPHA = 1.702
LN_EPS = 1e-5
GN_EPS = HEAD_DIM * 1e-5
DECAY_LORA = 64
ICLR_LORA = 64
GATE_LORA = 128
VRES_LORA = 32

LANES = 128
SUBLANES = 8
VMEM_LIMIT = 56 * 1024 * 1024

CHUNK = 64
NEG_BIG = -1e30
ROUTE_TB = 512


def _cp(sem, vmem=VMEM_LIMIT):
    return pltpu.CompilerParams(dimension_semantics=sem, vmem_limit_bytes=vmem)


def _dot(a, b, prec=HI):
    return jnp.dot(a, b, precision=prec, preferred_element_type=F32)


def _bdot(a, b):
    return jnp.dot(a.astype(BF16), b.astype(BF16), preferred_element_type=F32)


def _bdot_nt(a, b):
    return lax.dot_general(a.astype(BF16), b.astype(BF16), (((1,), (1,)), ((), ())),
                           preferred_element_type=F32)


def _sigmoid(x):
    return 1.0 / (1.0 + jnp.exp(-x))


def _split2(x):
    x1 = x.astype(BF16)
    x2 = (x - x1.astype(F32)).astype(BF16)
    return x1, x2


def _lhs3(x):
    x1, x2 = _split2(x)
    return jnp.concatenate([x1, x2, x1], axis=1)


def _rhs3(y):
    y1, y2 = _split2(y)
    return jnp.concatenate([y1, y1, y2], axis=0)


def _mm3(a, b):
    return jnp.dot(_lhs3(a), _rhs3(b), preferred_element_type=F32)


def _head_sum(x, ones3):
    outs = []
    for j in range(x.shape[1] // LANES):
        xs = x[:, j * LANES:(j + 1) * LANES]
        x1 = xs.astype(BF16)
        r1 = xs - x1.astype(F32)
        x2 = r1.astype(BF16)
        x3 = (r1 - x2.astype(F32)).astype(BF16)
        outs.append(jnp.dot(jnp.concatenate([x1, x2, x3], axis=1), ones3,
                            preferred_element_type=F32))
    return jnp.concatenate(outs, axis=1)


def _ada_kernel(c_ref, w_ref, b_ref, o_ref):
    c = c_ref[...]
    o_ref[0] = _dot(c * _sigmoid(c), w_ref[0]) + b_ref[0]


def _ada_all(cond, w_ada, b_ada):
    n_layer, d, n = w_ada.shape
    tn = 1536
    return pl.pallas_call(
        _ada_kernel,
        out_shape=jax.ShapeDtypeStruct((n_layer, SUBLANES, n), F32),
        grid=(n_layer, n // tn),
        in_specs=[pl.BlockSpec((SUBLANES, d), lambda l, j: (0, 0)),
                  pl.BlockSpec((1, d, tn), lambda l, j: (l, 0, j)),
                  pl.BlockSpec((1, 1, tn), lambda l, j: (l, 0, j))],
        out_specs=pl.BlockSpec((1, SUBLANES, tn), lambda l, j: (l, 0, j)),
        compiler_params=_cp(("arbitrary", "arbitrary")),
        name="ada",
    )(cond, w_ada, b_ada.reshape(n_layer, 1, n))


def _inproj_kernel(x_ref, sc_ref, sh_ref, wf_ref, wr_ref, wl_ref, wq_ref,
                   of_ref, or_ref, ol_ref, oq_ref):
    xm = (x_ref[0] * (1.0 + sc_ref[0]) + sh_ref[0]).astype(BF16)
    of_ref[0] = jnp.dot(xm, wf_ref[...], preferred_element_type=F32)
    or_ref[0] = jnp.dot(xm, wr_ref[...], preferred_element_type=F32)
    ol_ref[0] = jnp.dot(xm, wl_ref[...], preferred_element_type=F32)
    oq_ref[0] = jnp.dot(xm, wq_ref[...], preferred_element_type=F32).astype(BF16)


def _inproj(x, sc, sh, wf, wr, wl, wq, tm):
    g, m, d = x.shape
    ws = (wf, wr, wl, wq)
    mod = pl.BlockSpec((1, 1, d), lambda b, i: (b, 0, 0))
    return pl.pallas_call(
        _inproj_kernel,
        out_shape=[jax.ShapeDtypeStruct((g, m, w.shape[1]), dt)
                   for w, dt in zip(ws, (F32, F32, F32, BF16))],
        grid=(g, m // tm),
        in_specs=[pl.BlockSpec((1, tm, d), lambda b, i: (b, i, 0)), mod, mod]
                 + [pl.BlockSpec(w.shape, lambda b, i: (0, 0)) for w in ws],
        out_specs=[pl.BlockSpec((1, tm, w.shape[1]), lambda b, i: (b, i, 0)) for w in ws],
        compiler_params=_cp(("parallel", "parallel")),
        name="inproj",
    )(x, sc, sh, *ws)


def _dft_consts(t_len, n1, n2, tb2, width):
    groups = width // HEAD_DIM
    j = np.arange(HEAD_DIM)
    ang = 2.0 * np.pi * np.outer(j, j) / HEAD_DIM
    eye = np.eye(groups)
    cbd = np.kron(eye, np.cos(ang)) / np.sqrt(HEAD_DIM)
    sbd = np.kron(eye, np.sin(ang)) / np.sqrt(HEAD_DIM)
    a1 = 2.0 * np.pi * np.outer(np.arange(n1), np.arange(n1)) / n1
    c1, s1 = np.cos(a1), np.sin(a1)
    atw = 2.0 * np.pi * np.outer(np.arange(n1), np.arange(n2)) / t_len
    twc = np.cos(atw).reshape(n1, n2 // tb2, tb2).transpose(1, 0, 2)
    tws = np.sin(atw).reshape(n1, n2 // tb2, tb2).transpose(1, 0, 2)
    a2 = 2.0 * np.pi * np.outer(np.arange(n2), np.arange(n2)) / n2
    c2, s2 = np.cos(a2) / np.sqrt(t_len), np.sin(a2) / np.sqrt(t_len)
    return [jnp.asarray(v, F32) for v in (cbd, sbd, c1, s1, twc, tws, c2, s2)]


def _four_a_kernel(tb2, width, x_ref, cbd_ref, sbd_ref, c1_ref, s1_ref, twc_ref, tws_ref,
                   yr_ref, yi_ref):
    dot = functools.partial(jnp.dot, preferred_element_type=F32)
    c1 = _lhs3(c1_ref[...])
    s1 = _lhs3(s1_ref[...])
    cbd = _rhs3(cbd_ref[...])
    sbd = _rhs3(sbd_ref[...])
    for jj in range(tb2):
        x = _lhs3(x_ref[0, :, width * jj:width * (jj + 1)])
        zr = _rhs3(dot(x, cbd))
        zi = _rhs3(-dot(x, sbd))
        ar = dot(c1, zr) + dot(s1, zi)
        ai = dot(c1, zi) - dot(s1, zr)
        tc = twc_ref[0, :, jj:jj + 1]
        ts = tws_ref[0, :, jj:jj + 1]
        yr_ref[0, jj] = tc * ar + ts * ai
        yi_ref[0, jj] = tc * ai - ts * ar


def _four_b_kernel(yr_ref, yi_ref, c2_ref, s2_ref, o_ref):
    o_ref[0] = (_mm3(c2_ref[...], yr_ref[0]) + _mm3(s2_ref[...], yi_ref[0])).astype(BF16)


def _fourier(f, n1, n2):
    bsz, t_len, width = f.shape
    tb2 = SUBLANES
    cbd, sbd, c1, s1, twc, tws, c2, s2 = _dft_consts(t_len, n1, n2, tb2, width)
    const = lambda a: pl.BlockSpec(a.shape, lambda b, j: (0,) * a.ndim)
    yr, yi = pl.pallas_call(
        functools.partial(_four_a_kernel, tb2, width),
        out_shape=[jax.ShapeDtypeStruct((bsz, n2, n1, width), F32)] * 2,
        grid=(bsz, n2 // tb2),
        in_specs=[pl.BlockSpec((1, n1, tb2 * width), lambda b, j: (b, 0, j)),
                  const(cbd), const(sbd), const(c1), const(s1),
                  pl.BlockSpec((1, n1, tb2), lambda b, j: (j, 0, 0)),
                  pl.BlockSpec((1, n1, tb2), lambda b, j: (j, 0, 0))],
        out_specs=[pl.BlockSpec((1, tb2, n1, width), lambda b, j: (b, j, 0, 0))] * 2,
        compiler_params=_cp(("parallel", "parallel")),
        name="fourier_a",
    )(f.reshape(bsz, n1, n2 * width), cbd, sbd, c1, s1, twc, tws)
    ncol = n1 * width
    tc = min(ncol, 2048)
    out = pl.pallas_call(
        _four_b_kernel,
        out_shape=jax.ShapeDtypeStruct((bsz, n2, ncol), BF16),
        grid=(bsz, ncol // tc),
        in_specs=[pl.BlockSpec((1, n2, tc), lambda b, j: (b, 0, j)),
                  pl.BlockSpec((1, n2, tc), lambda b, j: (b, 0, j)),
                  const(c2), const(s2)],
        out_specs=pl.BlockSpec((1, n2, tc), lambda b, j: (b, 0, j)),
        compiler_params=_cp(("parallel", "parallel")),
        name="fourier_b",
    )(yr.reshape(bsz, n2, ncol), yi.reshape(bsz, n2, ncol), c2, s2)
    return out.reshape(bsz, t_len, width)


def _split_len(t_len):
    n1 = 1 << ((t_len.bit_length() - 1 + 1) // 2)
    return n1, t_len // n1


def _head_ones3():
    h = np.arange(LANES) // HEAD_DIM
    one = (h[:, None] == h[None, :]).astype(np.float32)
    return jnp.asarray(np.concatenate([one, one, one], axis=0), BF16)


def _rwkv_prep_kernel(first, x_ref, xp_ref, xn_ref, lo_ref, tsp_ref, tsn_ref, kk_ref,
                      ones_ref, *rest):
    if first:
        o_ref, vf_out_ref = rest
    else:
        v0_ref, v2_ref, vf_ref, o_ref = rest
    i = pl.program_id(1)
    last = pl.num_programs(1) - 1
    x = x_ref[0]
    tb, w3 = x.shape
    rw = w3 // 3
    row = lax.broadcasted_iota(jnp.int32, (tb, 1), 0)
    prev_edge = jnp.where(i == 0, 0.0, xp_ref[0, SUBLANES - 1:SUBLANES, :])
    next_edge = jnp.where(i == last, 0.0, xn_ref[0, 0:1, :])
    prev = jnp.where(row == 0, prev_edge, pltpu.roll(x, 1, 0))
    nxt = jnp.where(row == tb - 1, next_edge, pltpu.roll(x, tb - 1, 0))
    s = x + tsp_ref[...] * (prev - x) + tsn_ref[...] * (nxt - x)
    r = s[:, 0:rw]
    k = s[:, rw:2 * rw]
    v = s[:, 2 * rw:3 * rw]
    if first:
        vf_out_ref[0] = v
    else:
        mix = _sigmoid(v0_ref[...] + _mm3(lo_ref[0], v2_ref[...]))
        v = v + (vf_ref[0] - v) * mix
    kk = k * kk_ref[...]
    ss = _head_sum(kk * kk, ones_ref[...])
    kk = kk * lax.rsqrt(jnp.maximum(ss, 1e-24))
    o_ref[0, :, 0:rw] = r
    o_ref[0, :, rw:2 * rw] = k
    o_ref[0, :, 2 * rw:3 * rw] = v
    o_ref[0, :, 3 * rw:4 * rw] = kk


def _rwkv_prep(rkv, lora, ts_prev, ts_next, k_k, v_first, v0, v2pad, tb):
    g, m, w3 = rkv.shape
    rw = w3 // 3
    first = v_first is None
    nb8 = tb // SUBLANES
    n8 = m // SUBLANES
    ones3 = _head_ones3()
    row = lambda a: a.reshape(1, -1)
    vec = lambda n: pl.BlockSpec((1, n), lambda b, i: (0, 0))
    in_specs = [
        pl.BlockSpec((1, tb, w3), lambda b, i: (b, i, 0)),
        pl.BlockSpec((1, SUBLANES, w3), lambda b, i: (b, jnp.maximum(i * nb8 - 1, 0), 0)),
        pl.BlockSpec((1, SUBLANES, w3), lambda b, i: (b, jnp.minimum((i + 1) * nb8, n8 - 1), 0)),
        pl.BlockSpec((1, tb, LANES), lambda b, i: (b, i, 4)),
        vec(w3), vec(w3), vec(rw),
        pl.BlockSpec(ones3.shape, lambda b, i: (0, 0)),
    ]
    args = [rkv, rkv, rkv, lora, row(ts_prev), row(ts_next), row(k_k), ones3]
    out_main = jax.ShapeDtypeStruct((g, m, 4 * rw), F32)
    spec_main = pl.BlockSpec((1, tb, 4 * rw), lambda b, i: (b, i, 0))
    spec_v = pl.BlockSpec((1, tb, rw), lambda b, i: (b, i, 0))
    if first:
        out_shape = [out_main, jax.ShapeDtypeStruct((g, m, rw), F32)]
        out_specs = [spec_main, spec_v]
    else:
        in_specs += [vec(rw), pl.BlockSpec((LANES, rw), lambda b, i: (0, 0)), spec_v]
        args += [row(v0), v2pad, v_first]
        out_shape = out_main
        out_specs = spec_main
    res = pl.pallas_call(
        functools.partial(_rwkv_prep_kernel, first),
        out_shape=out_shape, grid=(g, m // tb), in_specs=in_specs, out_specs=out_specs,
        compiler_params=_cp(("parallel", "parallel")),
        name="rwkv_prep",
    )(*args)
    if first:
        return res[0], res[1]
    return res, v_first


def _stack_heads(x, lo_mask):
    return jnp.concatenate([jnp.where(lo_mask, x, 0.0), jnp.where(lo_mask, 0.0, x)], axis=0)


def _scan_masks(reverse):
    L = CHUNK
    L2, L4 = 2 * L, 4 * L
    ti = lax.broadcasted_iota(jnp.int32, (L4, L4), 0)
    si = lax.broadcasted_iota(jnp.int32, (L4, L4), 1)
    tq = ti % L2
    sq = si % L2
    tt = tq % L
    st = sq % L
    before = (st > tt) if reverse else (st < tt)
    keep = ((tq // L) == (sq // L)) & (before | ((ti >= L2) & (st == tt)))
    ci = lax.broadcasted_iota(jnp.int32, (L, L), 0)
    cj = lax.broadcasted_iota(jnp.int32, (L, L), 1)
    tri = jnp.where((cj >= ci) if reverse else (cj <= ci), 1.0, 0.0).astype(BF16)
    return keep, jnp.concatenate([tri, tri, tri], axis=1)


def _rwkv_bidir_kernel(pf_ref, pb_ref, lof_ref, lob_ref, s0_ref, w0_ref, w2_ref, a0_ref, a2_ref,
                       g2_ref, ka_ref, rk_ref, gng_ref, gnb_ref, ones_ref,
                       of_ref, ob_ref, so_ref, s_scr):
    i = pl.program_id(0)
    L = CHUNK
    L2, L4 = 2 * L, 4 * L
    bsz = pf_ref.shape[0]
    rw = pf_ref.shape[2] // 4
    n_pair = rw // LANES

    @pl.when(i == 0)
    def _():
        s_scr[...] = s0_ref[...]

    ones3 = ones_ref[...]
    lo_mask = lax.broadcasted_iota(jnp.int32, (1, LANES), 1) < HEAD_DIM
    ei = lax.broadcasted_iota(jnp.int32, (L2, L2), 0)
    ej = lax.broadcasted_iota(jnp.int32, (L2, L2), 1)
    eye = jnp.where(ei == ej, 1.0, 0.0)
    masks = (_scan_masks(False), _scan_masks(True))

    streams = []
    for dr, (p_ref, lo_ref) in enumerate(((pf_ref, lof_ref), (pb_ref, lob_ref))):
        for b in range(bsz):
            p = p_ref[b]
            lo = lo_ref[b]
            r = p[:, 0:rw]
            k = p[:, rw:2 * rw]
            v = p[:, 2 * rw:3 * rw]
            kk = p[:, 3 * rw:4 * rw]
            zw = w0_ref[dr] + _bdot(jnp.tanh(lo[:, 0:LANES]), w2_ref[dr])
            log_w = -(jnp.maximum(-zw, 0.0) + jnp.log(1.0 + jnp.exp(-jnp.abs(zw)))) - 0.5
            lw = -jnp.exp(log_w)
            iclr = _sigmoid(a0_ref[dr] + _bdot(lo[:, LANES:2 * LANES], a2_ref[dr]))
            kd = k * (1.0 + (iclr - 1.0) * ka_ref[...])
            gate = _bdot(_sigmoid(lo[:, (2 + dr) * LANES:(3 + dr) * LANES]), g2_ref[dr])
            bonus = _head_sum(r * kd * rk_ref[...], ones3) * v
            streams.append(dict(dr=dr, b=b, r=r, kd=kd, v=v, av=-kk, bv=kk * iclr, lw=lw,
                                gate=gate, bonus=bonus))

    chains = [(si_, j) for si_ in range(len(streams)) for j in range(n_pair)]

    def cols(si_, j, name):
        return streams[si_][name][:, j * LANES:(j + 1) * LANES]

    ops = []
    for si_, j in chains:
        dr = streams[si_]['dr']
        keep, tri3 = masks[dr]
        lwc = cols(si_, j, 'lw')
        l1 = lwc.astype(BF16)
        lr = lwc - l1.astype(F32)
        l2 = lr.astype(BF16)
        l3 = (lr - l2.astype(F32)).astype(BF16)
        cum = jnp.dot(tri3, jnp.concatenate([l1, l2, l3], axis=0), preferred_element_type=F32)
        end_row = 0 if dr else L - 1
        tot = cum[end_row:end_row + 1, :]
        g_inv = jnp.exp(-cum)
        g_end = jnp.exp(tot - cum)
        bvc = cols(si_, j, 'bv')
        kdc = cols(si_, j, 'kd')
        stk = lambda x: _stack_heads(x, lo_mask).astype(BF16)
        ops.append(dict(
            keep=keep, tot=tot,
            ar_s=jnp.concatenate([stk(cols(si_, j, 'av') * jnp.exp(cum - lwc)),
                                  stk(cols(si_, j, 'r') * jnp.exp(cum))], axis=0),
            bk_s=jnp.concatenate([stk(bvc * g_inv), stk(kdc * g_inv)], axis=0),
            v_s=stk(cols(si_, j, 'v')),
            vt_s=_stack_heads(cols(si_, j, 'v'), lo_mask).T.astype(BF16),
            bkh_s=jnp.concatenate([stk(bvc * g_end), stk(kdc * g_end)], axis=0)))

    dot = functools.partial(jnp.dot, preferred_element_type=F32)
    grams = [jnp.where(o['keep'], _bdot_nt(o['ar_s'], o['bk_s']), 0.0) for o in ops]
    tinv = [eye + g[0:L2, 0:L2] for g in grams]
    pw = [g[0:L2, 0:L2].astype(BF16) for g in grams]
    pw = [dot(a, a).astype(BF16) for a in pw]
    for _ in range(4):
        both = [dot(jnp.concatenate([p, t.astype(BF16)], axis=0), p) for p, t in zip(pw, tinv)]
        pw = [b[0:L2].astype(BF16) for b in both]
        tinv = [t + b[L2:L4] for t, b in zip(tinv, both)]
    tinv = [t + dot(t.astype(BF16), p) for t, p in zip(tinv, pw)]
    x1 = [dot(g[0:L2, L2:L4].astype(BF16), o['v_s']) for g, o in zip(grams, ops)]
    wu = [dot(t.astype(BF16), jnp.concatenate([o['ar_s'][0:L2], x.astype(BF16)], axis=1))
          for t, o, x in zip(tinv, ops, x1)]
    s_old = [s_scr[c] for c in range(len(chains))]
    hy = [_bdot_nt(jnp.concatenate([w[:, 0:L2].astype(BF16), o['ar_s'][L2:L4]], axis=0), s)
          for w, o, s in zip(wu, ops, s_old)]
    u = [h[0:L2] + w[:, L2:L4] for h, w in zip(hy, wu)]
    uv = [jnp.concatenate([uu.astype(BF16), o['v_s']], axis=0) for uu, o in zip(u, ops)]
    ys = [h[L2:L4] + dot(g[L2:L4, :].astype(BF16), x) for h, g, x in zip(hy, grams, uv)]
    for c, (s, uu, o) in enumerate(zip(s_old, u, ops)):
        s_scr[c] = s * jnp.exp(o['tot']) + dot(
            jnp.concatenate([uu.T.astype(BF16), o['vt_s']], axis=1), o['bkh_s'])

    for si_, st in enumerate(streams):
        y = jnp.concatenate([ys[si_ * n_pair + j][0:L] + ys[si_ * n_pair + j][L:L2]
                             for j in range(n_pair)], axis=1)
        mu = _head_sum(y, ones3) * (1.0 / HEAD_DIM)
        d = y - mu
        var = _head_sum(d * d, ones3) * (1.0 / HEAD_DIM)
        out = (d * lax.rsqrt(var + GN_EPS) * gng_ref[...] + gnb_ref[...] + st['bonus']) * st['gate']
        if st['dr'] == 0:
            of_ref[st['b']] = out
        else:
            ob_ref[st['b']] = out

    @pl.when(i == pl.num_programs(0) - 1)
    def _():
        so_ref[...] = s_scr[...]


def _rwkv_bidir(prep, lora, state0, prm):
    g, m, w4 = prep.shape
    rw = w4 // 4
    nblk = m // CHUNK
    w0, w2pad, a0, a2pad, g2, k_a, r_k, gn_g, gn_b = prm
    ones3 = _head_ones3()
    row = lambda a: a.reshape(1, -1)
    fmap = lambda i: (0, i, 0)
    bmap = lambda i: (0, nblk - 1 - i, 0)
    full = lambda a: pl.BlockSpec(a.shape, lambda i: (0,) * a.ndim)
    w0, a0 = w0[:, None, :], a0[:, None, :]
    consts = [w0, w2pad, a0, a2pad, g2, row(k_a), row(r_k), row(gn_g), row(gn_b), ones3]
    st_spec = pl.BlockSpec(state0.shape, lambda i: (0, 0, 0))
    y_f, y_b, s_out = pl.pallas_call(
        _rwkv_bidir_kernel,
        out_shape=[jax.ShapeDtypeStruct((g, m, rw), F32), jax.ShapeDtypeStruct((g, m, rw), F32),
                   jax.ShapeDtypeStruct(state0.shape, F32)],
        grid=(nblk,),
        in_specs=[pl.BlockSpec((g, CHUNK, w4), fmap), pl.BlockSpec((g, CHUNK, w4), bmap),
                  pl.BlockSpec((g, CHUNK, lora.shape[2]), fmap),
                  pl.BlockSpec((g, CHUNK, lora.shape[2]), bmap),
                  st_spec] + [full(a) for a in consts],
        out_specs=[pl.BlockSpec((g, CHUNK, rw), fmap), pl.BlockSpec((g, CHUNK, rw), bmap), st_spec],
        scratch_shapes=[pltpu.VMEM(state0.shape, F32)],
        compiler_params=_cp(("arbitrary",)),
        name="rwkv_scan",
    )(prep, prep, lora, lora, state0, *consts)
    return y_f, y_b, s_out


def _na_bias_table(rpb):
    n_head = rpb.shape[0]
    col = np.arange(GRID_W)
    col_start = np.clip(col - WIN_COLS // 2, 0, GRID_W - WIN_COLS)
    col_in = (col[None, :] >= col_start[:, None]) & (col[None, :] < col_start[:, None] + WIN_COLS)
    col_rel = np.clip(col[None, :] - col[:, None] + WIN_COLS - 1, 0, 2 * WIN_COLS - 2)
    n_rel = 2 * WIN_COLS - 1
    sel = (col_rel.reshape(-1)[None, :] == np.arange(n_rel)[:, None]).astype(np.float32)
    c = jnp.einsum('hrc,cn->hrn', rpb, jnp.asarray(sel), precision=HI)
    c = c.reshape(n_head, 2 * WIN_ROWS - 1, GRID_W, GRID_W)
    c = jnp.where(col_in[None, None], c, NEG_BIG)
    t = jnp.stack([c[:, d:d + WIN_ROWS] for d in range(WIN_ROWS)], axis=1)
    t = t.transpose(0, 1, 3, 2, 4).reshape(n_head // 2, 2, WIN_ROWS, GRID_W, WIN_ROWS * GRID_W)
    return t.astype(F32)


def _na_kernel(rows, q_ref, kp_ref, kc_ref, kn_ref, vp_ref, vc_ref, vn_ref, kx_ref, vx_ref,
               b_ref, o_ref, kcat, vcat):
    i = pl.program_id(2)
    w = GRID_W
    blk = WIN_ROWS * w
    scale = HEAD_DIM ** -0.5
    kcat[0:blk] = kp_ref[0].astype(BF16)
    kcat[blk:2 * blk] = kc_ref[0].astype(BF16)
    kcat[2 * blk:3 * blk] = kn_ref[0].astype(BF16)
    vcat[0:blk] = vp_ref[0].astype(BF16)
    vcat[blk:2 * blk] = vc_ref[0].astype(BF16)
    vcat[2 * blk:3 * blk] = vn_ref[0].astype(BF16)
    kx = kx_ref[0].astype(BF16)
    vx = vx_ref[0].astype(BF16)
    lo_mask = lax.broadcasted_iota(jnp.int32, (1, LANES), 1) < HEAD_DIM
    rng = range(WIN_ROWS)
    r0 = [jnp.clip(i * WIN_ROWS + rr - WIN_ROWS // 2, 0, rows - WIN_ROWS) for rr in rng]
    off = [pl.multiple_of((r0[rr] - (i - 1) * WIN_ROWS) * w, w) for rr in rng]
    dlt = [r0[rr] - (i * WIN_ROWS + rr) + WIN_ROWS - 1 for rr in rng]
    qs = [_stack_heads(q_ref[0, rr * w:(rr + 1) * w, :].astype(F32), lo_mask).astype(BF16)
          for rr in rng]
    s = [_bdot_nt(qs[rr], kcat[pl.ds(off[rr], blk), :]) * scale
         + jnp.concatenate([b_ref[0, 0, dlt[rr]], b_ref[0, 1, dlt[rr]]], axis=0) for rr in rng]
    sx = [_bdot_nt(q, kx) * scale for q in qs]
    mx = [jnp.maximum(jnp.max(a, axis=-1, keepdims=True), jnp.max(b, axis=-1, keepdims=True))
          for a, b in zip(s, sx)]
    p = [jnp.exp(a - m) for a, m in zip(s, mx)]
    px = [jnp.exp(a - m) for a, m in zip(sx, mx)]
    den = [jnp.sum(a, axis=-1, keepdims=True) + jnp.sum(b, axis=-1, keepdims=True)
           for a, b in zip(p, px)]
    o = [(_bdot(p[rr], vcat[pl.ds(off[rr], blk), :]) + _bdot(px[rr], vx)) / den[rr] for rr in rng]
    for rr in rng:
        o_ref[0, rr * w:(rr + 1) * w, :] = jnp.where(lo_mask, o[rr][0:w],
                                                     o[rr][w:2 * w]).astype(BF16)


def _na_attention(qkv, qkv_ctx, bias_tab):
    bsz, t_len, w3 = qkv.shape
    na = w3 // 3
    n_pair = na // LANES
    rows = t_len // GRID_W
    blk = WIN_ROWS * GRID_W
    nblk = rows // WIN_ROWS
    c_len = qkv_ctx.shape[1]
    kv = lambda sel, shift: pl.BlockSpec(
        (1, blk, LANES), lambda b, j, i: (b, jnp.clip(i + shift, 0, nblk - 1), sel * n_pair + j))
    cx = lambda sel: pl.BlockSpec((1, c_len, LANES), lambda b, j, i: (b, 0, sel * n_pair + j))
    return pl.pallas_call(
        functools.partial(_na_kernel, rows),
        out_shape=jax.ShapeDtypeStruct((bsz, t_len, na), BF16),
        grid=(bsz, n_pair, nblk),
        in_specs=[kv(0, 0), kv(1, -1), kv(1, 0), kv(1, 1), kv(2, -1), kv(2, 0), kv(2, 1),
                  cx(1), cx(2),
                  pl.BlockSpec((1, 2, WIN_ROWS, GRID_W, blk), lambda b, j, i: (j, 0, 0, 0, 0))],
        out_specs=pl.BlockSpec((1, blk, LANES), lambda b, j, i: (b, i, j)),
        scratch_shapes=[pltpu.VMEM((3 * blk, LANES), BF16), pltpu.VMEM((3 * blk, LANES), BF16)],
        compiler_params=_cp(("parallel", "parallel", "arbitrary")),
        name="na_attention",
    )(qkv, qkv, qkv, qkv, qkv, qkv, qkv, qkv_ctx, qkv_ctx, bias_tab)


def _ctx_attn_kernel(q_ref, k_ref, v_ref, o_ref):
    lo_mask = lax.broadcasted_iota(jnp.int32, (1, LANES), 1) < HEAD_DIM
    q = q_ref[0].astype(F32)
    c = q.shape[0]
    qs = _stack_heads(q, lo_mask)
    s = _bdot_nt(qs, k_ref[0]) * (HEAD_DIM ** -0.5)
    mx = jnp.max(s, axis=-1, keepdims=True)
    p = jnp.exp(s - mx)
    o = _bdot(p, v_ref[0]) / jnp.sum(p, axis=-1, keepdims=True)
    o_ref[0] = jnp.where(lo_mask, o[0:c], o[c:2 * c]).astype(BF16)


def _ctx_attention(qkv_ctx):
    bsz, c_len, w3 = qkv_ctx.shape
    na = w3 // 3
    n_pair = na // LANES
    sp = lambda sel: pl.BlockSpec((1, c_len, LANES), lambda b, j: (b, 0, sel * n_pair + j))
    return pl.pallas_call(
        _ctx_attn_kernel,
        out_shape=jax.ShapeDtypeStruct((bsz, c_len, na), BF16),
        grid=(bsz, n_pair),
        in_specs=[sp(0), sp(1), sp(2)],
        out_specs=pl.BlockSpec((1, c_len, LANES), lambda b, j: (b, 0, j)),
        compiler_params=_cp(("parallel", "parallel")),
        name="ctx_attention",
    )(qkv_ctx, qkv_ctx, qkv_ctx)


def _layer_norm(z, g, b):
    mu = jnp.mean(z, axis=-1, keepdims=True)
    d = z - mu
    var = jnp.mean(d * d, axis=-1, keepdims=True)
    return d * lax.rsqrt(var + LN_EPS) * g + b


def _outproj_kernel(alpha, n_alias, f_ref, rwf_ref, rwb_ref, at_ref, x_ref, ga_ref, sc_ref, sh_ref,
                    wfm_ref, wo_ref, g_ref, b_ref, wr_ref, br_ref, *rest):
    xo_ref, h_ref, lg_ref = rest[n_alias:]
    nf = f_ref.shape[2]
    nr = rwf_ref.shape[2]
    fm = jnp.dot(f_ref[0].astype(BF16), wfm_ref[...], preferred_element_type=F32)
    rwo = (rwf_ref[0] + rwb_ref[0]).astype(BF16)
    mix = (jnp.dot(fm.astype(BF16), wo_ref[0:nf, :], preferred_element_type=F32)
           + jnp.dot(rwo, wo_ref[nf:nf + nr, :], preferred_element_type=F32)
           + jnp.dot(at_ref[0].astype(BF16), wo_ref[nf + nr:, :], preferred_element_type=F32))
    xn = _layer_norm(alpha * x_ref[0] + ga_ref[0] * mix, g_ref[...], b_ref[...])
    xo_ref[0] = xn
    h = xn * (1.0 + sc_ref[0]) + sh_ref[0]
    h_ref[...] = h
    lg_ref[...] = _mm3(h, wr_ref[...]) + br_ref[...]


def _outproj(alpha, four, rwo, rwb, att, x, ga, sc, sh, wfm, wo, ln_g, ln_b, wr_pad, br_pad, tm,
             n_all, row0, bufs):
    g, m, d = x.shape
    nb = m // tm
    base = row0 // tm
    blk = lambda a: pl.BlockSpec((1, tm, a.shape[2]), lambda b, i: (b, i, 0))
    mod = pl.BlockSpec((1, 1, d), lambda b, i: (b, 0, 0))
    cst = lambda a: pl.BlockSpec(a.shape, lambda b, i: (0, 0))
    row = lambda a: a.reshape(1, -1)
    ln_g, ln_b, br_pad = row(ln_g), row(ln_b), row(br_pad)
    args = [four, rwo, rwb, att, x, ga, sc, sh, wfm, wo, ln_g, ln_b, wr_pad, br_pad]
    in_specs = [blk(four), blk(rwo), blk(rwb), blk(att), blk(x), mod, mod, mod, cst(wfm), cst(wo),
                cst(ln_g), cst(ln_b), cst(wr_pad), cst(br_pad)]
    aliases = {}
    if bufs is not None:
        aliases = {len(args): 1, len(args) + 1: 2}
        args += list(bufs)
        in_specs += [pl.BlockSpec(memory_space=pl.ANY)] * 2
    return pl.pallas_call(
        functools.partial(_outproj_kernel, alpha, len(aliases)),
        out_shape=[jax.ShapeDtypeStruct((g, m, d), F32), jax.ShapeDtypeStruct((n_all, d), F32),
                   jax.ShapeDtypeStruct((n_all, LANES), F32)],
        grid=(g, nb),
        in_specs=in_specs,
        out_specs=[pl.BlockSpec((1, tm, d), lambda b, i: (b, i, 0)),
                   pl.BlockSpec((tm, d), lambda b, i: (base + b * nb + i, 0)),
                   pl.BlockSpec((tm, LANES), lambda b, i: (base + b * nb + i, 0))],
        input_output_aliases=aliases,
        compiler_params=_cp(("parallel", "parallel")),
        name="outproj_ln",
    )(*args)


def _ln2_kernel(alpha, x_ref, y0_ref, y1_ref, y2_ref, y3_ref, gt_ref, ga_ref, g_ref, b_ref,
                o_ref):
    gt = gt_ref[...]
    y = (gt[:, 4:5] * y0_ref[0] + gt[:, 5:6] * y1_ref[0]
         + gt[:, 6:7] * y2_ref[0] + gt[:, 7:8] * y3_ref[0])
    o_ref[0] = _layer_norm(alpha * x_ref[0] + ga_ref[0] * y, g_ref[...], b_ref[...])


def _ln2(alpha, x, yk, gates, row0, ga, ln_g, ln_b, tm):
    g, m, d = x.shape
    nb = m // tm
    base = row0 // tm
    blk = pl.BlockSpec((1, tm, d), lambda b, i: (b, i, 0))
    vec = pl.BlockSpec((1, d), lambda b, i: (0, 0))
    ysp = lambda k: pl.BlockSpec((1, tm, d), lambda b, i: (k, base + b * nb + i, 0))
    return pl.pallas_call(
        functools.partial(_ln2_kernel, alpha),
        out_shape=jax.ShapeDtypeStruct((g, m, d), F32),
        grid=(g, nb),
        in_specs=[blk, ysp(0), ysp(1), ysp(2), ysp(3),
                  pl.BlockSpec((tm, LANES), lambda b, i: (base + b * nb + i, 0)),
                  pl.BlockSpec((1, 1, d), lambda b, i: (b, 0, 0)), vec, vec],
        out_specs=blk,
        compiler_params=_cp(("parallel", "parallel")),
        name="ln2",
    )(x, yk, yk, yk, yk, gates, ga, ln_g.reshape(1, -1), ln_b.reshape(1, -1))


def _moe_kernel(be_ref, nu_ref, x_ref, wgu_ref, bgu_ref, wdn_ref, bdn_ref, o_ref, wgu_s, wdn_s):
    i = pl.program_id(0)

    @pl.when((i == 0) | (be_ref[i] != be_ref[jnp.maximum(i - 1, 0)]))
    def _():
        wgu_s[...] = wgu_ref[0, 0].astype(BF16)
        wdn_s[...] = wdn_ref[0, 0].astype(BF16)

    @pl.when(i < nu_ref[0])
    def _():
        de = wdn_s.shape[0]
        gu = (jnp.dot(x_ref[...].astype(BF16), wgu_s[...], preferred_element_type=F32)
              + bgu_ref[0, 0])
        gg = jnp.minimum(gu[:, 0:de], SWIGLU_LIMIT)
        uu = jnp.clip(gu[:, de:2 * de], -SWIGLU_LIMIT, SWIGLU_LIMIT)
        act = (uu + 1.0) * gg * _sigmoid(SWIGLU_ALPHA * gg)
        o_ref[...] = (jnp.dot(act.astype(BF16), wdn_s[...], preferred_element_type=F32)
                      + bdn_ref[0, 0])


def _moe_experts(xs, block_e, n_used, layer, w_gu, b_gu, w_dn, b_dn, tm):
    cap, d = xs.shape
    n_layer, n_exp, _, de2 = w_gu.shape
    de = de2 // 2
    n_blocks = cap // tm
    wmap = lambda i, be, nu: (layer, be[i], 0, 0)
    grid_spec = pltpu.PrefetchScalarGridSpec(
        num_scalar_prefetch=2,
        grid=(n_blocks,),
        in_specs=[pl.BlockSpec((tm, d), lambda i, be, nu: (i, 0)),
                  pl.BlockSpec((1, 1, d, de2), wmap),
                  pl.BlockSpec((1, 1, 1, de2), wmap),
                  pl.BlockSpec((1, 1, de, d), wmap),
                  pl.BlockSpec((1, 1, 1, d), wmap)],
        out_specs=pl.BlockSpec((tm, d), lambda i, be, nu: (i, 0)),
        scratch_shapes=[pltpu.VMEM((d, de2), BF16), pltpu.VMEM((de, d), BF16)],
    )
    return pl.pallas_call(
        _moe_kernel,
        out_shape=jax.ShapeDtypeStruct((cap, d), F32),
        grid_spec=grid_spec,
        compiler_params=_cp(("arbitrary",)),
        name="moe_experts",
    )(block_e, n_used, xs, w_gu, b_gu.reshape(n_layer, n_exp, 1, de2), w_dn,
      b_dn.reshape(n_layer, n_exp, 1, d))


def _router_kernel(lg_ref, tri_ref, r_ref, cnt_ref, carry):
    i = pl.program_id(0)

    @pl.when(i == 0)
    def _():
        carry[...] = jnp.zeros_like(carry)

    lane = lax.broadcasted_iota(jnp.int32, lg_ref.shape, 1)
    lanef = lane.astype(F32)
    lg = jnp.where(lane < N_EXPERTS, lg_ref[...], NEG_BIG)
    tops, idxs, hots = [], [], []
    for _ in range(TOP_K):
        m = jnp.max(lg, axis=-1, keepdims=True)
        idx = jnp.min(jnp.where(lg == m, lanef, float(LANES)), axis=-1, keepdims=True)
        hot = lanef == idx
        lg = jnp.where(hot, NEG_BIG, lg)
        tops.append(m)
        idxs.append(idx)
        hots.append(hot)
    ex = [jnp.exp(m - tops[0]) for m in tops]
    den = ex[0] + ex[1] + ex[2] + ex[3]
    oh = jnp.where(hots[0] | hots[1] | hots[2] | hots[3], 1.0, 0.0)
    before = carry[...] + jnp.dot(tri_ref[...], oh.astype(BF16), preferred_element_type=F32)
    rec = jnp.zeros(lg_ref.shape, F32)
    for k in range(TOP_K):
        rank = jnp.sum(jnp.where(hots[k], before, 0.0), axis=-1, keepdims=True)
        rec = jnp.where(lane == k, idxs[k], rec)
        rec = jnp.where(lane == TOP_K + k, ex[k] / den, rec)
        rec = jnp.where(lane == 2 * TOP_K + k, rank, rec)
    r_ref[...] = rec
    carry[...] = carry[...] + jnp.sum(oh, axis=0, keepdims=True)
    cnt_ref[...] = carry[...]


def _route(logits):
    n_tok = logits.shape[0]
    tb = min(ROUTE_TB, n_tok)
    assert n_tok % tb == 0
    tri =jnp.asarray(np.tril(np.ones((tb, tb), np.float32), -1), BF16)
    return pl.pallas_call(
        _router_kernel,
        out_shape=[jax.ShapeDtypeStruct((n_tok, LANES), F32), jax.ShapeDtypeStruct((1, LANES), F32)],
        grid=(n_tok // tb,),
        in_specs=[pl.BlockSpec((tb, LANES), lambda i: (i, 0)), pl.BlockSpec((tb, tb), lambda i: (0, 0))],
        out_specs=[pl.BlockSpec((tb, LANES), lambda i: (i, 0)), pl.BlockSpec((1, LANES), lambda i: (0, 0))],
        scratch_shapes=[pltpu.VMEM((1, LANES), F32)],
        compiler_params=_cp(("arbitrary",)),
        name="router",
    )(logits, tri)


def _moe(h, logits, layer, wgu, bgu, wdn, bdn, tm):
    n_tok, d = h.shape
    n_slot = n_tok * TOP_K
    rec, cnt = _route(logits)
    top_e = rec[:, 0:TOP_K].astype(jnp.int32)
    rank = rec[:, 2 * TOP_K:3 * TOP_K].astype(jnp.int32)
    counts = cnt[0, 0:N_EXPERTS].astype(jnp.int32)
    first = jnp.cumsum(counts) - counts
    padded = (counts + tm - 1) // tm * tm
    pad_end = jnp.cumsum(padded)
    pad_start = pad_end - padded
    n_blocks = -(-n_slot // tm) + N_EXPERTS
    cap = n_blocks * tm
    blk_row = jnp.arange(n_blocks, dtype=jnp.int32) * tm
    block_e = jnp.minimum(jnp.sum(pad_end[None, :] <= blk_row[:, None], axis=1),
                          N_EXPERTS - 1).astype(jnp.int32)
    n_used = (pad_end[-1:] // tm).astype(jnp.int32)
    experts = jnp.arange(N_EXPERTS, dtype=jnp.int32)
    pos = jnp.sum(jnp.where(top_e[..., None] == experts, pad_start, 0), axis=-1) + rank
    order = jnp.argsort(pos.reshape(-1)).astype(jnp.int32)
    row_e = jnp.repeat(block_e, tm)
    off = jnp.arange(cap, dtype=jnp.int32) - pad_start[row_e]
    src = jnp.clip(first[row_e] + off, 0, n_slot - 1)
    slot_tok = jnp.where(off < counts[row_e], order.at[src].get(mode='promise_in_bounds') // TOP_K, 0)
    xs = h.at[slot_tok].get(mode='promise_in_bounds')
    ys = _moe_experts(xs, block_e, n_used, layer, wgu, bgu, wdn, bdn, tm)
    yk = ys.at[pos.T.reshape(-1)].get(mode='promise_in_bounds').reshape(TOP_K, n_tok, d)
    return yk, rec


def _pad_rows(w, start, total):
    return jnp.zeros((total, w.shape[1]), w.dtype).at[start:start + w.shape[0]].set(w)


def kernel(x, c, ctx, c_ctx, w_ada, b_ada, w_in, w_vdown, w_fmap, ts_prev, ts_next, rw_w0, rw_w2, rw_a0, rw_a2, rw_g2, rw_kk, rw_ka, rw_rk, rw_gn_g, rw_gn_b, rw_v0, rw_v2, na_rpb, w_out, ln1_g, ln1_b, w_router, b_router, w_gu, b_gu, w_dn, b_dn, ln2_g, ln2_b):
    bsz, n_lat, d = x.shape
    c_len = ctx.shape[1]
    depth = w_in.shape[0]
    f_dim = w_fmap.shape[1]
    rw_dim = rw_kk.shape[1]
    alpha = (2 * depth) ** 0.25
    n_pair = rw_dim // LANES

    tm_lat = min(512, n_lat)
    tm_ctx = min(256, c_len)
    tb_ctx = min(128, c_len)
    tm_moe = 512
    n1_lat, n2_lat = _split_len(n_lat)
    n1_ctx, n2_ctx = _split_len(c_len)

    cond = jnp.zeros((SUBLANES, d), F32).at[0:bsz].set(c).at[bsz].set(c_ctx)
    ada = _ada_all(cond, w_ada, b_ada)

    xl, xc = x, ctx
    vf_l = vf_c = None
    o_r = f_dim
    o_lora = f_dim + 3 * rw_dim
    o_q = o_lora + 2 * DECAY_LORA + 2 * ICLR_LORA + 2 * GATE_LORA
    for l in range(depth):
        last = l == depth - 1
        mods = ada[l].reshape(SUBLANES, 6, d)
        lat = lambda k: mods[0:bsz, k][:, None, :]
        cx = lambda k: jnp.broadcast_to(mods[bsz, k][None, None, :], (bsz, 1, d))

        wi = w_in[l]
        vdown = jnp.zeros((d, LANES), F32)
        if l > 0:
            vdown = vdown.at[:, 0:VRES_LORA].set(w_vdown[l - 1])
        wf = wi[:, 0:f_dim].astype(BF16)
        wr = wi[:, o_r:o_lora].astype(BF16)
        wl = jnp.concatenate([wi[:, o_lora:o_q], vdown], axis=1).astype(BF16)
        wq = wi[:, o_q:].astype(BF16)
        f_l, rkv_l, lora_l, qkv_l = _inproj(xl, lat(1), lat(0), wf, wr, wl, wq, tm_lat)
        f_c, rkv_c, lora_c, qkv_c = _inproj(xc, cx(1), cx(0), wf, wr, wl, wq, tm_ctx)

        if l == 0:
            v0 = v2pad = None
        else:
            v0 = rw_v0[l - 1]
            v2pad = _pad_rows(rw_v2[l - 1], 0, LANES)
        prep_c, vf_c = _rwkv_prep(rkv_c, lora_c, ts_prev[l], ts_next[l], rw_kk[l], vf_c, v0, v2pad,
                                  tb_ctx)
        prep_l, vf_l = _rwkv_prep(rkv_l, lora_l, ts_prev[l], ts_next[l], rw_kk[l], vf_l, v0, v2pad,
                                  tm_lat)
        prm = (rw_w0[l],
               jnp.stack([_pad_rows(rw_w2[l, dr], dr * DECAY_LORA, LANES) for dr in range(2)]),
               rw_a0[l],
               jnp.stack([_pad_rows(rw_a2[l, dr], dr * ICLR_LORA, LANES) for dr in range(2)]),
               rw_g2[l], rw_ka[l], rw_rk[l].reshape(-1), rw_gn_g[l], rw_gn_b[l])
        s0 = jnp.zeros((2 * bsz * n_pair, LANES, LANES), F32)
        rwf_c, rwb_c, s_ctx = _rwkv_bidir(prep_c, lora_c, s0, prm)
        rwf_l, rwb_l, _ = _rwkv_bidir(prep_l, lora_l, s_ctx, prm)

        att_l = _na_attention(qkv_l, qkv_c, _na_bias_table(na_rpb[l]))
        four_l = _fourier(f_l, n1_lat, n2_lat)

        wfm = w_fmap[l].astype(BF16)
        wo = w_out[l].astype(BF16)
        wr_pad = jnp.zeros((d, LANES), F32).at[:, 0:N_EXPERTS].set(w_router[l])
        br_pad = jnp.zeros((LANES,), F32).at[0:N_EXPERTS].set(b_router[l])
        n_l = bsz * n_lat
        n_all = n_l if last else n_l + bsz * c_len
        xl, h_all, lg_all = _outproj(alpha, four_l, rwf_l, rwb_l, att_l, xl, lat(2), lat(4), lat(3),
                                     wfm, wo, ln1_g[l], ln1_b[l], wr_pad, br_pad, tm_lat, n_all, 0,
                                     None)
        if last:
            yk, gates = _moe(h_all, lg_all, l, w_gu, b_gu, w_dn, b_dn, tm_moe)
            xl = _ln2(alpha, xl, yk, gates, 0, lat(5), ln2_g[l], ln2_b[l], tm_lat)
        else:
            att_c = _ctx_attention(qkv_c)
            four_c = _fourier(f_c, n1_ctx, n2_ctx)
            xc, h_all, lg_all = _outproj(alpha, four_c, rwf_c, rwb_c, att_c, xc, cx(2), cx(4), cx(3),
                                         wfm, wo, ln1_g[l], ln1_b[l], wr_pad, br_pad, tm_ctx, n_all,
                                         n_l, (h_all, lg_all))
            yk, gates = _moe(h_all, lg_all, l, w_gu, b_gu, w_dn, b_dn, tm_moe)
            xl = _ln2(alpha, xl, yk, gates, 0, lat(5), ln2_g[l], ln2_b[l], tm_lat)
            xc = _ln2(alpha, xc, yk, gates, n_l, cx(5), ln2_g[l], ln2_b[l], tm_ctx)
    return xl
```

```python
import functools

import numpy as np
import jax
import jax.numpy as jnp
from jax import lax
from jax.experimental import pallas as pl
from jax.experimental.pallas import tpu as pltpu

F32 = jnp.float32
BF16 = jnp.bfloat16
HI = lax.Precision.HIGHEST

GRID_W = 64
HEAD_DIM = 64
WIN_ROWS = 8
WIN_COLS = 16
N_EXPERTS = 32
TOP_K = 4
SWIGLU_LIMIT = 7.0
SWIGLU_ALPHA = 1.702
LN_EPS = 1e-5
GN_EPS = HEAD_DIM * 1e-5
DECAY_LORA = 64
ICLR_LORA = 64
GATE_LORA = 128
VRES_LORA = 32

LANES = 128
SUBLANES = 8
VMEM_LIMIT = 56 * 1024 * 1024

CHUNK = 64
NEG_BIG = -1e30
ROUTE_TB = 512


def _cp(sem, vmem=VMEM_LIMIT):
    return pltpu.CompilerParams(dimension_semantics=sem, vmem_limit_bytes=vmem)


def _dot(a, b, prec=HI):
    return jnp.dot(a, b, precision=prec, preferred_element_type=F32)


def _bdot(a, b):
    return jnp.dot(a.astype(BF16), b.astype(BF16), preferred_element_type=F32)


def _bdot_nt(a, b):
    return lax.dot_general(a.astype(BF16), b.astype(BF16), (((1,), (1,)), ((), ())),
                           preferred_element_type=F32)


def _sigmoid(x):
    return 1.0 / (1.0 + jnp.exp(-x))


def _split2(x):
    x1 = x.astype(BF16)
    x2 = (x - x1.astype(F32)).astype(BF16)
    return x1, x2


def _lhs3(x):
    x1, x2 = _split2(x)
    return jnp.concatenate([x1, x2, x1], axis=1)


def _rhs3(y):
    y1, y2 = _split2(y)
    return jnp.concatenate([y1, y1, y2], axis=0)


def _mm3(a, b):
    return jnp.dot(_lhs3(a), _rhs3(b), preferred_element_type=F32)


def _head_sum_all(xs, ones3):
    m, width = xs[0].shape
    n_col = width // LANES
    tall = jnp.concatenate([x[:, j * LANES:(j + 1) * LANES] for x in xs for j in range(n_col)],
                           axis=0)
    x1 = tall.astype(BF16)
    r1 = tall - x1.astype(F32)
    x2 = r1.astype(BF16)
    x3 = (r1 - x2.astype(F32)).astype(BF16)
    out = jnp.dot(jnp.concatenate([x1, x2, x3], axis=1), ones3, preferred_element_type=F32)
    return [jnp.concatenate([out[(i * n_col + j) * m:(i * n_col + j + 1) * m]
                             for j in range(n_col)], axis=1) for i in range(len(xs))]


def _head_sum(x, ones3):
    return _head_sum_all([x], ones3)[0]


def _ada_kernel(c_ref, w_ref, b_ref, o_ref):
    c = c_ref[...]
    o_ref[0] = _dot(c * _sigmoid(c), w_ref[0]) + b_ref[0]


def _ada_all(cond, w_ada, b_ada):
    n_layer, d, n = w_ada.shape
    tn = 1536
    return pl.pallas_call(
        _ada_kernel,
        out_shape=jax.ShapeDtypeStruct((n_layer, SUBLANES, n), F32),
        grid=(n_layer, n // tn),
        in_specs=[pl.BlockSpec((SUBLANES, d), lambda l, j: (0, 0)),
                  pl.BlockSpec((1, d, tn), lambda l, j: (l, 0, j)),
                  pl.BlockSpec((1, 1, tn), lambda l, j: (l, 0, j))],
        out_specs=pl.BlockSpec((1, SUBLANES, tn), lambda l, j: (l, 0, j)),
        compiler_params=_cp(("arbitrary", "arbitrary")),
        name="ada",
    )(cond, w_ada, b_ada.reshape(n_layer, 1, n))


def _inproj_kernel(x_ref, sc_ref, sh_ref, wf_ref, wr_ref, wl_ref, wq_ref,
                   of_ref, or_ref, ol_ref, oq_ref):
    xm = (x_ref[0] * (1.0 + sc_ref[0]) + sh_ref[0]).astype(BF16)
    of_ref[0] = jnp.dot(xm, wf_ref[...], preferred_element_type=F32)
    or_ref[0] = jnp.dot(xm, wr_ref[...], preferred_element_type=F32)
    ol_ref[0] = jnp.dot(xm, wl_ref[...], preferred_element_type=F32)
    oq_ref[0] = jnp.dot(xm, wq_ref[...], preferred_element_type=F32).astype(BF16)


def _inproj(x, sc, sh, wf, wr, wl, wq, tm):
    g, m, d = x.shape
    ws = (wf, wr, wl, wq)
    mod = pl.BlockSpec((1, 1, d), lambda b, i: (b, 0, 0))
    return pl.pallas_call(
        _inproj_kernel,
        out_shape=[jax.ShapeDtypeStruct((g, m, w.shape[1]), dt)
                   for w, dt in zip(ws, (F32, F32, F32, BF16))],
        grid=(g, m // tm),
        in_specs=[pl.BlockSpec((1, tm, d), lambda b, i: (b, i, 0)), mod, mod]
                 + [pl.BlockSpec(w.shape, lambda b, i: (0, 0)) for w in ws],
        out_specs=[pl.BlockSpec((1, tm, w.shape[1]), lambda b, i: (b, i, 0)) for w in ws],
        compiler_params=_cp(("parallel", "parallel")),
        name="inproj",
    )(x, sc, sh, *ws)


def _dft_consts(t_len, n1, n2, tb2, width):
    groups = width // HEAD_DIM
    j = np.arange(HEAD_DIM)
    ang = 2.0 * np.pi * np.outer(j, j) / HEAD_DIM
    eye = np.eye(groups)
    cbd = np.kron(eye, np.cos(ang)) / np.sqrt(HEAD_DIM)
    sbd = np.kron(eye, np.sin(ang)) / np.sqrt(HEAD_DIM)
    a1 = 2.0 * np.pi * np.outer(np.arange(n1), np.arange(n1)) / n1
    c1, s1 = np.cos(a1), np.sin(a1)
    atw = 2.0 * np.pi * np.outer(np.arange(n1), np.arange(n2)) / t_len
    twc = np.cos(atw).reshape(n1, n2 // tb2, tb2).transpose(1, 0, 2)
    tws = np.sin(atw).reshape(n1, n2 // tb2, tb2).transpose(1, 0, 2)
    a2 = 2.0 * np.pi * np.outer(np.arange(n2), np.arange(n2)) / n2
    c2, s2 = np.cos(a2) / np.sqrt(t_len), np.sin(a2) / np.sqrt(t_len)
    return [jnp.asarray(v, F32) for v in (cbd, sbd, c1, s1, twc, tws, c2, s2)]


def _four_a_kernel(tb2, width, x_ref, cbd_ref, sbd_ref, c1_ref, s1_ref, twc_ref, tws_ref,
                   yr_ref, yi_ref):
    dot = functools.partial(jnp.dot, preferred_element_type=F32)
    c1 = _lhs3(c1_ref[...])
    s1 = _lhs3(s1_ref[...])
    cbd = _rhs3(cbd_ref[...])
    sbd = _rhs3(sbd_ref[...])
    for jj in range(tb2):
        x = _lhs3(x_ref[0, :, width * jj:width * (jj + 1)])
        zr = _rhs3(dot(x, cbd))
        zi = _rhs3(-dot(x, sbd))
        ar = dot(c1, zr) + dot(s1, zi)
        ai = dot(c1, zi) - dot(s1, zr)
        tc = twc_ref[0, :, jj:jj + 1]
        ts = tws_ref[0, :, jj:jj + 1]
        yr_ref[0, jj] = tc * ar + ts * ai
        yi_ref[0, jj] = tc * ai - ts * ar


def _four_b_kernel(yr_ref, yi_ref, c2_ref, s2_ref, o_ref):
    o_ref[0] = (_mm3(c2_ref[...], yr_ref[0]) + _mm3(s2_ref[...], yi_ref[0])).astype(BF16)


def _fourier(f, n1, n2):
    bsz, t_len, width = f.shape
    tb2 = SUBLANES
    cbd, sbd, c1, s1, twc, tws, c2, s2 = _dft_consts(t_len, n1, n2, tb2, width)
    const = lambda a: pl.BlockSpec(a.shape, lambda b, j: (0,) * a.ndim)
    yr, yi = pl.pallas_call(
        functools.partial(_four_a_kernel, tb2, width),
        out_shape=[jax.ShapeDtypeStruct((bsz, n2, n1, width), F32)] * 2,
        grid=(bsz, n2 // tb2),
        in_specs=[pl.BlockSpec((1, n1, tb2 * width), lambda b, j: (b, 0, j)),
                  const(cbd), const(sbd), const(c1), const(s1),
                  pl.BlockSpec((1, n1, tb2), lambda b, j: (j, 0, 0)),
                  pl.BlockSpec((1, n1, tb2), lambda b, j: (j, 0, 0))],
        out_specs=[pl.BlockSpec((1, tb2, n1, width), lambda b, j: (b, j, 0, 0))] * 2,
        compiler_params=_cp(("parallel", "parallel")),
        name="fourier_a",
    )(f.reshape(bsz, n1, n2 * width), cbd, sbd, c1, s1, twc, tws)
    ncol = n1 * width
    tc = min(ncol, 2048)
    out = pl.pallas_call(
        _four_b_kernel,
        out_shape=jax.ShapeDtypeStruct((bsz, n2, ncol), BF16),
        grid=(bsz, ncol // tc),
        in_specs=[pl.BlockSpec((1, n2, tc), lambda b, j: (b, 0, j)),
                  pl.BlockSpec((1, n2, tc), lambda b, j: (b, 0, j)),
                  const(c2), const(s2)],
        out_specs=pl.BlockSpec((1, n2, tc), lambda b, j: (b, 0, j)),
        compiler_params=_cp(("parallel", "parallel")),
        name="fourier_b",
    )(yr.reshape(bsz, n2, ncol), yi.reshape(bsz, n2, ncol), c2, s2)
    return out.reshape(bsz, t_len, width)


def _split_len(t_len):
    n1 = 1 << ((t_len.bit_length() - 1 + 1) // 2)
    return n1, t_len // n1


def _head_ones3():
    h = np.arange(LANES) // HEAD_DIM
    one = (h[:, None] == h[None, :]).astype(np.float32)
    return jnp.asarray(np.concatenate([one, one, one], axis=0), BF16)


def _rwkv_prep_kernel(first, x_ref, xp_ref, xn_ref, lo_ref, tsp_ref, tsn_ref, kk_ref,
                      ones_ref, *rest):
    if first:
        o_ref, vf_out_ref = rest
    else:
        v0_ref, v2_ref, vf_ref, o_ref = rest
    i = pl.program_id(1)
    last = pl.num_programs(1) - 1
    x = x_ref[0]
    tb, w3 = x.shape
    rw = w3 // 3
    row = lax.broadcasted_iota(jnp.int32, (tb, 1), 0)
    prev_edge = jnp.where(i == 0, 0.0, xp_ref[0, SUBLANES - 1:SUBLANES, :])
    next_edge = jnp.where(i == last, 0.0, xn_ref[0, 0:1, :])
    prev = jnp.where(row == 0, prev_edge, pltpu.roll(x, 1, 0))
    nxt = jnp.where(row == tb - 1, next_edge, pltpu.roll(x, tb - 1, 0))
    s = x + tsp_ref[...] * (prev - x) + tsn_ref[...] * (nxt - x)
    r = s[:, 0:rw]
    k = s[:, rw:2 * rw]
    v = s[:, 2 * rw:3 * rw]
    if first:
        vf_out_ref[0] = v
    else:
        mix = _sigmoid(v0_ref[...] + _mm3(lo_ref[0], v2_ref[...]))
        v = v + (vf_ref[0] - v) * mix
    kk = k * kk_ref[...]
    ss = _head_sum(kk * kk, ones_ref[...])
    kk = kk * lax.rsqrt(jnp.maximum(ss, 1e-24))
    o_ref[0, :, 0:rw] = r
    o_ref[0, :, rw:2 * rw] = k
    o_ref[0, :, 2 * rw:3 * rw] = v
    o_ref[0, :, 3 * rw:4 * rw] = kk


def _rwkv_prep(rkv, lora, ts_prev, ts_next, k_k, v_first, v0, v2pad, tb):
    g, m, w3 = rkv.shape
    rw = w3 // 3
    first = v_first is None
    nb8 = tb // SUBLANES
    n8 = m // SUBLANES
    ones3 = _head_ones3()
    row = lambda a: a.reshape(1, -1)
    vec = lambda n: pl.BlockSpec((1, n), lambda b, i: (0, 0))
    in_specs = [
        pl.BlockSpec((1, tb, w3), lambda b, i: (b, i, 0)),
        pl.BlockSpec((1, SUBLANES, w3), lambda b, i: (b, jnp.maximum(i * nb8 - 1, 0), 0)),
        pl.BlockSpec((1, SUBLANES, w3), lambda b, i: (b, jnp.minimum((i + 1) * nb8, n8 - 1), 0)),
        pl.BlockSpec((1, tb, LANES), lambda b, i: (b, i, 4)),
        vec(w3), vec(w3), vec(rw),
        pl.BlockSpec(ones3.shape, lambda b, i: (0, 0)),
    ]
    args = [rkv, rkv, rkv, lora, row(ts_prev), row(ts_next), row(k_k), ones3]
    out_main = jax.ShapeDtypeStruct((g, m, 4 * rw), F32)
    spec_main = pl.BlockSpec((1, tb, 4 * rw), lambda b, i: (b, i, 0))
    spec_v = pl.BlockSpec((1, tb, rw), lambda b, i: (b, i, 0))
    if first:
        out_shape = [out_main, jax.ShapeDtypeStruct((g, m, rw), F32)]
        out_specs = [spec_main, spec_v]
    else:
        in_specs += [vec(rw), pl.BlockSpec((LANES, rw), lambda b, i: (0, 0)), spec_v]
        args += [row(v0), v2pad, v_first]
        out_shape = out_main
        out_specs = spec_main
    res = pl.pallas_call(
        functools.partial(_rwkv_prep_kernel, first),
        out_shape=out_shape, grid=(g, m // tb), in_specs=in_specs, out_specs=out_specs,
        compiler_params=_cp(("parallel", "parallel")),
        name="rwkv_prep",
    )(*args)
    if first:
        return res[0], res[1]
    return res, v_first


def _stack_heads(x, lo_mask):
    return jnp.concatenate([jnp.where(lo_mask, x, 0.0), jnp.where(lo_mask, 0.0, x)], axis=0)


def _scan_masks(reverse):
    L = CHUNK
    L2, L4 = 2 * L, 4 * L
    ti = lax.broadcasted_iota(jnp.int32, (L4, L4), 0)
    si = lax.broadcasted_iota(jnp.int32, (L4, L4), 1)
    tq = ti % L2
    sq = si % L2
    tt = tq % L
    st = sq % L
    before = (st > tt) if reverse else (st < tt)
    keep = ((tq // L) == (sq // L)) & (before | ((ti >= L2) & (st == tt)))
    ci = lax.broadcasted_iota(jnp.int32, (L, L), 0)
    cj = lax.broadcasted_iota(jnp.int32, (L, L), 1)
    tri = jnp.where((cj >= ci) if reverse else (cj <= ci), 1.0, 0.0).astype(BF16)
    return keep, jnp.concatenate([tri, tri, tri], axis=1)


def _rwkv_bidir_kernel(pf_ref, pb_ref, lof_ref, lob_ref, s0_ref, w0_ref, w2_ref, a0_ref, a2_ref,
                       g2_ref, ka_ref, rk_ref, gng_ref, gnb_ref, ones_ref,
                       of_ref, ob_ref, so_ref, s_scr):
    i = pl.program_id(0)
    L = CHUNK
    L2, L4 = 2 * L, 4 * L
    bsz = pf_ref.shape[0]
    rw = pf_ref.shape[2] // 4
    n_pair = rw // LANES

    @pl.when(i == 0)
    def _():
        s_scr[...] = s0_ref[...]

    ones3 = ones_ref[...]
    lo_mask = lax.broadcasted_iota(jnp.int32, (1, LANES), 1) < HEAD_DIM
    ei = lax.broadcasted_iota(jnp.int32, (L2, L2), 0)
    ej = lax.broadcasted_iota(jnp.int32, (L2, L2), 1)
    eye = jnp.where(ei == ej, 1.0, 0.0)
    masks = (_scan_masks(False), _scan_masks(True))

    streams = []
    for dr, (p_ref, lo_ref) in enumerate(((pf_ref, lof_ref), (pb_ref, lob_ref))):
        lo = jnp.concatenate([lo_ref[b] for b in range(bsz)], axis=0)
        zw = w0_ref[dr] + _bdot(jnp.tanh(lo[:, 0:LANES]), w2_ref[dr])
        log_w = -(jnp.maximum(-zw, 0.0) + jnp.log(1.0 + jnp.exp(-jnp.abs(zw)))) - 0.5
        lw_all = -jnp.exp(log_w)
        iclr_all = _sigmoid(a0_ref[dr] + _bdot(lo[:, LANES:2 * LANES], a2_ref[dr]))
        gate_all = _bdot(_sigmoid(lo[:, (2 + dr) * LANES:(3 + dr) * LANES]), g2_ref[dr])
        for b in range(bsz):
            rows = slice(b * L, (b + 1) * L)
            p = p_ref[b]
            r = p[:, 0:rw]
            k = p[:, rw:2 * rw]
            v = p[:, 2 * rw:3 * rw]
            kk = p[:, 3 * rw:4 * rw]
            iclr = iclr_all[rows]
            kd = k * (1.0 + (iclr - 1.0) * ka_ref[...])
            streams.append(dict(dr=dr, b=b, r=r, kd=kd, v=v, av=-kk, bv=kk * iclr,
                                lw=lw_all[rows], gate=gate_all[rows]))
    rk_sums = _head_sum_all([st['r'] * st['kd'] * rk_ref[...] for st in streams], ones3)
    for st, rks in zip(streams, rk_sums):
        st['bonus'] = rks * st['v']

    chains = [(si_, j) for si_ in range(len(streams)) for j in range(n_pair)]

    def cols(si_, j, name):
        return streams[si_][name][:, j * LANES:(j + 1) * LANES]

    ops = []
    for si_, j in chains:
        dr = streams[si_]['dr']
        keep, tri3 = masks[dr]
        lwc = cols(si_, j, 'lw')
        l1 = lwc.astype(BF16)
        lr = lwc - l1.astype(F32)
        l2 = lr.astype(BF16)
        l3 = (lr - l2.astype(F32)).astype(BF16)
        cum = jnp.dot(tri3, jnp.concatenate([l1, l2, l3], axis=0), preferred_element_type=F32)
        end_row = 0 if dr else L - 1
        tot = cum[end_row:end_row + 1, :]
        g_inv = jnp.exp(-cum)
        g_end = jnp.exp(tot - cum)
        bvc = cols(si_, j, 'bv')
        kdc = cols(si_, j, 'kd')
        stk = lambda x: _stack_heads(x, lo_mask).astype(BF16)
        ops.append(dict(
            keep=keep, tot=tot,
            ar_s=jnp.concatenate([stk(cols(si_, j, 'av') * jnp.exp(cum - lwc)),
                                  stk(cols(si_, j, 'r') * jnp.exp(cum))], axis=0),
            bk_s=jnp.concatenate([stk(bvc * g_inv), stk(kdc * g_inv)], axis=0),
            v_s=stk(cols(si_, j, 'v')),
            vt_s=_stack_heads(cols(si_, j, 'v'), lo_mask).T.astype(BF16),
            bkh_s=jnp.concatenate([stk(bvc * g_end), stk(kdc * g_end)], axis=0)))

    dot = functools.partial(jnp.dot, preferred_element_type=F32)
    grams = [jnp.where(o['keep'], _bdot_nt(o['ar_s'], o['bk_s']), 0.0) for o in ops]
    tinv = [eye + g[0:L2, 0:L2] for g in grams]
    pw = [g[0:L2, 0:L2].astype(BF16) for g in grams]
    pw = [dot(a, a).astype(BF16) for a in pw]
    for _ in range(4):
        both = [dot(jnp.concatenate([p, t.astype(BF16)], axis=0), p) for p, t in zip(pw, tinv)]
        pw = [b[0:L2].astype(BF16) for b in both]
        tinv = [t + b[L2:L4] for t, b in zip(tinv, both)]
    tinv = [t + dot(t.astype(BF16), p) for t, p in zip(tinv, pw)]
    x1 = [dot(g[0:L2, L2:L4].astype(BF16), o['v_s']) for g, o in zip(grams, ops)]
    wu = [dot(t.astype(BF16), jnp.concatenate([o['ar_s'][0:L2], x.astype(BF16)], axis=1))
          for t, o, x in zip(tinv, ops, x1)]
    s_old = [s_scr[c] for c in range(len(chains))]
    hy = [_bdot_nt(jnp.concatenate([w[:, 0:L2].astype(BF16), o['ar_s'][L2:L4]], axis=0), s)
          for w, o, s in zip(wu, ops, s_old)]
    u = [h[0:L2] + w[:, L2:L4] for h, w in zip(hy, wu)]
    uv = [jnp.concatenate([uu.astype(BF16), o['v_s']], axis=0) for uu, o in zip(u, ops)]
    ys = [h[L2:L4] + dot(g[L2:L4, :].astype(BF16), x) for h, g, x in zip(hy, grams, uv)]
    for c, (s, uu, o) in enumerate(zip(s_old, u, ops)):
        s_scr[c] = s * jnp.exp(o['tot']) + dot(
            jnp.concatenate([uu.T.astype(BF16), o['vt_s']], axis=1), o['bkh_s'])

    y_all = [jnp.concatenate([ys[si_ * n_pair + j][0:L] + ys[si_ * n_pair + j][L:L2]
                              for j in range(n_pair)], axis=1) for si_ in range(len(streams))]
    d_all = [y - mu * (1.0 / HEAD_DIM) for y, mu in zip(y_all, _head_sum_all(y_all, ones3))]
    var_all = _head_sum_all([d * d for d in d_all], ones3)
    for st, d, vs in zip(streams, d_all, var_all):
        var = vs * (1.0 / HEAD_DIM)
        out = (d * lax.rsqrt(var + GN_EPS) * gng_ref[...] + gnb_ref[...] + st['bonus']) * st['gate']
        if st['dr'] == 0:
            of_ref[st['b']] = out
        else:
            ob_ref[st['b']] = out

    @pl.when(i == pl.num_programs(0) - 1)
    def _():
        so_ref[...] = s_scr[...]


def _rwkv_bidir(prep, lora, state0, prm):
    g, m, w4 = prep.shape
    rw = w4 // 4
    nblk = m // CHUNK
    w0, w2pad, a0, a2pad, g2, k_a, r_k, gn_g, gn_b = prm
    ones3 = _head_ones3()
    row = lambda a: a.reshape(1, -1)
    fmap = lambda i: (0, i, 0)
    bmap = lambda i: (0, nblk - 1 - i, 0)
    full = lambda a: pl.BlockSpec(a.shape, lambda i: (0,) * a.ndim)
    w0, a0 = w0[:, None, :], a0[:, None, :]
    consts = [w0, w2pad, a0, a2pad, g2, row(k_a), row(r_k), row(gn_g), row(gn_b), ones3]
    st_spec = pl.BlockSpec(state0.shape, lambda i: (0, 0, 0))
    y_f, y_b, s_out = pl.pallas_call(
        _rwkv_bidir_kernel,
        out_shape=[jax.ShapeDtypeStruct((g, m, rw), F32), jax.ShapeDtypeStruct((g, m, rw), F32),
                   jax.ShapeDtypeStruct(state0.shape, F32)],
        grid=(nblk,),
        in_specs=[pl.BlockSpec((g, CHUNK, w4), fmap), pl.BlockSpec((g, CHUNK, w4), bmap),
                  pl.BlockSpec((g, CHUNK, lora.shape[2]), fmap),
                  pl.BlockSpec((g, CHUNK, lora.shape[2]), bmap),
                  st_spec] + [full(a) for a in consts],
        out_specs=[pl.BlockSpec((g, CHUNK, rw), fmap), pl.BlockSpec((g, CHUNK, rw), bmap), st_spec],
        scratch_shapes=[pltpu.VMEM(state0.shape, F32)],
        compiler_params=_cp(("arbitrary",)),
        name="rwkv_scan",
    )(prep, prep, lora, lora, state0, *consts)
    return y_f, y_b, s_out


def _na_bias_table(rpb):
    n_head = rpb.shape[0]
    col = np.arange(GRID_W)
    col_start = np.clip(col - WIN_COLS // 2, 0, GRID_W - WIN_COLS)
    col_in = (col[None, :] >= col_start[:, None]) & (col[None, :] < col_start[:, None] + WIN_COLS)
    col_rel = np.clip(col[None, :] - col[:, None] + WIN_COLS - 1, 0, 2 * WIN_COLS - 2)
    n_rel = 2 * WIN_COLS - 1
    sel = (col_rel.reshape(-1)[None, :] == np.arange(n_rel)[:, None]).astype(np.float32)
    c = jnp.einsum('hrc,cn->hrn', rpb, jnp.asarray(sel), precision=HI)
    c = c.reshape(n_head, 2 * WIN_ROWS - 1, GRID_W, GRID_W)
    c = jnp.where(col_in[None, None], c, NEG_BIG)
    t = jnp.stack([c[:, d:d + WIN_ROWS] for d in range(WIN_ROWS)], axis=1)
    t = t.transpose(0, 1, 3, 2, 4).reshape(n_head // 2, 2, WIN_ROWS, GRID_W, WIN_ROWS * GRID_W)
    return t.astype(F32)


def _na_kernel(rows, q_ref, kp_ref, kc_ref, kn_ref, vp_ref, vc_ref, vn_ref, kx_ref, vx_ref,
               b_ref, o_ref, kcat, vcat):
    i = pl.program_id(2)
    w = GRID_W
    blk = WIN_ROWS * w
    scale = HEAD_DIM ** -0.5
    kcat[0:blk] = kp_ref[0].astype(BF16)
    kcat[blk:2 * blk] = kc_ref[0].astype(BF16)
    kcat[2 * blk:3 * blk] = kn_ref[0].astype(BF16)
    vcat[0:blk] = vp_ref[0].astype(BF16)
    vcat[blk:2 * blk] = vc_ref[0].astype(BF16)
    vcat[2 * blk:3 * blk] = vn_ref[0].astype(BF16)
    kx = kx_ref[0].astype(BF16)
    vx = vx_ref[0].astype(BF16)
    lo_mask = lax.broadcasted_iota(jnp.int32, (1, LANES), 1) < HEAD_DIM
    rng = range(WIN_ROWS)
    r0 = [jnp.clip(i * WIN_ROWS + rr - WIN_ROWS // 2, 0, rows - WIN_ROWS) for rr in rng]
    off = [pl.multiple_of((r0[rr] - (i - 1) * WIN_ROWS) * w, w) for rr in rng]
    dlt = [r0[rr] - (i * WIN_ROWS + rr) + WIN_ROWS - 1 for rr in rng]
    qs = [_stack_heads(q_ref[0, rr * w:(rr + 1) * w, :].astype(F32), lo_mask).astype(BF16)
          for rr in rng]
    s = [_bdot_nt(qs[rr], kcat[pl.ds(off[rr], blk), :]) * scale
         + jnp.concatenate([b_ref[0, 0, dlt[rr]], b_ref[0, 1, dlt[rr]]], axis=0) for rr in rng]
    sx = [_bdot_nt(q, kx) * scale for q in qs]
    mx = [jnp.maximum(jnp.max(a, axis=-1, keepdims=True), jnp.max(b, axis=-1, keepdims=True))
          for a, b in zip(s, sx)]
    p = [jnp.exp(a - m) for a, m in zip(s, mx)]
    px = [jnp.exp(a - m) for a, m in zip(sx, mx)]
    den = [jnp.sum(a, axis=-1, keepdims=True) + jnp.sum(b, axis=-1, keepdims=True)
           for a, b in zip(p, px)]
    o = [(_bdot(p[rr], vcat[pl.ds(off[rr], blk), :]) + _bdot(px[rr], vx)) / den[rr] for rr in rng]
    for rr in rng:
        o_ref[0, rr * w:(rr + 1) * w, :] = jnp.where(lo_mask, o[rr][0:w],
                                                     o[rr][w:2 * w]).astype(BF16)


def _na_attention(qkv, qkv_ctx, bias_tab):
    bsz, t_len, w3 = qkv.shape
    na = w3 // 3
    n_pair = na // LANES
    rows = t_len // GRID_W
    blk = WIN_ROWS * GRID_W
    nblk = rows // WIN_ROWS
    c_len = qkv_ctx.shape[1]
    kv = lambda sel, shift: pl.BlockSpec(
        (1, blk, LANES), lambda b, j, i: (b, jnp.clip(i + shift, 0, nblk - 1), sel * n_pair + j))
    cx = lambda sel: pl.BlockSpec((1, c_len, LANES), lambda b, j, i: (b, 0, sel * n_pair + j))
    return pl.pallas_call(
        functools.partial(_na_kernel, rows),
        out_shape=jax.ShapeDtypeStruct((bsz, t_len, na), BF16),
        grid=(bsz, n_pair, nblk),
        in_specs=[kv(0, 0), kv(1, -1), kv(1, 0), kv(1, 1), kv(2, -1), kv(2, 0), kv(2, 1),
                  cx(1), cx(2),
                  pl.BlockSpec((1, 2, WIN_ROWS, GRID_W, blk), lambda b, j, i: (j, 0, 0, 0, 0))],
        out_specs=pl.BlockSpec((1, blk, LANES), lambda b, j, i: (b, i, j)),
        scratch_shapes=[pltpu.VMEM((3 * blk, LANES), BF16), pltpu.VMEM((3 * blk, LANES), BF16)],
        compiler_params=_cp(("parallel", "parallel", "arbitrary")),
        name="na_attention",
    )(qkv, qkv, qkv, qkv, qkv, qkv, qkv, qkv_ctx, qkv_ctx, bias_tab)


def _ctx_attn_kernel(q_ref, k_ref, v_ref, o_ref):
    lo_mask = lax.broadcasted_iota(jnp.int32, (1, LANES), 1) < HEAD_DIM
    q = q_ref[0].astype(F32)
    c = q.shape[0]
    qs = _stack_heads(q, lo_mask)
    s = _bdot_nt(qs, k_ref[0]) * (HEAD_DIM ** -0.5)
    mx = jnp.max(s, axis=-1, keepdims=True)
    p = jnp.exp(s - mx)
    o = _bdot(p, v_ref[0]) / jnp.sum(p, axis=-1, keepdims=True)
    o_ref[0] = jnp.where(lo_mask, o[0:c], o[c:2 * c]).astype(BF16)


def _ctx_attention(qkv_ctx):
    bsz, c_len, w3 = qkv_ctx.shape
    na = w3 // 3
    n_pair = na // LANES
    sp = lambda sel: pl.BlockSpec((1, c_len, LANES), lambda b, j: (b, 0, sel * n_pair + j))
    return pl.pallas_call(
        _ctx_attn_kernel,
        out_shape=jax.ShapeDtypeStruct((bsz, c_len, na), BF16),
        grid=(bsz, n_pair),
        in_specs=[sp(0), sp(1), sp(2)],
        out_specs=pl.BlockSpec((1, c_len, LANES), lambda b, j: (b, 0, j)),
        compiler_params=_cp(("parallel", "parallel")),
        name="ctx_attention",
    )(qkv_ctx, qkv_ctx, qkv_ctx)


def _layer_norm(z, g, b):
    mu = jnp.mean(z, axis=-1, keepdims=True)
    d = z - mu
    var = jnp.mean(d * d, axis=-1, keepdims=True)
    return d * lax.rsqrt(var + LN_EPS) * g + b


def _outproj_kernel(alpha, n_alias, f_ref, rwf_ref, rwb_ref, at_ref, x_ref, ga_ref, sc_ref, sh_ref,
                    wfm_ref, wo_ref, g_ref, b_ref, wr_ref, br_ref, *rest):
    xo_ref, h_ref, lg_ref = rest[n_alias:]
    nf = f_ref.shape[2]
    nr = rwf_ref.shape[2]
    fm = jnp.dot(f_ref[0].astype(BF16), wfm_ref[...], preferred_element_type=F32)
    rwo = (rwf_ref[0] + rwb_ref[0]).astype(BF16)
    mix = (jnp.dot(fm.astype(BF16), wo_ref[0:nf, :], preferred_element_type=F32)
           + jnp.dot(rwo, wo_ref[nf:nf + nr, :], preferred_element_type=F32)
           + jnp.dot(at_ref[0].astype(BF16), wo_ref[nf + nr:, :], preferred_element_type=F32))
    xn = _layer_norm(alpha * x_ref[0] + ga_ref[0] * mix, g_ref[...], b_ref[...])
    xo_ref[0] = xn
    h = xn * (1.0 + sc_ref[0]) + sh_ref[0]
    h_ref[...] = h
    lg_ref[...] = _mm3(h, wr_ref[...]) + br_ref[...]


def _outproj(alpha, four, rwo, rwb, att, x, ga, sc, sh, wfm, wo, ln_g, ln_b, wr_pad, br_pad, tm,
             n_all, row0, bufs):
    g, m, d = x.shape
    nb = m // tm
    base = row0 // tm
    blk = lambda a: pl.BlockSpec((1, tm, a.shape[2]), lambda b, i: (b, i, 0))
    mod = pl.BlockSpec((1, 1, d), lambda b, i: (b, 0, 0))
    cst = lambda a: pl.BlockSpec(a.shape, lambda b, i: (0, 0))
    row = lambda a: a.reshape(1, -1)
    ln_g, ln_b, br_pad = row(ln_g), row(ln_b), row(br_pad)
    args = [four, rwo, rwb, att, x, ga, sc, sh, wfm, wo, ln_g, ln_b, wr_pad, br_pad]
    in_specs = [blk(four), blk(rwo), blk(rwb), blk(att), blk(x), mod, mod, mod, cst(wfm), cst(wo),
                cst(ln_g), cst(ln_b), cst(wr_pad), cst(br_pad)]
    aliases = {}
    if bufs is not None:
        aliases = {len(args): 1, len(args) + 1: 2}
        args += list(bufs)
        in_specs += [pl.BlockSpec(memory_space=pl.ANY)] * 2
    return pl.pallas_call(
        functools.partial(_outproj_kernel, alpha, len(aliases)),
        out_shape=[jax.ShapeDtypeStruct((g, m, d), F32), jax.ShapeDtypeStruct((n_all, d), F32),
                   jax.ShapeDtypeStruct((n_all, LANES), F32)],
        grid=(g, nb),
        in_specs=in_specs,
        out_specs=[pl.BlockSpec((1, tm, d), lambda b, i: (b, i, 0)),
                   pl.BlockSpec((tm, d), lambda b, i: (base + b * nb + i, 0)),
                   pl.BlockSpec((tm, LANES), lambda b, i: (base + b * nb + i, 0))],
        input_output_aliases=aliases,
        compiler_params=_cp(("parallel", "parallel")),
        name="outproj_ln",
    )(*args)


def _ln2_kernel(alpha, x_ref, y0_ref, y1_ref, y2_ref, y3_ref, gt_ref, ga_ref, g_ref, b_ref,
                o_ref):
    gt = gt_ref[...]
    y = (gt[:, 4:5] * y0_ref[0] + gt[:, 5:6] * y1_ref[0]
         + gt[:, 6:7] * y2_ref[0] + gt[:, 7:8] * y3_ref[0])
    o_ref[0] = _layer_norm(alpha * x_ref[0] + ga_ref[0] * y, g_ref[...], b_ref[...])


def _ln2(alpha, x, yk, gates, row0, ga, ln_g, ln_b, tm):
    g, m, d = x.shape
    nb = m // tm
    base = row0 // tm
    blk = pl.BlockSpec((1, tm, d), lambda b, i: (b, i, 0))
    vec = pl.BlockSpec((1, d), lambda b, i: (0, 0))
    ysp = lambda k: pl.BlockSpec((1, tm, d), lambda b, i: (k, base + b * nb + i, 0))
    return pl.pallas_call(
        functools.partial(_ln2_kernel, alpha),
        out_shape=jax.ShapeDtypeStruct((g, m, d), F32),
        grid=(g, nb),
        in_specs=[blk, ysp(0), ysp(1), ysp(2), ysp(3),
                  pl.BlockSpec((tm, LANES), lambda b, i: (base + b * nb + i, 0)),
                  pl.BlockSpec((1, 1, d), lambda b, i: (b, 0, 0)), vec, vec],
        out_specs=blk,
        compiler_params=_cp(("parallel", "parallel")),
        name="ln2",
    )(x, yk, yk, yk, yk, gates, ga, ln_g.reshape(1, -1), ln_b.reshape(1, -1))


def _moe_kernel(be_ref, nu_ref, x_ref, wgu_ref, bgu_ref, wdn_ref, bdn_ref, o_ref, wgu_s, wdn_s):
    i = pl.program_id(0)

    @pl.when((i == 0) | (be_ref[i] != be_ref[jnp.maximum(i - 1, 0)]))
    def _():
        wgu_s[...] = wgu_ref[0, 0].astype(BF16)
        wdn_s[...] = wdn_ref[0, 0].astype(BF16)

    @pl.when(i < nu_ref[0])
    def _():
        de = wdn_s.shape[0]
        gu = (jnp.dot(x_ref[...].astype(BF16), wgu_s[...], preferred_element_type=F32)
              + bgu_ref[0, 0])
        gg = jnp.minimum(gu[:, 0:de], SWIGLU_LIMIT)
        uu = jnp.clip(gu[:, de:2 * de], -SWIGLU_LIMIT, SWIGLU_LIMIT)
        act = (uu + 1.0) * gg * _sigmoid(SWIGLU_ALPHA * gg)
        o_ref[...] = (jnp.dot(act.astype(BF16), wdn_s[...], preferred_element_type=F32)
                      + bdn_ref[0, 0])


def _moe_experts(xs, block_e, n_used, layer, w_gu, b_gu, w_dn, b_dn, tm):
    cap, d = xs.shape
    n_layer, n_exp, _, de2 = w_gu.shape
    de = de2 // 2
    n_blocks = cap // tm
    wmap = lambda i, be, nu: (layer, be[i], 0, 0)
    grid_spec = pltpu.PrefetchScalarGridSpec(
        num_scalar_prefetch=2,
        grid=(n_blocks,),
        in_specs=[pl.BlockSpec((tm, d), lambda i, be, nu: (i, 0)),
                  pl.BlockSpec((1, 1, d, de2), wmap),
                  pl.BlockSpec((1, 1, 1, de2), wmap),
                  pl.BlockSpec((1, 1, de, d), wmap),
                  pl.BlockSpec((1, 1, 1, d), wmap)],
        out_specs=pl.BlockSpec((tm, d), lambda i, be, nu: (i, 0)),
        scratch_shapes=[pltpu.VMEM((d, de2), BF16), pltpu.VMEM((de, d), BF16)],
    )
    return pl.pallas_call(
        _moe_kernel,
        out_shape=jax.ShapeDtypeStruct((cap, d), F32),
        grid_spec=grid_spec,
        compiler_params=_cp(("arbitrary",)),
        name="moe_experts",
    )(block_e, n_used, xs, w_gu, b_gu.reshape(n_layer, n_exp, 1, de2), w_dn,
      b_dn.reshape(n_layer, n_exp, 1, d))


def _router_kernel(lg_ref, tri_ref, r_ref, cnt_ref, carry):
    i = pl.program_id(0)

    @pl.when(i == 0)
    def _():
        carry[...] = jnp.zeros_like(carry)

    lane = lax.broadcasted_iota(jnp.int32, lg_ref.shape, 1)
    lanef = lane.astype(F32)
    lg = jnp.where(lane < N_EXPERTS, lg_ref[...], NEG_BIG)
    tops, idxs, hots = [], [], []
    for _ in range(TOP_K):
        m = jnp.max(lg, axis=-1, keepdims=True)
        idx = jnp.min(jnp.where(lg == m, lanef, float(LANES)), axis=-1, keepdims=True)
        hot = lanef == idx
        lg = jnp.where(hot, NEG_BIG, lg)
        tops.append(m)
        idxs.append(idx)
        hots.append(hot)
    ex = [jnp.exp(m - tops[0]) for m in tops]
    den = ex[0] + ex[1] + ex[2] + ex[3]
    oh = jnp.where(hots[0] | hots[1] | hots[2] | hots[3], 1.0, 0.0)
    before = carry[...] + jnp.dot(tri_ref[...], oh.astype(BF16), preferred_element_type=F32)
    rec = jnp.zeros(lg_ref.shape, F32)
    for k in range(TOP_K):
        rank = jnp.sum(jnp.where(hots[k], before, 0.0), axis=-1, keepdims=True)
        rec = jnp.where(lane == k, idxs[k], rec)
        rec = jnp.where(lane == TOP_K + k, ex[k] / den, rec)
        rec = jnp.where(lane == 2 * TOP_K + k, rank, rec)
    r_ref[...] = rec
    carry[...] = carry[...] + jnp.sum(oh, axis=0, keepdims=True)
    cnt_ref[...] = carry[...]


def _route(logits):
    n_tok = logits.shape[0]
    tb = min(ROUTE_TB, n_tok)
    assert n_tok % tb == 0
    tri =jnp.asarray(np.tril(np.ones((tb, tb), np.float32), -1), BF16)
    return pl.pallas_call(
        _router_kernel,
        out_shape=[jax.ShapeDtypeStruct((n_tok, LANES), F32), jax.ShapeDtypeStruct((1, LANES), F32)],
        grid=(n_tok // tb,),
        in_specs=[pl.BlockSpec((tb, LANES), lambda i: (i, 0)), pl.BlockSpec((tb, tb), lambda i: (0, 0))],
        out_specs=[pl.BlockSpec((tb, LANES), lambda i: (i, 0)), pl.BlockSpec((1, LANES), lambda i: (0, 0))],
        scratch_shapes=[pltpu.VMEM((1, LANES), F32)],
        compiler_params=_cp(("arbitrary",)),
        name="router",
    )(logits, tri)


def _moe(h, logits, layer, wgu, bgu, wdn, bdn, tm):
    n_tok, d = h.shape
    n_slot = n_tok * TOP_K
    rec, cnt = _route(logits)
    top_e = rec[:, 0:TOP_K].astype(jnp.int32)
    rank = rec[:, 2 * TOP_K:3 * TOP_K].astype(jnp.int32)
    counts = cnt[0, 0:N_EXPERTS].astype(jnp.int32)
    first = jnp.cumsum(counts) - counts
    padded = (counts + tm - 1) // tm * tm
    pad_end = jnp.cumsum(padded)
    pad_start = pad_end - padded
    n_blocks = -(-n_slot // tm) + N_EXPERTS
    cap = n_blocks * tm
    blk_row = jnp.arange(n_blocks, dtype=jnp.int32) * tm
    block_e = jnp.minimum(jnp.sum(pad_end[None, :] <= blk_row[:, None], axis=1),
                          N_EXPERTS - 1).astype(jnp.int32)
    n_used = (pad_end[-1:] // tm).astype(jnp.int32)
    experts = jnp.arange(N_EXPERTS, dtype=jnp.int32)
    pos = jnp.sum(jnp.where(top_e[..., None] == experts, pad_start, 0), axis=-1) + rank
    order = jnp.argsort(pos.reshape(-1)).astype(jnp.int32)
    row_e = jnp.repeat(block_e, tm)
    off = jnp.arange(cap, dtype=jnp.int32) - pad_start[row_e]
    src = jnp.clip(first[row_e] + off, 0, n_slot - 1)
    slot_tok = jnp.where(off < counts[row_e], order.at[src].get(mode='promise_in_bounds') // TOP_K, 0)
    xs = h.at[slot_tok].get(mode='promise_in_bounds')
    ys = _moe_experts(xs, block_e, n_used, layer, wgu, bgu, wdn, bdn, tm)
    yk = ys.at[pos.T.reshape(-1)].get(mode='promise_in_bounds').reshape(TOP_K, n_tok, d)
    return yk, rec


def _pad_rows(w, start, total):
    return jnp.zeros((total, w.shape[1]), w.dtype).at[start:start + w.shape[0]].set(w)


def kernel(x, c, ctx, c_ctx, w_ada, b_ada, w_in, w_vdown, w_fmap, ts_prev, ts_next, rw_w0, rw_w2, rw_a0, rw_a2, rw_g2, rw_kk, rw_ka, rw_rk, rw_gn_g, rw_gn_b, rw_v0, rw_v2, na_rpb, w_out, ln1_g, ln1_b, w_router, b_router, w_gu, b_gu, w_dn, b_dn, ln2_g, ln2_b):
    bsz, n_lat, d = x.shape
    c_len = ctx.shape[1]
    depth = w_in.shape[0]
    f_dim = w_fmap.shape[1]
    rw_dim = rw_kk.shape[1]
    alpha = (2 * depth) ** 0.25
    n_pair = rw_dim // LANES

    tm_lat = min(512, n_lat)
    tm_ctx = min(256, c_len)
    tb_ctx = min(128, c_len)
    tm_moe = 512
    n1_lat, n2_lat = _split_len(n_lat)
    n1_ctx, n2_ctx = _split_len(c_len)

    cond = jnp.zeros((SUBLANES, d), F32).at[0:bsz].set(c).at[bsz].set(c_ctx)
    ada = _ada_all(cond, w_ada, b_ada)

    xl, xc = x, ctx
    vf_l = vf_c = None
    o_r = f_dim
    o_lora = f_dim + 3 * rw_dim
    o_q = o_lora + 2 * DECAY_LORA + 2 * ICLR_LORA + 2 * GATE_LORA
    for l in range(depth):
        last = l == depth - 1
        mods = ada[l].reshape(SUBLANES, 6, d)
        lat = lambda k: mods[0:bsz, k][:, None, :]
        cx = lambda k: jnp.broadcast_to(mods[bsz, k][None, None, :], (bsz, 1, d))

        wi = w_in[l]
        vdown = jnp.zeros((d, LANES), F32)
        if l > 0:
            vdown = vdown.at[:, 0:VRES_LORA].set(w_vdown[l - 1])
        wf = wi[:, 0:f_dim].astype(BF16)
        wr = wi[:, o_r:o_lora].astype(BF16)
        wl = jnp.concatenate([wi[:, o_lora:o_q], vdown], axis=1).astype(BF16)
        wq = wi[:, o_q:].astype(BF16)
        f_l, rkv_l, lora_l, qkv_l = _inproj(xl, lat(1), lat(0), wf, wr, wl, wq, tm_lat)
        f_c, rkv_c, lora_c, qkv_c = _inproj(xc, cx(1), cx(0), wf, wr, wl, wq, tm_ctx)

        if l == 0:
            v0 = v2pad = None
        else:
            v0 = rw_v0[l - 1]
            v2pad = _pad_rows(rw_v2[l - 1], 0, LANES)
        prep_c, vf_c = _rwkv_prep(rkv_c, lora_c, ts_prev[l], ts_next[l], rw_kk[l], vf_c, v0, v2pad,
                                  tb_ctx)
        prep_l, vf_l = _rwkv_prep(rkv_l, lora_l, ts_prev[l], ts_next[l], rw_kk[l], vf_l, v0, v2pad,
                                  tm_lat)
        prm = (rw_w0[l],
               jnp.stack([_pad_rows(rw_w2[l, dr], dr * DECAY_LORA, LANES) for dr in range(2)]),
               rw_a0[l],
               jnp.stack([_pad_rows(rw_a2[l, dr], dr * ICLR_LORA, LANES) for dr in range(2)]),
               rw_g2[l], rw_ka[l], rw_rk[l].reshape(-1), rw_gn_g[l], rw_gn_b[l])
        s0 = jnp.zeros((2 * bsz * n_pair, LANES, LANES), F32)
        rwf_c, rwb_c, s_ctx = _rwkv_bidir(prep_c, lora_c, s0, prm)
        rwf_l, rwb_l, _ = _rwkv_bidir(prep_l, lora_l, s_ctx, prm)

        att_l = _na_attention(qkv_l, qkv_c, _na_bias_table(na_rpb[l]))
        four_l = _fourier(f_l, n1_lat, n2_lat)

        wfm = w_fmap[l].astype(BF16)
        wo = w_out[l].astype(BF16)
        wr_pad = jnp.zeros((d, LANES), F32).at[:, 0:N_EXPERTS].set(w_router[l])
        br_pad = jnp.zeros((LANES,), F32).at[0:N_EXPERTS].set(b_router[l])
        n_l = bsz * n_lat
        n_all = n_l if last else n_l + bsz * c_len
        xl, h_all, lg_all = _outproj(alpha, four_l, rwf_l, rwb_l, att_l, xl, lat(2), lat(4), lat(3),
                                     wfm, wo, ln1_g[l], ln1_b[l], wr_pad, br_pad, tm_lat, n_all, 0,
                                     None)
        if last:
            yk, gates = _moe(h_all, lg_all, l, w_gu, b_gu, w_dn, b_dn, tm_moe)
            xl = _ln2(alpha, xl, yk, gates, 0, lat(5), ln2_g[l], ln2_b[l], tm_lat)
        else:
            att_c = _ctx_attention(qkv_c)
            four_c = _fourier(f_c, n1_ctx, n2_ctx)
            xc, h_all, lg_all = _outproj(alpha, four_c, rwf_c, rwb_c, att_c, xc, cx(2), cx(4), cx(3),
                                         wfm, wo, ln1_g[l], ln1_b[l], wr_pad, br_pad, tm_ctx, n_all,
                                         n_l, (h_all, lg_all))
            yk, gates = _moe(h_all, lg_all, l, w_gu, b_gu, w_dn, b_dn, tm_moe)
            xl = _ln2(alpha, xl, yk, gates, 0, lat(5), ln2_g[l], ln2_b[l], tm_lat)
            xc = _ln2(alpha, xc, yk, gates, n_l, cx(5), ln2_g[l], ln2_b[l], tm_ctx)
    return xl
```

```python
import functools

import numpy as np
import jax
import jax.numpy as jnp
from jax import lax
from jax.experimental import pallas as pl
from jax.experimental.pallas import tpu as pltpu

F32 = jnp.float32
BF16 = jnp.bfloat16
HI = lax.Precision.HIGHEST

GRID_W = 64
HEAD_DIM = 64
WIN_ROWS = 8
WIN_COLS = 16
N_EXPERTS = 32
TOP_K = 4
SWIGLU_LIMIT = 7.0
SWIGLU_ALPHA = 1.702
LN_EPS = 1e-5
GN_EPS = HEAD_DIM * 1e-5
DECAY_LORA = 64
ICLR_LORA = 64
GATE_LORA = 128
VRES_LORA = 32

LANES = 128
SUBLANES = 8
VMEM_LIMIT = 56 * 1024 * 1024

CHUNK = 64
NEG_BIG = -1e30
ROUTE_TB = 512


def _cp(sem, vmem=VMEM_LIMIT):
    return pltpu.CompilerParams(dimension_semantics=sem, vmem_limit_bytes=vmem)


def _dot(a, b, prec=HI):
    return jnp.dot(a, b, precision=prec, preferred_element_type=F32)


def _bdot(a, b):
    return jnp.dot(a.astype(BF16), b.astype(BF16), preferred_element_type=F32)


def _bdot_nt(a, b):
    return lax.dot_general(a.astype(BF16), b.astype(BF16), (((1,), (1,)), ((), ())),
                           preferred_element_type=F32)


def _sigmoid(x):
    return 1.0 / (1.0 + jnp.exp(-x))


def _split2(x):
    x1 = x.astype(BF16)
    x2 = (x - x1.astype(F32)).astype(BF16)
    return x1, x2


def _lhs3(x):
    x1, x2 = _split2(x)
    return jnp.concatenate([x1, x2, x1], axis=1)


def _rhs3(y):
    y1, y2 = _split2(y)
    return jnp.concatenate([y1, y1, y2], axis=0)


def _mm3(a, b):
    return jnp.dot(_lhs3(a), _rhs3(b), preferred_element_type=F32)


def _head_sum_all(xs, ones3):
    m, width = xs[0].shape
    n_col = width // LANES
    tall = jnp.concatenate([x[:, j * LANES:(j + 1) * LANES] for x in xs for j in range(n_col)],
                           axis=0)
    x1 = tall.astype(BF16)
    r1 = tall - x1.astype(F32)
    x2 = r1.astype(BF16)
    x3 = (r1 - x2.astype(F32)).astype(BF16)
    out = jnp.dot(jnp.concatenate([x1, x2, x3], axis=1), ones3, preferred_element_type=F32)
    return [jnp.concatenate([out[(i * n_col + j) * m:(i * n_col + j + 1) * m]
                             for j in range(n_col)], axis=1) for i in range(len(xs))]


def _head_sum(x, ones3):
    return _head_sum_all([x], ones3)[0]


def _ada_kernel(c_ref, w_ref, b_ref, o_ref):
    c = c_ref[...]
    o_ref[0] = _dot(c * _sigmoid(c), w_ref[0]) + b_ref[0]


def _ada_all(cond, w_ada, b_ada):
    n_layer, d, n = w_ada.shape
    tn = 1536
    return pl.pallas_call(
        _ada_kernel,
        out_shape=jax.ShapeDtypeStruct((n_layer, SUBLANES, n), F32),
        grid=(n_layer, n // tn),
        in_specs=[pl.BlockSpec((SUBLANES, d), lambda l, j: (0, 0)),
                  pl.BlockSpec((1, d, tn), lambda l, j: (l, 0, j)),
                  pl.BlockSpec((1, 1, tn), lambda l, j: (l, 0, j))],
        out_specs=pl.BlockSpec((1, SUBLANES, tn), lambda l, j: (l, 0, j)),
        compiler_params=_cp(("arbitrary", "arbitrary")),
        name="ada",
    )(cond, w_ada, b_ada.reshape(n_layer, 1, n))


def _inproj_kernel(x_ref, sc_ref, sh_ref, wf_ref, wr_ref, wl_ref, wq_ref,
                   of_ref, or_ref, ol_ref, oq_ref):
    xm = (x_ref[0] * (1.0 + sc_ref[0]) + sh_ref[0]).astype(BF16)
    of_ref[0] = jnp.dot(xm, wf_ref[...], preferred_element_type=F32)
    or_ref[0] = jnp.dot(xm, wr_ref[...], preferred_element_type=F32)
    ol_ref[0] = jnp.dot(xm, wl_ref[...], preferred_element_type=F32)
    oq_ref[0] = jnp.dot(xm, wq_ref[...], preferred_element_type=F32).astype(BF16)


def _inproj(x, sc, sh, wf, wr, wl, wq, tm):
    g, m, d = x.shape
    ws = (wf, wr, wl, wq)
    mod = pl.BlockSpec((1, 1, d), lambda b, i: (b, 0, 0))
    return pl.pallas_call(
        _inproj_kernel,
        out_shape=[jax.ShapeDtypeStruct((g, m, w.shape[1]), dt)
                   for w, dt in zip(ws, (F32, F32, F32, BF16))],
        grid=(g, m // tm),
        in_specs=[pl.BlockSpec((1, tm, d), lambda b, i: (b, i, 0)), mod, mod]
                 + [pl.BlockSpec(w.shape, lambda b, i: (0, 0)) for w in ws],
        out_specs=[pl.BlockSpec((1, tm, w.shape[1]), lambda b, i: (b, i, 0)) for w in ws],
        compiler_params=_cp(("parallel", "parallel")),
        name="inproj",
    )(x, sc, sh, *ws)


def _dft_consts(t_len, n1, n2, tb2, width):
    groups = width // HEAD_DIM
    j = np.arange(HEAD_DIM)
    ang = 2.0 * np.pi * np.outer(j, j) / HEAD_DIM
    eye = np.eye(groups)
    cbd = np.kron(eye, np.cos(ang)) / np.sqrt(HEAD_DIM)
    sbd = np.kron(eye, np.sin(ang)) / np.sqrt(HEAD_DIM)
    a1 = 2.0 * np.pi * np.outer(np.arange(n1), np.arange(n1)) / n1
    c1, s1 = np.cos(a1), np.sin(a1)
    atw = 2.0 * np.pi * np.outer(np.arange(n1), np.arange(n2)) / t_len
    twc = np.cos(atw).reshape(n1, n2 // tb2, tb2).transpose(1, 0, 2)
    tws = np.sin(atw).reshape(n1, n2 // tb2, tb2).transpose(1, 0, 2)
    a2 = 2.0 * np.pi * np.outer(np.arange(n2), np.arange(n2)) / n2
    c2, s2 = np.cos(a2) / np.sqrt(t_len), np.sin(a2) / np.sqrt(t_len)
    return [jnp.asarray(v, F32) for v in (cbd, sbd, c1, s1, twc, tws, c2, s2)]


def _four_a_kernel(tb2, width, x_ref, cbd_ref, sbd_ref, c1_ref, s1_ref, twc_ref, tws_ref,
                   yr_ref, yi_ref):
    dot = functools.partial(jnp.dot, preferred_element_type=F32)
    c1 = _lhs3(c1_ref[...])
    s1 = _lhs3(s1_ref[...])
    cbd = _rhs3(cbd_ref[...])
    sbd = _rhs3(sbd_ref[...])
    for jj in range(tb2):
        x = _lhs3(x_ref[0, :, width * jj:width * (jj + 1)])
        zr = _rhs3(dot(x, cbd))
        zi = _rhs3(-dot(x, sbd))
        ar = dot(c1, zr) + dot(s1, zi)
        ai = dot(c1, zi) - dot(s1, zr)
        tc = twc_ref[0, :, jj:jj + 1]
        ts = tws_ref[0, :, jj:jj + 1]
        yr_ref[0, jj] = tc * ar + ts * ai
        yi_ref[0, jj] = tc * ai - ts * ar


def _four_b_kernel(yr_ref, yi_ref, c2_ref, s2_ref, o_ref):
    o_ref[0] = (_mm3(c2_ref[...], yr_ref[0]) + _mm3(s2_ref[...], yi_ref[0])).astype(BF16)


def _fourier(f, n1, n2):
    bsz, t_len, width = f.shape
    tb2 = SUBLANES
    cbd, sbd, c1, s1, twc, tws, c2, s2 = _dft_consts(t_len, n1, n2, tb2, width)
    const = lambda a: pl.BlockSpec(a.shape, lambda b, j: (0,) * a.ndim)
    yr, yi = pl.pallas_call(
        functools.partial(_four_a_kernel, tb2, width),
        out_shape=[jax.ShapeDtypeStruct((bsz, n2, n1, width), F32)] * 2,
        grid=(bsz, n2 // tb2),
        in_specs=[pl.BlockSpec((1, n1, tb2 * width), lambda b, j: (b, 0, j)),
                  const(cbd), const(sbd), const(c1), const(s1),
                  pl.BlockSpec((1, n1, tb2), lambda b, j: (j, 0, 0)),
                  pl.BlockSpec((1, n1, tb2), lambda b, j: (j, 0, 0))],
        out_specs=[pl.BlockSpec((1, tb2, n1, width), lambda b, j: (b, j, 0, 0))] * 2,
        compiler_params=_cp(("parallel", "parallel")),
        name="fourier_a",
    )(f.reshape(bsz, n1, n2 * width), cbd, sbd, c1, s1, twc, tws)
    ncol = n1 * width
    tc = min(ncol, 2048)
    out = pl.pallas_call(
        _four_b_kernel,
        out_shape=jax.ShapeDtypeStruct((bsz, n2, ncol), BF16),
        grid=(bsz, ncol // tc),
        in_specs=[pl.BlockSpec((1, n2, tc), lambda b, j: (b, 0, j)),
                  pl.BlockSpec((1, n2, tc), lambda b, j: (b, 0, j)),
                  const(c2), const(s2)],
        out_specs=pl.BlockSpec((1, n2, tc), lambda b, j: (b, 0, j)),
        compiler_params=_cp(("parallel", "parallel")),
        name="fourier_b",
    )(yr.reshape(bsz, n2, ncol), yi.reshape(bsz, n2, ncol), c2, s2)
    return out.reshape(bsz, t_len, width)


def _split_len(t_len):
    n1 = 1 << ((t_len.bit_length() - 1 + 1) // 2)
    return n1, t_len // n1


def _head_ones3():
    h = np.arange(LANES) // HEAD_DIM
    one = (h[:, None] == h[None, :]).astype(np.float32)
    return jnp.asarray(np.concatenate([one, one, one], axis=0), BF16)


def _rwkv_prep_kernel(first, x_ref, xp_ref, xn_ref, lo_ref, tsp_ref, tsn_ref, kk_ref,
                      ones_ref, *rest):
    if first:
        o_ref, vf_out_ref = rest
    else:
        v0_ref, v2_ref, vf_ref, o_ref = rest
    i = pl.program_id(1)
    last = pl.num_programs(1) - 1
    x = x_ref[0]
    tb, w3 = x.shape
    rw = w3 // 3
    row = lax.broadcasted_iota(jnp.int32, (tb, 1), 0)
    prev_edge = jnp.where(i == 0, 0.0, xp_ref[0, SUBLANES - 1:SUBLANES, :])
    next_edge = jnp.where(i == last, 0.0, xn_ref[0, 0:1, :])
    prev = jnp.where(row == 0, prev_edge, pltpu.roll(x, 1, 0))
    nxt = jnp.where(row == tb - 1, next_edge, pltpu.roll(x, tb - 1, 0))
    s = x + tsp_ref[...] * (prev - x) + tsn_ref[...] * (nxt - x)
    r = s[:, 0:rw]
    k = s[:, rw:2 * rw]
    v = s[:, 2 * rw:3 * rw]
    if first:
        vf_out_ref[0] = v
    else:
        mix = _sigmoid(v0_ref[...] + _mm3(lo_ref[0], v2_ref[...]))
        v = v + (vf_ref[0] - v) * mix
    kk = k * kk_ref[...]
    ss = _head_sum(kk * kk, ones_ref[...])
    kk = kk * lax.rsqrt(jnp.maximum(ss, 1e-24))
    o_ref[0, :, 0:rw] = r
    o_ref[0, :, rw:2 * rw] = k
    o_ref[0, :, 2 * rw:3 * rw] = v
    o_ref[0, :, 3 * rw:4 * rw] = kk


def _rwkv_prep(rkv, lora, ts_prev, ts_next, k_k, v_first, v0, v2pad, tb):
    g, m, w3 = rkv.shape
    rw = w3 // 3
    first = v_first is None
    nb8 = tb // SUBLANES
    n8 = m // SUBLANES
    ones3 = _head_ones3()
    row = lambda a: a.reshape(1, -1)
    vec = lambda n: pl.BlockSpec((1, n), lambda b, i: (0, 0))
    in_specs = [
        pl.BlockSpec((1, tb, w3), lambda b, i: (b, i, 0)),
        pl.BlockSpec((1, SUBLANES, w3), lambda b, i: (b, jnp.maximum(i * nb8 - 1, 0), 0)),
        pl.BlockSpec((1, SUBLANES, w3), lambda b, i: (b, jnp.minimum((i + 1) * nb8, n8 - 1), 0)),
        pl.BlockSpec((1, tb, LANES), lambda b, i: (b, i, 4)),
        vec(w3), vec(w3), vec(rw),
        pl.BlockSpec(ones3.shape, lambda b, i: (0, 0)),
    ]
    args = [rkv, rkv, rkv, lora, row(ts_prev), row(ts_next), row(k_k), ones3]
    out_main = jax.ShapeDtypeStruct((g, m, 4 * rw), F32)
    spec_main = pl.BlockSpec((1, tb, 4 * rw), lambda b, i: (b, i, 0))
    spec_v = pl.BlockSpec((1, tb, rw), lambda b, i: (b, i, 0))
    if first:
        out_shape = [out_main, jax.ShapeDtypeStruct((g, m, rw), F32)]
        out_specs = [spec_main, spec_v]
    else:
        in_specs += [vec(rw), pl.BlockSpec((LANES, rw), lambda b, i: (0, 0)), spec_v]
        args += [row(v0), v2pad, v_first]
        out_shape = out_main
        out_specs = spec_main
    res = pl.pallas_call(
        functools.partial(_rwkv_prep_kernel, first),
        out_shape=out_shape, grid=(g, m // tb), in_specs=in_specs, out_specs=out_specs,
        compiler_params=_cp(("parallel", "parallel")),
        name="rwkv_prep",
    )(*args)
    if first:
        return res[0], res[1]
    return res, v_first


def _stack_heads(x, lo_mask):
    return jnp.concatenate([jnp.where(lo_mask, x, 0.0), jnp.where(lo_mask, 0.0, x)], axis=0)


def _scan_masks(reverse):
    L = CHUNK
    L2, L4 = 2 * L, 4 * L
    ti = lax.broadcasted_iota(jnp.int32, (L4, L4), 0)
    si = lax.broadcasted_iota(jnp.int32, (L4, L4), 1)
    tq = ti % L2
    sq = si % L2
    tt = tq % L
    st = sq % L
    before = (st > tt) if reverse else (st < tt)
    keep = ((tq // L) == (sq // L)) & (before | ((ti >= L2) & (st == tt)))
    ci = lax.broadcasted_iota(jnp.int32, (L, L), 0)
    cj = lax.broadcasted_iota(jnp.int32, (L, L), 1)
    tri = jnp.where((cj >= ci) if reverse else (cj <= ci), 1.0, 0.0).astype(BF16)
    return keep, jnp.concatenate([tri, tri, tri], axis=1)


def _rwkv_bidir_kernel(pf_ref, pb_ref, lof_ref, lob_ref, s0_ref, w0_ref, w2_ref, a0_ref, a2_ref,
                       g2_ref, ka_ref, rk_ref, gng_ref, gnb_ref, ones_ref,
                       of_ref, ob_ref, so_ref, s_scr):
    i = pl.program_id(0)
    L = CHUNK
    L2, L4 = 2 * L, 4 * L
    bsz = pf_ref.shape[0]
    rw = pf_ref.shape[2] // 4
    n_pair = rw // LANES

    @pl.when(i == 0)
    def _():
        s_scr[...] = s0_ref[...]

    ones3 = ones_ref[...]
    lo_mask = lax.broadcasted_iota(jnp.int32, (1, LANES), 1) < HEAD_DIM
    ei = lax.broadcasted_iota(jnp.int32, (L2, L2), 0)
    ej = lax.broadcasted_iota(jnp.int32, (L2, L2), 1)
    eye = jnp.where(ei == ej, 1.0, 0.0)
    masks = (_scan_masks(False), _scan_masks(True))

    streams = []
    for dr, (p_ref, lo_ref) in enumerate(((pf_ref, lof_ref), (pb_ref, lob_ref))):
        lo = jnp.concatenate([lo_ref[b] for b in range(bsz)], axis=0)
        zw = w0_ref[dr] + _bdot(jnp.tanh(lo[:, 0:LANES]), w2_ref[dr])
        log_w = -(jnp.maximum(-zw, 0.0) + jnp.log(1.0 + jnp.exp(-jnp.abs(zw)))) - 0.5
        lw_all = -jnp.exp(log_w)
        iclr_all = _sigmoid(a0_ref[dr] + _bdot(lo[:, LANES:2 * LANES], a2_ref[dr]))
        gate_all = _bdot(_sigmoid(lo[:, (2 + dr) * LANES:(3 + dr) * LANES]), g2_ref[dr])
        for b in range(bsz):
            rows = slice(b * L, (b + 1) * L)
            p = p_ref[b]
            r = p[:, 0:rw]
            k = p[:, rw:2 * rw]
            v = p[:, 2 * rw:3 * rw]
            kk = p[:, 3 * rw:4 * rw]
            iclr = iclr_all[rows]
            kd = k * (1.0 + (iclr - 1.0) * ka_ref[...])
            streams.append(dict(dr=dr, b=b, r=r, kd=kd, v=v, av=-kk, bv=kk * iclr,
                                lw=lw_all[rows], gate=gate_all[rows]))
    rk_sums = _head_sum_all([st['r'] * st['kd'] * rk_ref[...] for st in streams], ones3)
    for st, rks in zip(streams, rk_sums):
        st['bonus'] = rks * st['v']

    chains = [(si_, j) for si_ in range(len(streams)) for j in range(n_pair)]

    def cols(si_, j, name):
        return streams[si_][name][:, j * LANES:(j + 1) * LANES]

    ops = []
    for si_, j in chains:
        dr = streams[si_]['dr']
        keep, tri3 = masks[dr]
        lwc = cols(si_, j, 'lw')
        l1 = lwc.astype(BF16)
        lr = lwc - l1.astype(F32)
        l2 = lr.astype(BF16)
        l3 = (lr - l2.astype(F32)).astype(BF16)
        cum = jnp.dot(tri3, jnp.concatenate([l1, l2, l3], axis=0), preferred_element_type=F32)
        end_row = 0 if dr else L - 1
        tot = cum[end_row:end_row + 1, :]
        g_inv = jnp.exp(-cum)
        g_end = jnp.exp(tot - cum)
        bvc = cols(si_, j, 'bv')
        kdc = cols(si_, j, 'kd')
        stk = lambda x: _stack_heads(x, lo_mask).astype(BF16)
        ops.append(dict(
            keep=keep, tot=tot,
            ar_s=jnp.concatenate([stk(cols(si_, j, 'av') * jnp.exp(cum - lwc)),
                                  stk(cols(si_, j, 'r') * jnp.exp(cum))], axis=0),
            bk_s=jnp.concatenate([stk(bvc * g_inv), stk(kdc * g_inv)], axis=0),
            v_s=stk(cols(si_, j, 'v')),
            vt_s=_stack_heads(cols(si_, j, 'v'), lo_mask).T.astype(BF16),
            bkh_s=jnp.concatenate([stk(bvc * g_end), stk(kdc * g_end)], axis=0)))

    dot = functools.partial(jnp.dot, preferred_element_type=F32)
    grams = [jnp.where(o['keep'], _bdot_nt(o['ar_s'], o['bk_s']), 0.0) for o in ops]
    tinv = [eye + g[0:L2, 0:L2] for g in grams]
    pw = [g[0:L2, 0:L2].astype(BF16) for g in grams]
    pw = [dot(a, a).astype(BF16) for a in pw]
    for _ in range(4):
        both = [dot(jnp.concatenate([p, t.astype(BF16)], axis=0), p) for p, t in zip(pw, tinv)]
        pw = [b[0:L2].astype(BF16) for b in both]
        tinv = [t + b[L2:L4] for t, b in zip(tinv, both)]
    tinv = [t + dot(t.astype(BF16), p) for t, p in zip(tinv, pw)]
    x1 = [dot(g[0:L2, L2:L4].astype(BF16), o['v_s']) for g, o in zip(grams, ops)]
    wu = [dot(t.astype(BF16), jnp.concatenate([o['ar_s'][0:L2], x.astype(BF16)], axis=1))
          for t, o, x in zip(tinv, ops, x1)]
    s_old = [s_scr[c] for c in range(len(chains))]
    hy = [_bdot_nt(jnp.concatenate([w[:, 0:L2].astype(BF16), o['ar_s'][L2:L4]], axis=0), s)
          for w, o, s in zip(wu, ops, s_old)]
    u = [h[0:L2] + w[:, L2:L4] for h, w in zip(hy, wu)]
    uv = [jnp.concatenate([uu.astype(BF16), o['v_s']], axis=0) for uu, o in zip(u, ops)]
    ys = [h[L2:L4] + dot(g[L2:L4, :].astype(BF16), x) for h, g, x in zip(hy, grams, uv)]
    for c, (s, uu, o) in enumerate(zip(s_old, u, ops)):
        s_scr[c] = s * jnp.exp(o['tot']) + dot(
            jnp.concatenate([uu.T.astype(BF16), o['vt_s']], axis=1), o['bkh_s'])

    y_all = [jnp.concatenate([ys[si_ * n_pair + j][0:L] + ys[si_ * n_pair + j][L:L2]
                              for j in range(n_pair)], axis=1) for si_ in range(len(streams))]
    d_all = [y - mu * (1.0 / HEAD_DIM) for y, mu in zip(y_all, _head_sum_all(y_all, ones3))]
    var_all = _head_sum_all([d * d for d in d_all], ones3)
    for st, d, vs in zip(streams, d_all, var_all):
        var = vs * (1.0 / HEAD_DIM)
        out = (d * lax.rsqrt(var + GN_EPS) * gng_ref[...] + gnb_ref[...] + st['bonus']) * st['gate']
        if st['dr'] == 0:
            of_ref[st['b']] = out
        else:
            ob_ref[st['b']] = out

    @pl.when(i == pl.num_programs(0) - 1)
    def _():
        so_ref[...] = s_scr[...]


def _rwkv_bidir(prep, lora, state0, prm):
    g, m, w4 = prep.shape
    rw = w4 // 4
    nblk = m // CHUNK
    w0, w2pad, a0, a2pad, g2, k_a, r_k, gn_g, gn_b = prm
    ones3 = _head_ones3()
    row = lambda a: a.reshape(1, -1)
    fmap = lambda i: (0, i, 0)
    bmap = lambda i: (0, nblk - 1 - i, 0)
    full = lambda a: pl.BlockSpec(a.shape, lambda i: (0,) * a.ndim)
    w0, a0 = w0[:, None, :], a0[:, None, :]
    consts = [w0, w2pad, a0, a2pad, g2, row(k_a), row(r_k), row(gn_g), row(gn_b), ones3]
    st_spec = pl.BlockSpec(state0.shape, lambda i: (0, 0, 0))
    y_f, y_b, s_out = pl.pallas_call(
        _rwkv_bidir_kernel,
        out_shape=[jax.ShapeDtypeStruct((g, m, rw), F32), jax.ShapeDtypeStruct((g, m, rw), F32),
                   jax.ShapeDtypeStruct(state0.shape, F32)],
        grid=(nblk,),
        in_specs=[pl.BlockSpec((g, CHUNK, w4), fmap), pl.BlockSpec((g, CHUNK, w4), bmap),
                  pl.BlockSpec((g, CHUNK, lora.shape[2]), fmap),
                  pl.BlockSpec((g, CHUNK, lora.shape[2]), bmap),
                  st_spec] + [full(a) for a in consts],
        out_specs=[pl.BlockSpec((g, CHUNK, rw), fmap), pl.BlockSpec((g, CHUNK, rw), bmap), st_spec],
        scratch_shapes=[pltpu.VMEM(state0.shape, F32)],
        compiler_params=_cp(("arbitrary",)),
        name="rwkv_scan",
    )(prep, prep, lora, lora, state0, *consts)
    return y_f, y_b, s_out


def _na_bias_table(rpb):
    n_head = rpb.shape[0]
    col = np.arange(GRID_W)
    col_start = np.clip(col - WIN_COLS // 2, 0, GRID_W - WIN_COLS)
    col_in = (col[None, :] >= col_start[:, None]) & (col[None, :] < col_start[:, None] + WIN_COLS)
    col_rel = np.clip(col[None, :] - col[:, None] + WIN_COLS - 1, 0, 2 * WIN_COLS - 2)
    n_rel = 2 * WIN_COLS - 1
    sel = (col_rel.reshape(-1)[None, :] == np.arange(n_rel)[:, None]).astype(np.float32)
    c = jnp.einsum('hrc,cn->hrn', rpb, jnp.asarray(sel), precision=HI)
    c = c.reshape(n_head, 2 * WIN_ROWS - 1, GRID_W, GRID_W)
    c = jnp.where(col_in[None, None], c, NEG_BIG)
    t = jnp.stack([c[:, d:d + WIN_ROWS] for d in range(WIN_ROWS)], axis=1)
    t = t.transpose(0, 1, 3, 2, 4).reshape(n_head // 2, 2, WIN_ROWS, GRID_W, WIN_ROWS * GRID_W)
    return t.astype(F32)


def _na_kernel(rows, q_ref, kp_ref, kc_ref, kn_ref, vp_ref, vc_ref, vn_ref, kx_ref, vx_ref,
               b_ref, o_ref, kcat, vcat):
    i = pl.program_id(2)
    w = GRID_W
    blk = WIN_ROWS * w
    scale = HEAD_DIM ** -0.5
    kcat[0:blk] = kp_ref[0].astype(BF16)
    kcat[blk:2 * blk] = kc_ref[0].astype(BF16)
    kcat[2 * blk:3 * blk] = kn_ref[0].astype(BF16)
    vcat[0:blk] = vp_ref[0].astype(BF16)
    vcat[blk:2 * blk] = vc_ref[0].astype(BF16)
    vcat[2 * blk:3 * blk] = vn_ref[0].astype(BF16)
    kx = kx_ref[0].astype(BF16)
    vx = vx_ref[0].astype(BF16)
    lo_mask = lax.broadcasted_iota(jnp.int32, (1, LANES), 1) < HEAD_DIM
    rng = range(WIN_ROWS)
    r0 = [jnp.clip(i * WIN_ROWS + rr - WIN_ROWS // 2, 0, rows - WIN_ROWS) for rr in rng]
    off = [pl.multiple_of((r0[rr] - (i - 1) * WIN_ROWS) * w, w) for rr in rng]
    dlt = [r0[rr] - (i * WIN_ROWS + rr) + WIN_ROWS - 1 for rr in rng]
    qs = [_stack_heads(q_ref[0, rr * w:(rr + 1) * w, :].astype(F32), lo_mask).astype(BF16)
          for rr in rng]
    s = [_bdot_nt(qs[rr], kcat[pl.ds(off[rr], blk), :]) * scale
         + jnp.concatenate([b_ref[0, 0, dlt[rr]], b_ref[0, 1, dlt[rr]]], axis=0) for rr in rng]
    sx = [_bdot_nt(q, kx) * scale for q in qs]
    mx = [jnp.maximum(jnp.max(a, axis=-1, keepdims=True), jnp.max(b, axis=-1, keepdims=True))
          for a, b in zip(s, sx)]
    p = [jnp.exp(a - m) for a, m in zip(s, mx)]
    px = [jnp.exp(a - m) for a, m in zip(sx, mx)]
    den = [jnp.sum(a, axis=-1, keepdims=True) + jnp.sum(b, axis=-1, keepdims=True)
           for a, b in zip(p, px)]
    o = [(_bdot(p[rr], vcat[pl.ds(off[rr], blk), :]) + _bdot(px[rr], vx)) / den[rr] for rr in rng]
    for rr in rng:
        o_ref[0, rr * w:(rr + 1) * w, :] = jnp.where(lo_mask, o[rr][0:w],
                                                     o[rr][w:2 * w]).astype(BF16)


def _na_attention(qkv, qkv_ctx, bias_tab):
    bsz, t_len, w3 = qkv.shape
    na = w3 // 3
    n_pair = na // LANES
    rows = t_len // GRID_W
    blk = WIN_ROWS * GRID_W
    nblk = rows // WIN_ROWS
    c_len = qkv_ctx.shape[1]
    kv = lambda sel, shift: pl.BlockSpec(
        (1, blk, LANES), lambda b, j, i: (b, jnp.clip(i + shift, 0, nblk - 1), sel * n_pair + j))
    cx = lambda sel: pl.BlockSpec((1, c_len, LANES), lambda b, j, i: (b, 0, sel * n_pair + j))
    return pl.pallas_call(
        functools.partial(_na_kernel, rows),
        out_shape=jax.ShapeDtypeStruct((bsz, t_len, na), BF16),
        grid=(bsz, n_pair, nblk),
        in_specs=[kv(0, 0), kv(1, -1), kv(1, 0), kv(1, 1), kv(2, -1), kv(2, 0), kv(2, 1),
                  cx(1), cx(2),
                  pl.BlockSpec((1, 2, WIN_ROWS, GRID_W, blk), lambda b, j, i: (j, 0, 0, 0, 0))],
        out_specs=pl.BlockSpec((1, blk, LANES), lambda b, j, i: (b, i, j)),
        scratch_shapes=[pltpu.VMEM((3 * blk, LANES), BF16), pltpu.VMEM((3 * blk, LANES), BF16)],
        compiler_params=_cp(("parallel", "parallel", "arbitrary")),
        name="na_attention",
    )(qkv, qkv, qkv, qkv, qkv, qkv, qkv, qkv_ctx, qkv_ctx, bias_tab)


def _ctx_attn_kernel(q_ref, k_ref, v_ref, o_ref):
    lo_mask = lax.broadcasted_iota(jnp.int32, (1, LANES), 1) < HEAD_DIM
    q = q_ref[0].astype(F32)
    c = q.shape[0]
    qs = _stack_heads(q, lo_mask)
    s = _bdot_nt(qs, k_ref[0]) * (HEAD_DIM ** -0.5)
    mx = jnp.max(s, axis=-1, keepdims=True)
    p = jnp.exp(s - mx)
    o = _bdot(p, v_ref[0]) / jnp.sum(p, axis=-1, keepdims=True)
    o_ref[0] = jnp.where(lo_mask, o[0:c], o[c:2 * c]).astype(BF16)


def _ctx_attention(qkv_ctx):
    bsz, c_len, w3 = qkv_ctx.shape
    na = w3 // 3
    n_pair = na // LANES
    sp = lambda sel: pl.BlockSpec((1, c_len, LANES), lambda b, j: (b, 0, sel * n_pair + j))
    return pl.pallas_call(
        _ctx_attn_kernel,
        out_shape=jax.ShapeDtypeStruct((bsz, c_len, na), BF16),
        grid=(bsz, n_pair),
        in_specs=[sp(0), sp(1), sp(2)],
        out_specs=pl.BlockSpec((1, c_len, LANES), lambda b, j: (b, 0, j)),
        compiler_params=_cp(("parallel", "parallel")),
        name="ctx_attention",
    )(qkv_ctx, qkv_ctx, qkv_ctx)


def _layer_norm(z, g, b):
    mu = jnp.mean(z, axis=-1, keepdims=True)
    d = z - mu
    var = jnp.mean(d * d, axis=-1, keepdims=True)
    return d * lax.rsqrt(var + LN_EPS) * g + b


def _outproj_kernel(alpha, n_alias, f_ref, rwf_ref, rwb_ref, at_ref, x_ref, ga_ref, sc_ref, sh_ref,
                    wfm_ref, wo_ref, g_ref, b_ref, wr_ref, br_ref, *rest):
    xo_ref, h_ref, lg_ref = rest[n_alias:]
    nf = f_ref.shape[2]
    nr = rwf_ref.shape[2]
    fm = jnp.dot(f_ref[0].astype(BF16), wfm_ref[...], preferred_element_type=F32)
    rwo = (rwf_ref[0] + rwb_ref[0]).astype(BF16)
    mix = (jnp.dot(fm.astype(BF16), wo_ref[0:nf, :], preferred_element_type=F32)
           + jnp.dot(rwo, wo_ref[nf:nf + nr, :], preferred_element_type=F32)
           + jnp.dot(at_ref[0].astype(BF16), wo_ref[nf + nr:, :], preferred_element_type=F32))
    xn = _layer_norm(alpha * x_ref[0] + ga_ref[0] * mix, g_ref[...], b_ref[...])
    xo_ref[0] = xn
    h = xn * (1.0 + sc_ref[0]) + sh_ref[0]
    h_ref[...] = h
    lg_ref[...] = _mm3(h, wr_ref[...]) + br_ref[...]


def _outproj(alpha, four, rwo, rwb, att, x, ga, sc, sh, wfm, wo, ln_g, ln_b, wr_pad, br_pad, tm,
             n_all, row0, bufs):
    g, m, d = x.shape
    nb = m // tm
    base = row0 // tm
    blk = lambda a: pl.BlockSpec((1, tm, a.shape[2]), lambda b, i: (b, i, 0))
    mod = pl.BlockSpec((1, 1, d), lambda b, i: (b, 0, 0))
    cst = lambda a: pl.BlockSpec(a.shape, lambda b, i: (0, 0))
    row = lambda a: a.reshape(1, -1)
    ln_g, ln_b, br_pad = row(ln_g), row(ln_b), row(br_pad)
    args = [four, rwo, rwb, att, x, ga, sc, sh, wfm, wo, ln_g, ln_b, wr_pad, br_pad]
    in_specs = [blk(four), blk(rwo), blk(rwb), blk(att), blk(x), mod, mod, mod, cst(wfm), cst(wo),
                cst(ln_g), cst(ln_b), cst(wr_pad), cst(br_pad)]
    aliases = {}
    if bufs is not None:
        aliases = {len(args): 1, len(args) + 1: 2}
        args += list(bufs)
        in_specs += [pl.BlockSpec(memory_space=pl.ANY)] * 2
    return pl.pallas_call(
        functools.partial(_outproj_kernel, alpha, len(aliases)),
        out_shape=[jax.ShapeDtypeStruct((g, m, d), F32), jax.ShapeDtypeStruct((n_all, d), F32),
                   jax.ShapeDtypeStruct((n_all, LANES), F32)],
        grid=(g, nb),
        in_specs=in_specs,
        out_specs=[pl.BlockSpec((1, tm, d), lambda b, i: (b, i, 0)),
                   pl.BlockSpec((tm, d), lambda b, i: (base + b * nb + i, 0)),
                   pl.BlockSpec((tm, LANES), lambda b, i: (base + b * nb + i, 0))],
        input_output_aliases=aliases,
        compiler_params=_cp(("parallel", "parallel")),
        name="outproj_ln",
    )(*args)


def _ln2_kernel(alpha, x_ref, y0_ref, y1_ref, y2_ref, y3_ref, gt_ref, ga_ref, g_ref, b_ref,
                o_ref):
    gt = gt_ref[...]
    y = (gt[:, 4:5] * y0_ref[0] + gt[:, 5:6] * y1_ref[0]
         + gt[:, 6:7] * y2_ref[0] + gt[:, 7:8] * y3_ref[0])
    o_ref[0] = _layer_norm(alpha * x_ref[0] + ga_ref[0] * y, g_ref[...], b_ref[...])


def _ln2(alpha, x, yk, gates, row0, ga, ln_g, ln_b, tm):
    g, m, d = x.shape
    nb = m // tm
    base = row0 // tm
    blk = pl.BlockSpec((1, tm, d), lambda b, i: (b, i, 0))
    vec = pl.BlockSpec((1, d), lambda b, i: (0, 0))
    ysp = lambda k: pl.BlockSpec((1, tm, d), lambda b, i: (k, base + b * nb + i, 0))
    return pl.pallas_call(
        functools.partial(_ln2_kernel, alpha),
        out_shape=jax.ShapeDtypeStruct((g, m, d), F32),
        grid=(g, nb),
        in_specs=[blk, ysp(0), ysp(1), ysp(2), ysp(3),
                  pl.BlockSpec((tm, LANES), lambda b, i: (base + b * nb + i, 0)),
                  pl.BlockSpec((1, 1, d), lambda b, i: (b, 0, 0)), vec, vec],
        out_specs=blk,
        compiler_params=_cp(("parallel", "parallel")),
        name="ln2",
    )(x, yk, yk, yk, yk, gates, ga, ln_g.reshape(1, -1), ln_b.reshape(1, -1))


def _moe_kernel(blk0, be_ref, nu_ref, x_ref, wgu_ref, bgu_ref, wdn_ref, bdn_ref, *rest):
    o_ref, wgu_s, wdn_s = rest[-3:]
    i = pl.program_id(0)
    gi = blk0 + i

    @pl.when((i == 0) | (be_ref[gi] != be_ref[jnp.maximum(gi - 1, 0)]))
    def _():
        wgu_s[...] = wgu_ref[0, 0].astype(BF16)
        wdn_s[...] = wdn_ref[0, 0].astype(BF16)

    @pl.when(gi < nu_ref[0])
    def _():
        de = wdn_s.shape[0]
        gu = (jnp.dot(x_ref[...].astype(BF16), wgu_s[...], preferred_element_type=F32)
              + bgu_ref[0, 0])
        gg = jnp.minimum(gu[:, 0:de], SWIGLU_LIMIT)
        uu = jnp.clip(gu[:, de:2 * de], -SWIGLU_LIMIT, SWIGLU_LIMIT)
        act = (uu + 1.0) * gg * _sigmoid(SWIGLU_ALPHA * gg)
        o_ref[...] = (jnp.dot(act.astype(BF16), wdn_s[...], preferred_element_type=F32)
                      + bdn_ref[0, 0])


def _moe_experts(xs, blk0, cap, ys_buf, block_e, n_used, layer, w_gu, b_gu, w_dn, b_dn, tm):
    rows, d = xs.shape
    n_layer, n_exp, _, de2 = w_gu.shape
    de = de2 // 2
    wmap = lambda i, be, nu: (layer, be[blk0 + i], 0, 0)
    in_specs = [pl.BlockSpec((tm, d), lambda i, be, nu: (i, 0)),
                pl.BlockSpec((1, 1, d, de2), wmap),
                pl.BlockSpec((1, 1, 1, de2), wmap),
                pl.BlockSpec((1, 1, de, d), wmap),
                pl.BlockSpec((1, 1, 1, d), wmap)]
    args = [block_e, n_used, xs, w_gu, b_gu.reshape(n_layer, n_exp, 1, de2), w_dn,
            b_dn.reshape(n_layer, n_exp, 1, d)]
    aliases = {}
    if ys_buf is not None:
        aliases = {len(args): 0}
        args.append(ys_buf)
        in_specs.append(pl.BlockSpec(memory_space=pl.ANY))
    grid_spec = pltpu.PrefetchScalarGridSpec(
        num_scalar_prefetch=2,
        grid=(rows // tm,),
        in_specs=in_specs,
        out_specs=pl.BlockSpec((tm, d), lambda i, be, nu: (blk0 + i, 0)),
        scratch_shapes=[pltpu.VMEM((d, de2), BF16), pltpu.VMEM((de, d), BF16)],
    )
    return pl.pallas_call(
        functools.partial(_moe_kernel, blk0),
        out_shape=jax.ShapeDtypeStruct((cap, d), F32),
        grid_spec=grid_spec,
        input_output_aliases=aliases,
        compiler_params=_cp(("arbitrary",)),
        name="moe_experts",
    )(*args)


def _router_kernel(lg_ref, tri_ref, r_ref, cnt_ref, carry):
    i = pl.program_id(0)

    @pl.when(i == 0)
    def _():
        carry[...] = jnp.zeros_like(carry)

    lane = lax.broadcasted_iota(jnp.int32, lg_ref.shape, 1)
    lanef = lane.astype(F32)
    lg = jnp.where(lane < N_EXPERTS, lg_ref[...], NEG_BIG)
    tops, idxs, hots = [], [], []
    for _ in range(TOP_K):
        m = jnp.max(lg, axis=-1, keepdims=True)
        idx = jnp.min(jnp.where(lg == m, lanef, float(LANES)), axis=-1, keepdims=True)
        hot = lanef == idx
        lg = jnp.where(hot, NEG_BIG, lg)
        tops.append(m)
        idxs.append(idx)
        hots.append(hot)
    ex = [jnp.exp(m - tops[0]) for m in tops]
    den = ex[0] + ex[1] + ex[2] + ex[3]
    oh = jnp.where(hots[0] | hots[1] | hots[2] | hots[3], 1.0, 0.0)
    before = carry[...] + jnp.dot(tri_ref[...], oh.astype(BF16), preferred_element_type=F32)
    rec = jnp.zeros(lg_ref.shape, F32)
    for k in range(TOP_K):
        rank = jnp.sum(jnp.where(hots[k], before, 0.0), axis=-1, keepdims=True)
        rec = jnp.where(lane == k, idxs[k], rec)
        rec = jnp.where(lane == TOP_K + k, ex[k] / den, rec)
        rec = jnp.where(lane == 2 * TOP_K + k, rank, rec)
    r_ref[...] = rec
    carry[...] = carry[...] + jnp.sum(oh, axis=0, keepdims=True)
    cnt_ref[...] = carry[...]


def _route(logits):
    n_tok = logits.shape[0]
    tb = min(ROUTE_TB, n_tok)
    assert n_tok % tb == 0
    tri =jnp.asarray(np.tril(np.ones((tb, tb), np.float32), -1), BF16)
    return pl.pallas_call(
        _router_kernel,
        out_shape=[jax.ShapeDtypeStruct((n_tok, LANES), F32), jax.ShapeDtypeStruct((1, LANES), F32)],
        grid=(n_tok // tb,),
        in_specs=[pl.BlockSpec((tb, LANES), lambda i: (i, 0)), pl.BlockSpec((tb, tb), lambda i: (0, 0))],
        out_specs=[pl.BlockSpec((tb, LANES), lambda i: (i, 0)), pl.BlockSpec((1, LANES), lambda i: (0, 0))],
        scratch_shapes=[pltpu.VMEM((1, LANES), F32)],
        compiler_params=_cp(("arbitrary",)),
        name="router",
    )(logits, tri)


def _moe(h, logits, layer, wgu, bgu, wdn, bdn, tm):
    n_tok, d = h.shape
    n_slot = n_tok * TOP_K
    rec, cnt = _route(logits)
    top_e = rec[:, 0:TOP_K].astype(jnp.int32)
    rank = rec[:, 2 * TOP_K:3 * TOP_K].astype(jnp.int32)
    counts = cnt[0, 0:N_EXPERTS].astype(jnp.int32)
    first = jnp.cumsum(counts) - counts
    padded = (counts + tm - 1) // tm * tm
    pad_end = jnp.cumsum(padded)
    pad_start = pad_end - padded
    n_blocks = -(-n_slot // tm) + N_EXPERTS
    cap = n_blocks * tm
    blk_row = jnp.arange(n_blocks, dtype=jnp.int32) * tm
    block_e = jnp.minimum(jnp.sum(pad_end[None, :] <= blk_row[:, None], axis=1),
                          N_EXPERTS - 1).astype(jnp.int32)
    n_used = (pad_end[-1:] // tm).astype(jnp.int32)
    experts = jnp.arange(N_EXPERTS, dtype=jnp.int32)
    pos = jnp.sum(jnp.where(top_e[..., None] == experts, pad_start, 0), axis=-1) + rank
    order = jnp.argsort(pos.reshape(-1)).astype(jnp.int32)
    row_e = jnp.repeat(block_e, tm)
    off = jnp.arange(cap, dtype=jnp.int32) - pad_start[row_e]
    src = jnp.clip(first[row_e] + off, 0, n_slot - 1)
    slot_tok = jnp.where(off < counts[row_e], order.at[src].get(mode='promise_in_bounds') // TOP_K, 0)
    blk_half = n_blocks // 2
    ys = None
    for blk0, blk1 in ((0, blk_half), (blk_half, n_blocks)):
        xs = h.at[slot_tok[blk0 * tm:blk1 * tm]].get(mode='promise_in_bounds')
        ys = _moe_experts(xs, blk0, cap, ys, block_e, n_used, layer, wgu, bgu, wdn, bdn, tm)
    yk = ys.at[pos.T.reshape(-1)].get(mode='promise_in_bounds').reshape(TOP_K, n_tok, d)
    return yk, rec


def _pad_rows(w, start, total):
    return jnp.zeros((total, w.shape[1]), w.dtype).at[start:start + w.shape[0]].set(w)


def kernel(x, c, ctx, c_ctx, w_ada, b_ada, w_in, w_vdown, w_fmap, ts_prev, ts_next, rw_w0, rw_w2, rw_a0, rw_a2, rw_g2, rw_kk, rw_ka, rw_rk, rw_gn_g, rw_gn_b, rw_v0, rw_v2, na_rpb, w_out, ln1_g, ln1_b, w_router, b_router, w_gu, b_gu, w_dn, b_dn, ln2_g, ln2_b):
    bsz, n_lat, d = x.shape
    c_len = ctx.shape[1]
    depth = w_in.shape[0]
    f_dim = w_fmap.shape[1]
    rw_dim = rw_kk.shape[1]
    alpha = (2 * depth) ** 0.25
    n_pair = rw_dim // LANES

    tm_lat = min(512, n_lat)
    tm_ctx = min(256, c_len)
    tb_ctx = min(128, c_len)
    tm_moe = 512
    n1_lat, n2_lat = _split_len(n_lat)
    n1_ctx, n2_ctx = _split_len(c_len)

    cond = jnp.zeros((SUBLANES, d), F32).at[0:bsz].set(c).at[bsz].set(c_ctx)
    ada = _ada_all(cond, w_ada, b_ada)

    xl, xc = x, ctx
    vf_l = vf_c = None
    o_r = f_dim
    o_lora = f_dim + 3 * rw_dim
    o_q = o_lora + 2 * DECAY_LORA + 2 * ICLR_LORA + 2 * GATE_LORA
    for l in range(depth):
        last = l == depth - 1
        mods = ada[l].reshape(SUBLANES, 6, d)
        lat = lambda k: mods[0:bsz, k][:, None, :]
        cx = lambda k: jnp.broadcast_to(mods[bsz, k][None, None, :], (bsz, 1, d))

        wi = w_in[l]
        vdown = jnp.zeros((d, LANES), F32)
        if l > 0:
            vdown = vdown.at[:, 0:VRES_LORA].set(w_vdown[l - 1])
        wf = wi[:, 0:f_dim].astype(BF16)
        wr = wi[:, o_r:o_lora].astype(BF16)
        wl = jnp.concatenate([wi[:, o_lora:o_q], vdown], axis=1).astype(BF16)
        wq = wi[:, o_q:].astype(BF16)
        f_l, rkv_l, lora_l, qkv_l = _inproj(xl, lat(1), lat(0), wf, wr, wl, wq, tm_lat)
        f_c, rkv_c, lora_c, qkv_c = _inproj(xc, cx(1), cx(0), wf, wr, wl, wq, tm_ctx)

        if l == 0:
            v0 = v2pad = None
        else:
            v0 = rw_v0[l - 1]
            v2pad = _pad_rows(rw_v2[l - 1], 0, LANES)
        prep_c, vf_c = _rwkv_prep(rkv_c, lora_c, ts_prev[l], ts_next[l], rw_kk[l], vf_c, v0, v2pad,
                                  tb_ctx)
        prep_l, vf_l = _rwkv_prep(rkv_l, lora_l, ts_prev[l], ts_next[l], rw_kk[l], vf_l, v0, v2pad,
                                  tm_lat)
        prm = (rw_w0[l],
               jnp.stack([_pad_rows(rw_w2[l, dr], dr * DECAY_LORA, LANES) for dr in range(2)]),
               rw_a0[l],
               jnp.stack([_pad_rows(rw_a2[l, dr], dr * ICLR_LORA, LANES) for dr in range(2)]),
               rw_g2[l], rw_ka[l], rw_rk[l].reshape(-1), rw_gn_g[l], rw_gn_b[l])
        s0 = jnp.zeros((2 * bsz * n_pair, LANES, LANES), F32)
        rwf_c, rwb_c, s_ctx = _rwkv_bidir(prep_c, lora_c, s0, prm)
        rwf_l, rwb_l, _ = _rwkv_bidir(prep_l, lora_l, s_ctx, prm)

        att_l = _na_attention(qkv_l, qkv_c, _na_bias_table(na_rpb[l]))
        four_l = _fourier(f_l, n1_lat, n2_lat)

        wfm = w_fmap[l].astype(BF16)
        wo = w_out[l].astype(BF16)
        wr_pad = jnp.zeros((d, LANES), F32).at[:, 0:N_EXPERTS].set(w_router[l])
        br_pad = jnp.zeros((LANES,), F32).at[0:N_EXPERTS].set(b_router[l])
        n_l = bsz * n_lat
        n_all = n_l if last else n_l + bsz * c_len
        xl, h_all, lg_all = _outproj(alpha, four_l, rwf_l, rwb_l, att_l, xl, lat(2), lat(4), lat(3),
                                     wfm, wo, ln1_g[l], ln1_b[l], wr_pad, br_pad, tm_lat, n_all, 0,
                                     None)
        if last:
            yk, gates = _moe(h_all, lg_all, l, w_gu, b_gu, w_dn, b_dn, tm_moe)
            xl = _ln2(alpha, xl, yk, gates, 0, lat(5), ln2_g[l], ln2_b[l], tm_lat)
        else:
            att_c = _ctx_attention(qkv_c)
            four_c = _fourier(f_c, n1_ctx, n2_ctx)
            xc, h_all, lg_all = _outproj(alpha, four_c, rwf_c, rwb_c, att_c, xc, cx(2), cx(4), cx(3),
                                         wfm, wo, ln1_g[l], ln1_b[l], wr_pad, br_pad, tm_ctx, n_all,
                                         n_l, (h_all, lg_all))
            yk, gates = _moe(h_all, lg_all, l, w_gu, b_gu, w_dn, b_dn, tm_moe)
            xl = _ln2(alpha, xl, yk, gates, 0, lat(5), ln2_g[l], ln2_b[l], tm_lat)
            xc = _ln2(alpha, xc, yk, gates, n_l, cx(5), ln2_g[l], ln2_b[l], tm_ctx)
    return xl
```

```python
import functools

import numpy as np
import jax
import jax.numpy as jnp
from jax import lax
from jax.experimental import pallas as pl
from jax.experimental.pallas import tpu as pltpu

F32 = jnp.float32
BF16 = jnp.bfloat16
HI = lax.Precision.HIGHEST

GRID_W = 64
HEAD_DIM = 64
WIN_ROWS = 8
WIN_COLS = 16
N_EXPERTS = 32
TOP_K = 4
SWIGLU_LIMIT = 7.0
SWIGLU_ALPHA = 1.702
LN_EPS = 1e-5
GN_EPS = HEAD_DIM * 1e-5
DECAY_LORA = 64
ICLR_LORA = 64
GATE_LORA = 128
VRES_LORA = 32

LANES = 128
SUBLANES = 8
VMEM_LIMIT = 56 * 1024 * 1024

CHUNK = 64
NEG_BIG = -1e30
ROUTE_TB = 512


def _cp(sem, vmem=VMEM_LIMIT):
    return pltpu.CompilerParams(dimension_semantics=sem, vmem_limit_bytes=vmem)


def _dot(a, b, prec=HI):
    return jnp.dot(a, b, precision=prec, preferred_element_type=F32)


def _bdot(a, b):
    return jnp.dot(a.astype(BF16), b.astype(BF16), preferred_element_type=F32)


def _bdot_nt(a, b):
    return lax.dot_general(a.astype(BF16), b.astype(BF16), (((1,), (1,)), ((), ())),
                           preferred_element_type=F32)


def _sigmoid(x):
    return 1.0 / (1.0 + jnp.exp(-x))


def _split2(x):
    x1 = x.astype(BF16)
    x2 = (x - x1.astype(F32)).astype(BF16)
    return x1, x2


def _lhs3(x):
    x1, x2 = _split2(x)
    return jnp.concatenate([x1, x2, x1], axis=1)


def _rhs3(y):
    y1, y2 = _split2(y)
    return jnp.concatenate([y1, y1, y2], axis=0)


def _mm3(a, b):
    return jnp.dot(_lhs3(a), _rhs3(b), preferred_element_type=F32)


def _head_sum_all(xs, ones3):
    m, width = xs[0].shape
    n_col = width // LANES
    tall = jnp.concatenate([x[:, j * LANES:(j + 1) * LANES] for x in xs for j in range(n_col)],
                           axis=0)
    x1 = tall.astype(BF16)
    r1 = tall - x1.astype(F32)
    x2 = r1.astype(BF16)
    x3 = (r1 - x2.astype(F32)).astype(BF16)
    out = jnp.dot(jnp.concatenate([x1, x2, x3], axis=1), ones3, preferred_element_type=F32)
    return [jnp.concatenate([out[(i * n_col + j) * m:(i * n_col + j + 1) * m]
                             for j in range(n_col)], axis=1) for i in range(len(xs))]


def _head_sum(x, ones3):
    return _head_sum_all([x], ones3)[0]


def _ada_kernel(c_ref, w_ref, b_ref, o_ref):
    c = c_ref[...]
    o_ref[0] = _dot(c * _sigmoid(c), w_ref[0]) + b_ref[0]


def _ada_all(cond, w_ada, b_ada):
    n_layer, d, n = w_ada.shape
    tn = 1536
    return pl.pallas_call(
        _ada_kernel,
        out_shape=jax.ShapeDtypeStruct((n_layer, SUBLANES, n), F32),
        grid=(n_layer, n // tn),
        in_specs=[pl.BlockSpec((SUBLANES, d), lambda l, j: (0, 0)),
                  pl.BlockSpec((1, d, tn), lambda l, j: (l, 0, j)),
                  pl.BlockSpec((1, 1, tn), lambda l, j: (l, 0, j))],
        out_specs=pl.BlockSpec((1, SUBLANES, tn), lambda l, j: (l, 0, j)),
        compiler_params=_cp(("arbitrary", "arbitrary")),
        name="ada",
    )(cond, w_ada, b_ada.reshape(n_layer, 1, n))


def _inproj_kernel(x_ref, sc_ref, sh_ref, wf_ref, wr_ref, wl_ref, wq_ref,
                   of_ref, or_ref, ol_ref, oq_ref):
    xm = (x_ref[0] * (1.0 + sc_ref[0]) + sh_ref[0]).astype(BF16)
    of_ref[0] = jnp.dot(xm, wf_ref[...], preferred_element_type=F32)
    or_ref[0] = jnp.dot(xm, wr_ref[...], preferred_element_type=F32)
    ol_ref[0] = jnp.dot(xm, wl_ref[...], preferred_element_type=F32)
    oq_ref[0] = jnp.dot(xm, wq_ref[...], preferred_element_type=F32).astype(BF16)


def _inproj(x, sc, sh, wf, wr, wl, wq, tm):
    g, m, d = x.shape
    ws = (wf, wr, wl, wq)
    mod = pl.BlockSpec((1, 1, d), lambda b, i: (b, 0, 0))
    return pl.pallas_call(
        _inproj_kernel,
        out_shape=[jax.ShapeDtypeStruct((g, m, w.shape[1]), dt)
                   for w, dt in zip(ws, (F32, F32, F32, BF16))],
        grid=(g, m // tm),
        in_specs=[pl.BlockSpec((1, tm, d), lambda b, i: (b, i, 0)), mod, mod]
                 + [pl.BlockSpec(w.shape, lambda b, i: (0, 0)) for w in ws],
        out_specs=[pl.BlockSpec((1, tm, w.shape[1]), lambda b, i: (b, i, 0)) for w in ws],
        compiler_params=_cp(("parallel", "parallel")),
        name="inproj",
    )(x, sc, sh, *ws)


def _dft_consts(t_len, n1, n2, tb2, width):
    groups = width // HEAD_DIM
    j = np.arange(HEAD_DIM)
    ang = 2.0 * np.pi * np.outer(j, j) / HEAD_DIM
    eye = np.eye(groups)
    cbd = np.kron(eye, np.cos(ang)) / np.sqrt(HEAD_DIM)
    sbd = np.kron(eye, np.sin(ang)) / np.sqrt(HEAD_DIM)
    a1 = 2.0 * np.pi * np.outer(np.arange(n1), np.arange(n1)) / n1
    c1, s1 = np.cos(a1), np.sin(a1)
    atw = 2.0 * np.pi * np.outer(np.arange(n1), np.arange(n2)) / t_len
    twc = np.cos(atw).reshape(n1, n2 // tb2, tb2).transpose(1, 0, 2)
    tws = np.sin(atw).reshape(n1, n2 // tb2, tb2).transpose(1, 0, 2)
    a2 = 2.0 * np.pi * np.outer(np.arange(n2), np.arange(n2)) / n2
    c2, s2 = np.cos(a2) / np.sqrt(t_len), np.sin(a2) / np.sqrt(t_len)
    return [jnp.asarray(v, F32) for v in (cbd, sbd, c1, s1, twc, tws, c2, s2)]


def _four_a_kernel(tb2, width, x_ref, cbd_ref, sbd_ref, c1_ref, s1_ref, twc_ref, tws_ref,
                   yr_ref, yi_ref):
    dot = functools.partial(jnp.dot, preferred_element_type=F32)
    c1 = _lhs3(c1_ref[...])
    s1 = _lhs3(s1_ref[...])
    cbd = _rhs3(cbd_ref[...])
    sbd = _rhs3(sbd_ref[...])
    for jj in range(tb2):
        x = _lhs3(x_ref[0, :, width * jj:width * (jj + 1)])
        zr = _rhs3(dot(x, cbd))
        zi = _rhs3(-dot(x, sbd))
        ar = dot(c1, zr) + dot(s1, zi)
        ai = dot(c1, zi) - dot(s1, zr)
        tc = twc_ref[0, :, jj:jj + 1]
        ts = tws_ref[0, :, jj:jj + 1]
        yr_ref[0, jj] = tc * ar + ts * ai
        yi_ref[0, jj] = tc * ai - ts * ar


def _four_b_kernel(yr_ref, yi_ref, c2_ref, s2_ref, o_ref):
    o_ref[0] = (_mm3(c2_ref[...], yr_ref[0]) + _mm3(s2_ref[...], yi_ref[0])).astype(BF16)


def _fourier(f, n1, n2):
    bsz, t_len, width = f.shape
    tb2 = SUBLANES
    cbd, sbd, c1, s1, twc, tws, c2, s2 = _dft_consts(t_len, n1, n2, tb2, width)
    const = lambda a: pl.BlockSpec(a.shape, lambda b, j: (0,) * a.ndim)
    yr, yi = pl.pallas_call(
        functools.partial(_four_a_kernel, tb2, width),
        out_shape=[jax.ShapeDtypeStruct((bsz, n2, n1, width), F32)] * 2,
        grid=(bsz, n2 // tb2),
        in_specs=[pl.BlockSpec((1, n1, tb2 * width), lambda b, j: (b, 0, j)),
                  const(cbd), const(sbd), const(c1), const(s1),
                  pl.BlockSpec((1, n1, tb2), lambda b, j: (j, 0, 0)),
                  pl.BlockSpec((1, n1, tb2), lambda b, j: (j, 0, 0))],
        out_specs=[pl.BlockSpec((1, tb2, n1, width), lambda b, j: (b, j, 0, 0))] * 2,
        compiler_params=_cp(("parallel", "parallel")),
        name="fourier_a",
    )(f.reshape(bsz, n1, n2 * width), cbd, sbd, c1, s1, twc, tws)
    ncol = n1 * width
    tc = min(ncol, 2048)
    out = pl.pallas_call(
        _four_b_kernel,
        out_shape=jax.ShapeDtypeStruct((bsz, n2, ncol), BF16),
        grid=(bsz, ncol // tc),
        in_specs=[pl.BlockSpec((1, n2, tc), lambda b, j: (b, 0, j)),
                  pl.BlockSpec((1, n2, tc), lambda b, j: (b, 0, j)),
                  const(c2), const(s2)],
        out_specs=pl.BlockSpec((1, n2, tc), lambda b, j: (b, 0, j)),
        compiler_params=_cp(("parallel", "parallel")),
        name="fourier_b",
    )(yr.reshape(bsz, n2, ncol), yi.reshape(bsz, n2, ncol), c2, s2)
    return out.reshape(bsz, t_len, width)


def _split_len(t_len):
    n1 = 1 << ((t_len.bit_length() - 1 + 1) // 2)
    return n1, t_len // n1


def _head_ones3():
    h = np.arange(LANES) // HEAD_DIM
    one = (h[:, None] == h[None, :]).astype(np.float32)
    return jnp.asarray(np.concatenate([one, one, one], axis=0), BF16)


def _rwkv_prep_kernel(first, x_ref, xp_ref, xn_ref, lo_ref, tsp_ref, tsn_ref, kk_ref,
                      ones_ref, *rest):
    if first:
        o_ref, vf_out_ref = rest
    else:
        v0_ref, v2_ref, vf_ref, o_ref = rest
    i = pl.program_id(1)
    last = pl.num_programs(1) - 1
    x = x_ref[0]
    tb, w3 = x.shape
    rw = w3 // 3
    row = lax.broadcasted_iota(jnp.int32, (tb, 1), 0)
    prev_edge = jnp.where(i == 0, 0.0, xp_ref[0, SUBLANES - 1:SUBLANES, :])
    next_edge = jnp.where(i == last, 0.0, xn_ref[0, 0:1, :])
    prev = jnp.where(row == 0, prev_edge, pltpu.roll(x, 1, 0))
    nxt = jnp.where(row == tb - 1, next_edge, pltpu.roll(x, tb - 1, 0))
    s = x + tsp_ref[...] * (prev - x) + tsn_ref[...] * (nxt - x)
    r = s[:, 0:rw]
    k = s[:, rw:2 * rw]
    v = s[:, 2 * rw:3 * rw]
    if first:
        vf_out_ref[0] = v
    else:
        mix = _sigmoid(v0_ref[...] + _mm3(lo_ref[0], v2_ref[...]))
        v = v + (vf_ref[0] - v) * mix
    kk = k * kk_ref[...]
    ss = _head_sum(kk * kk, ones_ref[...])
    kk = kk * lax.rsqrt(jnp.maximum(ss, 1e-24))
    o_ref[0, :, 0:rw] = r
    o_ref[0, :, rw:2 * rw] = k
    o_ref[0, :, 2 * rw:3 * rw] = v
    o_ref[0, :, 3 * rw:4 * rw] = kk


def _rwkv_prep(rkv, lora, ts_prev, ts_next, k_k, v_first, v0, v2pad, tb):
    g, m, w3 = rkv.shape
    rw = w3 // 3
    first = v_first is None
    nb8 = tb // SUBLANES
    n8 = m // SUBLANES
    ones3 = _head_ones3()
    row = lambda a: a.reshape(1, -1)
    vec = lambda n: pl.BlockSpec((1, n), lambda b, i: (0, 0))
    in_specs = [
        pl.BlockSpec((1, tb, w3), lambda b, i: (b, i, 0)),
        pl.BlockSpec((1, SUBLANES, w3), lambda b, i: (b, jnp.maximum(i * nb8 - 1, 0), 0)),
        pl.BlockSpec((1, SUBLANES, w3), lambda b, i: (b, jnp.minimum((i + 1) * nb8, n8 - 1), 0)),
        pl.BlockSpec((1, tb, LANES), lambda b, i: (b, i, 4)),
        vec(w3), vec(w3), vec(rw),
        pl.BlockSpec(ones3.shape, lambda b, i: (0, 0)),
    ]
    args = [rkv, rkv, rkv, lora, row(ts_prev), row(ts_next), row(k_k), ones3]
    out_main = jax.ShapeDtypeStruct((g, m, 4 * rw), F32)
    spec_main = pl.BlockSpec((1, tb, 4 * rw), lambda b, i: (b, i, 0))
    spec_v = pl.BlockSpec((1, tb, rw), lambda b, i: (b, i, 0))
    if first:
        out_shape = [out_main, jax.ShapeDtypeStruct((g, m, rw), F32)]
        out_specs = [spec_main, spec_v]
    else:
        in_specs += [vec(rw), pl.BlockSpec((LANES, rw), lambda b, i: (0, 0)), spec_v]
        args += [row(v0), v2pad, v_first]
        out_shape = out_main
        out_specs = spec_main
    res = pl.pallas_call(
        functools.partial(_rwkv_prep_kernel, first),
        out_shape=out_shape, grid=(g, m // tb), in_specs=in_specs, out_specs=out_specs,
        compiler_params=_cp(("parallel", "parallel")),
        name="rwkv_prep",
    )(*args)
    if first:
        return res[0], res[1]
    return res, v_first


def _stack_heads(x, lo_mask):
    return jnp.concatenate([jnp.where(lo_mask, x, 0.0), jnp.where(lo_mask, 0.0, x)], axis=0)


def _scan_masks(reverse):
    L = CHUNK
    L2, L4 = 2 * L, 4 * L
    ti = lax.broadcasted_iota(jnp.int32, (L4, L4), 0)
    si = lax.broadcasted_iota(jnp.int32, (L4, L4), 1)
    tq = ti % L2
    sq = si % L2
    tt = tq % L
    st = sq % L
    before = (st > tt) if reverse else (st < tt)
    keep = ((tq // L) == (sq // L)) & (before | ((ti >= L2) & (st == tt)))
    ci = lax.broadcasted_iota(jnp.int32, (L, L), 0)
    cj = lax.broadcasted_iota(jnp.int32, (L, L), 1)
    tri = jnp.where((cj >= ci) if reverse else (cj <= ci), 1.0, 0.0).astype(BF16)
    return keep, jnp.concatenate([tri, tri, tri], axis=1)


def _rwkv_bidir_kernel(pf_ref, pb_ref, lof_ref, lob_ref, s0_ref, w0_ref, w2_ref, a0_ref, a2_ref,
                       g2_ref, ka_ref, rk_ref, gng_ref, gnb_ref, ones_ref,
                       of_ref, ob_ref, so_ref, s_scr):
    i = pl.program_id(0)
    L = CHUNK
    L2, L4 = 2 * L, 4 * L
    bsz = pf_ref.shape[0]
    rw = pf_ref.shape[2] // 4
    n_pair = rw // LANES

    @pl.when(i == 0)
    def _():
        s_scr[...] = s0_ref[...]

    ones3 = ones_ref[...]
    lo_mask = lax.broadcasted_iota(jnp.int32, (1, LANES), 1) < HEAD_DIM
    ei = lax.broadcasted_iota(jnp.int32, (L2, L2), 0)
    ej = lax.broadcasted_iota(jnp.int32, (L2, L2), 1)
    eye = jnp.where(ei == ej, 1.0, 0.0)
    masks = (_scan_masks(False), _scan_masks(True))

    streams = []
    for dr, (p_ref, lo_ref) in enumerate(((pf_ref, lof_ref), (pb_ref, lob_ref))):
        lo = jnp.concatenate([lo_ref[b] for b in range(bsz)], axis=0)
        zw = w0_ref[dr] + _bdot(jnp.tanh(lo[:, 0:LANES]), w2_ref[dr])
        log_w = -(jnp.maximum(-zw, 0.0) + jnp.log(1.0 + jnp.exp(-jnp.abs(zw)))) - 0.5
        lw_all = -jnp.exp(log_w)
        iclr_all = _sigmoid(a0_ref[dr] + _bdot(lo[:, LANES:2 * LANES], a2_ref[dr]))
        gate_all = _bdot(_sigmoid(lo[:, (2 + dr) * LANES:(3 + dr) * LANES]), g2_ref[dr])
        for b in range(bsz):
            rows = slice(b * L, (b + 1) * L)
            p = p_ref[b]
            r = p[:, 0:rw]
            k = p[:, rw:2 * rw]
            v = p[:, 2 * rw:3 * rw]
            kk = p[:, 3 * rw:4 * rw]
            iclr = iclr_all[rows]
            kd = k * (1.0 + (iclr - 1.0) * ka_ref[...])
            streams.append(dict(dr=dr, b=b, r=r, kd=kd, v=v, av=-kk, bv=kk * iclr,
                                lw=lw_all[rows], gate=gate_all[rows]))
    rk_sums = _head_sum_all([st['r'] * st['kd'] * rk_ref[...] for st in streams], ones3)
    for st, rks in zip(streams, rk_sums):
        st['bonus'] = rks * st['v']

    chains = [(si_, j) for si_ in range(len(streams)) for j in range(n_pair)]

    def cols(si_, j, name):
        return streams[si_][name][:, j * LANES:(j + 1) * LANES]

    ops = []
    for si_, j in chains:
        dr = streams[si_]['dr']
        keep, tri3 = masks[dr]
        lwc = cols(si_, j, 'lw')
        l1 = lwc.astype(BF16)
        lr = lwc - l1.astype(F32)
        l2 = lr.astype(BF16)
        l3 = (lr - l2.astype(F32)).astype(BF16)
        cum = jnp.dot(tri3, jnp.concatenate([l1, l2, l3], axis=0), preferred_element_type=F32)
        end_row = 0 if dr else L - 1
        tot = cum[end_row:end_row + 1, :]
        g_inv = jnp.exp(-cum)
        g_end = jnp.exp(tot - cum)
        bvc = cols(si_, j, 'bv')
        kdc = cols(si_, j, 'kd')
        stk = lambda x: _stack_heads(x, lo_mask).astype(BF16)
        ops.append(dict(
            keep=keep, tot=tot,
            ar_s=jnp.concatenate([stk(cols(si_, j, 'av') * jnp.exp(cum - lwc)),
                                  stk(cols(si_, j, 'r') * jnp.exp(cum))], axis=0),
            bk_s=jnp.concatenate([stk(bvc * g_inv), stk(kdc * g_inv)], axis=0),
            v_s=stk(cols(si_, j, 'v')),
            bkht_s=jnp.concatenate([_stack_heads(bvc * g_end, lo_mask).T.astype(BF16),
                                    _stack_heads(kdc * g_end, lo_mask).T.astype(BF16)], axis=1)))

    dot = functools.partial(jnp.dot, preferred_element_type=F32)
    grams = [jnp.where(o['keep'], _bdot_nt(o['ar_s'], o['bk_s']), 0.0) for o in ops]
    tinv = [eye + g[0:L2, 0:L2] for g in grams]
    pw = [g[0:L2, 0:L2].astype(BF16) for g in grams]
    pw = [dot(a, a).astype(BF16) for a in pw]
    for _ in range(4):
        both = [dot(jnp.concatenate([p, t.astype(BF16)], axis=0), p) for p, t in zip(pw, tinv)]
        pw = [b[0:L2].astype(BF16) for b in both]
        tinv = [t + b[L2:L4] for t, b in zip(tinv, both)]
    tinv = [t + dot(t.astype(BF16), p) for t, p in zip(tinv, pw)]
    x1 = [dot(g[0:L2, L2:L4].astype(BF16), o['v_s']) for g, o in zip(grams, ops)]
    wu = [dot(t.astype(BF16), jnp.concatenate([o['ar_s'][0:L2], x.astype(BF16)], axis=1))
          for t, o, x in zip(tinv, ops, x1)]
    s_old = [s_scr[c] for c in range(len(chains))]
    hy = [_bdot_nt(jnp.concatenate([w[:, 0:L2].astype(BF16), o['ar_s'][L2:L4]], axis=0), s)
          for w, o, s in zip(wu, ops, s_old)]
    u = [h[0:L2] + w[:, L2:L4] for h, w in zip(hy, wu)]
    uv = [jnp.concatenate([uu.astype(BF16), o['v_s']], axis=0) for uu, o in zip(u, ops)]
    yd = [dot(jnp.concatenate([g[L2:L4, :].astype(BF16), o['bkht_s']], axis=0), x)
          for g, o, x in zip(grams, ops, uv)]
    ys = [h[L2:L4] + b[0:L2] for h, b in zip(hy, yd)]
    for c, (s, b, o) in enumerate(zip(s_old, yd, ops)):
        s_scr[c] = s * jnp.exp(o['tot']) + b[L2:L4].T

    y_all = [jnp.concatenate([ys[si_ * n_pair + j][0:L] + ys[si_ * n_pair + j][L:L2]
                              for j in range(n_pair)], axis=1) for si_ in range(len(streams))]
    d_all = [y - mu * (1.0 / HEAD_DIM) for y, mu in zip(y_all, _head_sum_all(y_all, ones3))]
    var_all = _head_sum_all([d * d for d in d_all], ones3)
    for st, d, vs in zip(streams, d_all, var_all):
        var = vs * (1.0 / HEAD_DIM)
        out = (d * lax.rsqrt(var + GN_EPS) * gng_ref[...] + gnb_ref[...] + st['bonus']) * st['gate']
        if st['dr'] == 0:
            of_ref[st['b']] = out
        else:
            ob_ref[st['b']] = out

    @pl.when(i == pl.num_programs(0) - 1)
    def _():
        so_ref[...] = s_scr[...]


def _rwkv_bidir(prep, lora, state0, prm):
    g, m, w4 = prep.shape
    rw = w4 // 4
    nblk = m // CHUNK
    w0, w2pad, a0, a2pad, g2, k_a, r_k, gn_g, gn_b = prm
    ones3 = _head_ones3()
    row = lambda a: a.reshape(1, -1)
    fmap = lambda i: (0, i, 0)
    bmap = lambda i: (0, nblk - 1 - i, 0)
    full = lambda a: pl.BlockSpec(a.shape, lambda i: (0,) * a.ndim)
    w0, a0 = w0[:, None, :], a0[:, None, :]
    consts = [w0, w2pad, a0, a2pad, g2, row(k_a), row(r_k), row(gn_g), row(gn_b), ones3]
    st_spec = pl.BlockSpec(state0.shape, lambda i: (0, 0, 0))
    y_f, y_b, s_out = pl.pallas_call(
        _rwkv_bidir_kernel,
        out_shape=[jax.ShapeDtypeStruct((g, m, rw), F32), jax.ShapeDtypeStruct((g, m, rw), F32),
                   jax.ShapeDtypeStruct(state0.shape, F32)],
        grid=(nblk,),
        in_specs=[pl.BlockSpec((g, CHUNK, w4), fmap), pl.BlockSpec((g, CHUNK, w4), bmap),
                  pl.BlockSpec((g, CHUNK, lora.shape[2]), fmap),
                  pl.BlockSpec((g, CHUNK, lora.shape[2]), bmap),
                  st_spec] + [full(a) for a in consts],
        out_specs=[pl.BlockSpec((g, CHUNK, rw), fmap), pl.BlockSpec((g, CHUNK, rw), bmap), st_spec],
        scratch_shapes=[pltpu.VMEM(state0.shape, F32)],
        compiler_params=_cp(("arbitrary",)),
        name="rwkv_scan",
    )(prep, prep, lora, lora, state0, *consts)
    return y_f, y_b, s_out


def _na_bias_table(rpb):
    n_head = rpb.shape[0]
    col = np.arange(GRID_W)
    col_start = np.clip(col - WIN_COLS // 2, 0, GRID_W - WIN_COLS)
    col_in = (col[None, :] >= col_start[:, None]) & (col[None, :] < col_start[:, None] + WIN_COLS)
    col_rel = np.clip(col[None, :] - col[:, None] + WIN_COLS - 1, 0, 2 * WIN_COLS - 2)
    n_rel = 2 * WIN_COLS - 1
    sel = (col_rel.reshape(-1)[None, :] == np.arange(n_rel)[:, None]).astype(np.float32)
    c = jnp.einsum('hrc,cn->hrn', rpb, jnp.asarray(sel), precision=HI)
    c = c.reshape(n_head, 2 * WIN_ROWS - 1, GRID_W, GRID_W)
    c = jnp.where(col_in[None, None], c, NEG_BIG)
    t = jnp.stack([c[:, d:d + WIN_ROWS] for d in range(WIN_ROWS)], axis=1)
    t = t.transpose(0, 1, 3, 2, 4).reshape(n_head // 2, 2, WIN_ROWS, GRID_W, WIN_ROWS * GRID_W)
    return t.astype(F32)


def _na_kernel(rows, q_ref, kp_ref, kc_ref, kn_ref, vp_ref, vc_ref, vn_ref, kx_ref, vx_ref,
               b_ref, o_ref, kcat, vcat):
    i = pl.program_id(2)
    w = GRID_W
    blk = WIN_ROWS * w
    scale = HEAD_DIM ** -0.5
    kcat[0:blk] = kp_ref[0].astype(BF16)
    kcat[blk:2 * blk] = kc_ref[0].astype(BF16)
    kcat[2 * blk:3 * blk] = kn_ref[0].astype(BF16)
    vcat[0:blk] = vp_ref[0].astype(BF16)
    vcat[blk:2 * blk] = vc_ref[0].astype(BF16)
    vcat[2 * blk:3 * blk] = vn_ref[0].astype(BF16)
    kx = kx_ref[0].astype(BF16)
    vx = vx_ref[0].astype(BF16)
    lo_mask = lax.broadcasted_iota(jnp.int32, (1, LANES), 1) < HEAD_DIM
    rng = range(WIN_ROWS)
    r0 = [jnp.clip(i * WIN_ROWS + rr - WIN_ROWS // 2, 0, rows - WIN_ROWS) for rr in rng]
    off = [pl.multiple_of((r0[rr] - (i - 1) * WIN_ROWS) * w, w) for rr in rng]
    dlt = [r0[rr] - (i * WIN_ROWS + rr) + WIN_ROWS - 1 for rr in rng]
    qs = [_stack_heads(q_ref[0, rr * w:(rr + 1) * w, :].astype(F32), lo_mask).astype(BF16)
          for rr in rng]
    s = [_bdot_nt(qs[rr], kcat[pl.ds(off[rr], blk), :]) * scale
         + jnp.concatenate([b_ref[0, 0, dlt[rr]], b_ref[0, 1, dlt[rr]]], axis=0) for rr in rng]
    sx = [_bdot_nt(q, kx) * scale for q in qs]
    mx = [jnp.maximum(jnp.max(a, axis=-1, keepdims=True), jnp.max(b, axis=-1, keepdims=True))
          for a, b in zip(s, sx)]
    p = [jnp.exp(a - m) for a, m in zip(s, mx)]
    px = [jnp.exp(a - m) for a, m in zip(sx, mx)]
    den = [jnp.sum(a, axis=-1, keepdims=True) + jnp.sum(b, axis=-1, keepdims=True)
           for a, b in zip(p, px)]
    o = [(_bdot(p[rr], vcat[pl.ds(off[rr], blk), :]) + _bdot(px[rr], vx)) / den[rr] for rr in rng]
    for rr in rng:
        o_ref[0, rr * w:(rr + 1) * w, :] = jnp.where(lo_mask, o[rr][0:w],
                                                     o[rr][w:2 * w]).astype(BF16)


def _na_attention(qkv, qkv_ctx, bias_tab):
    bsz, t_len, w3 = qkv.shape
    na = w3 // 3
    n_pair = na // LANES
    rows = t_len // GRID_W
    blk = WIN_ROWS * GRID_W
    nblk = rows // WIN_ROWS
    c_len = qkv_ctx.shape[1]
    kv = lambda sel, shift: pl.BlockSpec(
        (1, blk, LANES), lambda b, j, i: (b, jnp.clip(i + shift, 0, nblk - 1), sel * n_pair + j))
    cx = lambda sel: pl.BlockSpec((1, c_len, LANES), lambda b, j, i: (b, 0, sel * n_pair + j))
    return pl.pallas_call(
        functools.partial(_na_kernel, rows),
        out_shape=jax.ShapeDtypeStruct((bsz, t_len, na), BF16),
        grid=(bsz, n_pair, nblk),
        in_specs=[kv(0, 0), kv(1, -1), kv(1, 0), kv(1, 1), kv(2, -1), kv(2, 0), kv(2, 1),
                  cx(1), cx(2),
                  pl.BlockSpec((1, 2, WIN_ROWS, GRID_W, blk), lambda b, j, i: (j, 0, 0, 0, 0))],
        out_specs=pl.BlockSpec((1, blk, LANES), lambda b, j, i: (b, i, j)),
        scratch_shapes=[pltpu.VMEM((3 * blk, LANES), BF16), pltpu.VMEM((3 * blk, LANES), BF16)],
        compiler_params=_cp(("parallel", "parallel", "arbitrary")),
        name="na_attention",
    )(qkv, qkv, qkv, qkv, qkv, qkv, qkv, qkv_ctx, qkv_ctx, bias_tab)


def _ctx_attn_kernel(q_ref, k_ref, v_ref, o_ref):
    lo_mask = lax.broadcasted_iota(jnp.int32, (1, LANES), 1) < HEAD_DIM
    q = q_ref[0].astype(F32)
    c = q.shape[0]
    qs = _stack_heads(q, lo_mask)
    s = _bdot_nt(qs, k_ref[0]) * (HEAD_DIM ** -0.5)
    mx = jnp.max(s, axis=-1, keepdims=True)
    p = jnp.exp(s - mx)
    o = _bdot(p, v_ref[0]) / jnp.sum(p, axis=-1, keepdims=True)
    o_ref[0] = jnp.where(lo_mask, o[0:c], o[c:2 * c]).astype(BF16)


def _ctx_attention(qkv_ctx):
    bsz, c_len, w3 = qkv_ctx.shape
    na = w3 // 3
    n_pair = na // LANES
    sp = lambda sel: pl.BlockSpec((1, c_len, LANES), lambda b, j: (b, 0, sel * n_pair + j))
    return pl.pallas_call(
        _ctx_attn_kernel,
        out_shape=jax.ShapeDtypeStruct((bsz, c_len, na), BF16),
        grid=(bsz, n_pair),
        in_specs=[sp(0), sp(1), sp(2)],
        out_specs=pl.BlockSpec((1, c_len, LANES), lambda b, j: (b, 0, j)),
        compiler_params=_cp(("parallel", "parallel")),
        name="ctx_attention",
    )(qkv_ctx, qkv_ctx, qkv_ctx)


def _layer_norm(z, g, b):
    mu = jnp.mean(z, axis=-1, keepdims=True)
    d = z - mu
    var = jnp.mean(d * d, axis=-1, keepdims=True)
    return d * lax.rsqrt(var + LN_EPS) * g + b


def _outproj_kernel(alpha, n_alias, f_ref, rwf_ref, rwb_ref, at_ref, x_ref, ga_ref, sc_ref, sh_ref,
                    wfm_ref, wo_ref, g_ref, b_ref, wr_ref, br_ref, *rest):
    xo_ref, h_ref, lg_ref = rest[n_alias:]
    nf = f_ref.shape[2]
    nr = rwf_ref.shape[2]
    fm = jnp.dot(f_ref[0].astype(BF16), wfm_ref[...], preferred_element_type=F32)
    rwo = (rwf_ref[0] + rwb_ref[0]).astype(BF16)
    mix = (jnp.dot(fm.astype(BF16), wo_ref[0:nf, :], preferred_element_type=F32)
           + jnp.dot(rwo, wo_ref[nf:nf + nr, :], preferred_element_type=F32)
           + jnp.dot(at_ref[0].astype(BF16), wo_ref[nf + nr:, :], preferred_element_type=F32))
    xn = _layer_norm(alpha * x_ref[0] + ga_ref[0] * mix, g_ref[...], b_ref[...])
    xo_ref[0] = xn
    h = xn * (1.0 + sc_ref[0]) + sh_ref[0]
    h_ref[...] = h
    lg_ref[...] = _mm3(h, wr_ref[...]) + br_ref[...]


def _outproj(alpha, four, rwo, rwb, att, x, ga, sc, sh, wfm, wo, ln_g, ln_b, wr_pad, br_pad, tm,
             n_all, row0, bufs):
    g, m, d = x.shape
    nb = m // tm
    base = row0 // tm
    blk = lambda a: pl.BlockSpec((1, tm, a.shape[2]), lambda b, i: (b, i, 0))
    mod = pl.BlockSpec((1, 1, d), lambda b, i: (b, 0, 0))
    cst = lambda a: pl.BlockSpec(a.shape, lambda b, i: (0, 0))
    row = lambda a: a.reshape(1, -1)
    ln_g, ln_b, br_pad = row(ln_g), row(ln_b), row(br_pad)
    args = [four, rwo, rwb, att, x, ga, sc, sh, wfm, wo, ln_g, ln_b, wr_pad, br_pad]
    in_specs = [blk(four), blk(rwo), blk(rwb), blk(att), blk(x), mod, mod, mod, cst(wfm), cst(wo),
                cst(ln_g), cst(ln_b), cst(wr_pad), cst(br_pad)]
    aliases = {}
    if bufs is not None:
        aliases = {len(args): 1, len(args) + 1: 2}
        args += list(bufs)
        in_specs += [pl.BlockSpec(memory_space=pl.ANY)] * 2
    return pl.pallas_call(
        functools.partial(_outproj_kernel, alpha, len(aliases)),
        out_shape=[jax.ShapeDtypeStruct((g, m, d), F32), jax.ShapeDtypeStruct((n_all, d), F32),
                   jax.ShapeDtypeStruct((n_all, LANES), F32)],
        grid=(g, nb),
        in_specs=in_specs,
        out_specs=[pl.BlockSpec((1, tm, d), lambda b, i: (b, i, 0)),
                   pl.BlockSpec((tm, d), lambda b, i: (base + b * nb + i, 0)),
                   pl.BlockSpec((tm, LANES), lambda b, i: (base + b * nb + i, 0))],
        input_output_aliases=aliases,
        compiler_params=_cp(("parallel", "parallel")),
        name="outproj_ln",
    )(*args)


def _ln2_kernel(alpha, x_ref, y0_ref, y1_ref, y2_ref, y3_ref, gt_ref, ga_ref, g_ref, b_ref,
                o_ref):
    gt = gt_ref[...]
    y = (gt[:, 4:5] * y0_ref[0] + gt[:, 5:6] * y1_ref[0]
         + gt[:, 6:7] * y2_ref[0] + gt[:, 7:8] * y3_ref[0])
    o_ref[0] = _layer_norm(alpha * x_ref[0] + ga_ref[0] * y, g_ref[...], b_ref[...])


def _ln2(alpha, x, yk, gates, row0, ga, ln_g, ln_b, tm):
    g, m, d = x.shape
    nb = m // tm
    base = row0 // tm
    blk = pl.BlockSpec((1, tm, d), lambda b, i: (b, i, 0))
    vec = pl.BlockSpec((1, d), lambda b, i: (0, 0))
    ysp = lambda k: pl.BlockSpec((1, tm, d), lambda b, i: (k, base + b * nb + i, 0))
    return pl.pallas_call(
        functools.partial(_ln2_kernel, alpha),
        out_shape=jax.ShapeDtypeStruct((g, m, d), F32),
        grid=(g, nb),
        in_specs=[blk, ysp(0), ysp(1), ysp(2), ysp(3),
                  pl.BlockSpec((tm, LANES), lambda b, i: (base + b * nb + i, 0)),
                  pl.BlockSpec((1, 1, d), lambda b, i: (b, 0, 0)), vec, vec],
        out_specs=blk,
        compiler_params=_cp(("parallel", "parallel")),
        name="ln2",
    )(x, yk, yk, yk, yk, gates, ga, ln_g.reshape(1, -1), ln_b.reshape(1, -1))


def _moe_kernel(be_ref, nu_ref, x_ref, wgu_ref, bgu_ref, wdn_ref, bdn_ref, o_ref, wgu_s, wdn_s):
    i = pl.program_id(0)

    @pl.when((i == 0) | (be_ref[i] != be_ref[jnp.maximum(i - 1, 0)]))
    def _():
        wgu_s[...] = wgu_ref[0, 0].astype(BF16)
        wdn_s[...] = wdn_ref[0, 0].astype(BF16)

    @pl.when(i < nu_ref[0])
    def _():
        de = wdn_s.shape[0]
        gu = (jnp.dot(x_ref[...].astype(BF16), wgu_s[...], preferred_element_type=F32)
              + bgu_ref[0, 0])
        gg = jnp.minimum(gu[:, 0:de], SWIGLU_LIMIT)
        uu = jnp.clip(gu[:, de:2 * de], -SWIGLU_LIMIT, SWIGLU_LIMIT)
        act = (uu + 1.0) * gg * _sigmoid(SWIGLU_ALPHA * gg)
        o_ref[...] = (jnp.dot(act.astype(BF16), wdn_s[...], preferred_element_type=F32)
                      + bdn_ref[0, 0])


def _moe_experts(xs, block_e, n_used, layer, w_gu, b_gu, w_dn, b_dn, tm):
    cap, d = xs.shape
    n_layer, n_exp, _, de2 = w_gu.shape
    de = de2 // 2
    n_blocks = cap // tm
    wmap = lambda i, be, nu: (layer, be[i], 0, 0)
    grid_spec = pltpu.PrefetchScalarGridSpec(
        num_scalar_prefetch=2,
        grid=(n_blocks,),
        in_specs=[pl.BlockSpec((tm, d), lambda i, be, nu: (i, 0)),
                  pl.BlockSpec((1, 1, d, de2), wmap),
                  pl.BlockSpec((1, 1, 1, de2), wmap),
                  pl.BlockSpec((1, 1, de, d), wmap),
                  pl.BlockSpec((1, 1, 1, d), wmap)],
        out_specs=pl.BlockSpec((tm, d), lambda i, be, nu: (i, 0)),
        scratch_shapes=[pltpu.VMEM((d, de2), BF16), pltpu.VMEM((de, d), BF16)],
    )
    return pl.pallas_call(
        _moe_kernel,
        out_shape=jax.ShapeDtypeStruct((cap, d), F32),
        grid_spec=grid_spec,
        compiler_params=_cp(("arbitrary",)),
        name="moe_experts",
    )(block_e, n_used, xs, w_gu, b_gu.reshape(n_layer, n_exp, 1, de2), w_dn,
      b_dn.reshape(n_layer, n_exp, 1, d))


def _router_kernel(lg_ref, tri_ref, r_ref, cnt_ref, carry):
    i = pl.program_id(0)

    @pl.when(i == 0)
    def _():
        carry[...] = jnp.zeros_like(carry)

    lane = lax.broadcasted_iota(jnp.int32, lg_ref.shape, 1)
    lanef = lane.astype(F32)
    lg = jnp.where(lane < N_EXPERTS, lg_ref[...], NEG_BIG)
    tops, idxs, hots = [], [], []
    for _ in range(TOP_K):
        m = jnp.max(lg, axis=-1, keepdims=True)
        idx = jnp.min(jnp.where(lg == m, lanef, float(LANES)), axis=-1, keepdims=True)
        hot = lanef == idx
        lg = jnp.where(hot, NEG_BIG, lg)
        tops.append(m)
        idxs.append(idx)
        hots.append(hot)
    ex = [jnp.exp(m - tops[0]) for m in tops]
    den = ex[0] + ex[1] + ex[2] + ex[3]
    oh = jnp.where(hots[0] | hots[1] | hots[2] | hots[3], 1.0, 0.0)
    before = carry[...] + jnp.dot(tri_ref[...], oh.astype(BF16), preferred_element_type=F32)
    rec = jnp.zeros(lg_ref.shape, F32)
    for k in range(TOP_K):
        rank = jnp.sum(jnp.where(hots[k], before, 0.0), axis=-1, keepdims=True)
        rec = jnp.where(lane == k, idxs[k], rec)
        rec = jnp.where(lane == TOP_K + k, ex[k] / den, rec)
        rec = jnp.where(lane == 2 * TOP_K + k, rank, rec)
    r_ref[...] = rec
    carry[...] = carry[...] + jnp.sum(oh, axis=0, keepdims=True)
    cnt_ref[...] = carry[...]


def _route(logits):
    n_tok = logits.shape[0]
    tb = min(ROUTE_TB, n_tok)
    assert n_tok % tb == 0
    tri =jnp.asarray(np.tril(np.ones((tb, tb), np.float32), -1), BF16)
    return pl.pallas_call(
        _router_kernel,
        out_shape=[jax.ShapeDtypeStruct((n_tok, LANES), F32), jax.ShapeDtypeStruct((1, LANES), F32)],
        grid=(n_tok // tb,),
        in_specs=[pl.BlockSpec((tb, LANES), lambda i: (i, 0)), pl.BlockSpec((tb, tb), lambda i: (0, 0))],
        out_specs=[pl.BlockSpec((tb, LANES), lambda i: (i, 0)), pl.BlockSpec((1, LANES), lambda i: (0, 0))],
        scratch_shapes=[pltpu.VMEM((1, LANES), F32)],
        compiler_params=_cp(("arbitrary",)),
        name="router",
    )(logits, tri)


def _moe(h, logits, layer, wgu, bgu, wdn, bdn, tm):
    n_tok, d = h.shape
    n_slot = n_tok * TOP_K
    rec, cnt = _route(logits)
    top_e = rec[:, 0:TOP_K].astype(jnp.int32)
    rank = rec[:, 2 * TOP_K:3 * TOP_K].astype(jnp.int32)
    counts = cnt[0, 0:N_EXPERTS].astype(jnp.int32)
    first = jnp.cumsum(counts) - counts
    padded = (counts + tm - 1) // tm * tm
    pad_end = jnp.cumsum(padded)
    pad_start = pad_end - padded
    n_blocks = -(-n_slot // tm) + N_EXPERTS
    cap = n_blocks * tm
    blk_row = jnp.arange(n_blocks, dtype=jnp.int32) * tm
    block_e = jnp.minimum(jnp.sum(pad_end[None, :] <= blk_row[:, None], axis=1),
                          N_EXPERTS - 1).astype(jnp.int32)
    n_used = (pad_end[-1:] // tm).astype(jnp.int32)
    experts = jnp.arange(N_EXPERTS, dtype=jnp.int32)
    pos = jnp.sum(jnp.where(top_e[..., None] == experts, pad_start, 0), axis=-1) + rank
    order = jnp.argsort(pos.reshape(-1)).astype(jnp.int32)
    row_e = jnp.repeat(block_e, tm)
    off = jnp.arange(cap, dtype=jnp.int32) - pad_start[row_e]
    src = jnp.clip(first[row_e] + off, 0, n_slot - 1)
    slot_tok = jnp.where(off < counts[row_e], order.at[src].get(mode='promise_in_bounds') // TOP_K, 0)
    xs = h.at[slot_tok].get(mode='promise_in_bounds')
    ys = _moe_experts(xs, block_e, n_used, layer, wgu, bgu, wdn, bdn, tm)
    yk = ys.at[pos.T.reshape(-1)].get(mode='promise_in_bounds').reshape(TOP_K, n_tok, d)
    return yk, rec


def _pad_rows(w, start, total):
    return jnp.zeros((total, w.shape[1]), w.dtype).at[start:start + w.shape[0]].set(w)


def kernel(x, c, ctx, c_ctx, w_ada, b_ada, w_in, w_vdown, w_fmap, ts_prev, ts_next, rw_w0, rw_w2, rw_a0, rw_a2, rw_g2, rw_kk, rw_ka, rw_rk, rw_gn_g, rw_gn_b, rw_v0, rw_v2, na_rpb, w_out, ln1_g, ln1_b, w_router, b_router, w_gu, b_gu, w_dn, b_dn, ln2_g, ln2_b):
    bsz, n_lat, d = x.shape
    c_len = ctx.shape[1]
    depth = w_in.shape[0]
    f_dim = w_fmap.shape[1]
    rw_dim = rw_kk.shape[1]
    alpha = (2 * depth) ** 0.25
    n_pair = rw_dim // LANES

    tm_lat = min(512, n_lat)
    tm_ctx = min(256, c_len)
    tb_ctx = min(128, c_len)
    tm_moe = 512
    n1_lat, n2_lat = _split_len(n_lat)
    n1_ctx, n2_ctx = _split_len(c_len)

    cond = jnp.zeros((SUBLANES, d), F32).at[0:bsz].set(c).at[bsz].set(c_ctx)
    ada = _ada_all(cond, w_ada, b_ada)

    xl, xc = x, ctx
    vf_l = vf_c = None
    o_r = f_dim
    o_lora = f_dim + 3 * rw_dim
    o_q = o_lora + 2 * DECAY_LORA + 2 * ICLR_LORA + 2 * GATE_LORA
    for l in range(depth):
        last = l == depth - 1
        mods = ada[l].reshape(SUBLANES, 6, d)
        lat = lambda k: mods[0:bsz, k][:, None, :]
        cx = lambda k: jnp.broadcast_to(mods[bsz, k][None, None, :], (bsz, 1, d))

        wi = w_in[l]
        vdown = jnp.zeros((d, LANES), F32)
        if l > 0:
            vdown = vdown.at[:, 0:VRES_LORA].set(w_vdown[l - 1])
        wf = wi[:, 0:f_dim].astype(BF16)
        wr = wi[:, o_r:o_lora].astype(BF16)
        wl = jnp.concatenate([wi[:, o_lora:o_q], vdown], axis=1).astype(BF16)
        wq = wi[:, o_q:].astype(BF16)
        f_l, rkv_l, lora_l, qkv_l = _inproj(xl, lat(1), lat(0), wf, wr, wl, wq, tm_lat)
        f_c, rkv_c, lora_c, qkv_c = _inproj(xc, cx(1), cx(0), wf, wr, wl, wq, tm_ctx)

        if l == 0:
            v0 = v2pad = None
        else:
            v0 = rw_v0[l - 1]
            v2pad = _pad_rows(rw_v2[l - 1], 0, LANES)
        prep_c, vf_c = _rwkv_prep(rkv_c, lora_c, ts_prev[l], ts_next[l], rw_kk[l], vf_c, v0, v2pad,
                                  tb_ctx)
        prep_l, vf_l = _rwkv_prep(rkv_l, lora_l, ts_prev[l], ts_next[l], rw_kk[l], vf_l, v0, v2pad,
                                  tm_lat)
        prm = (rw_w0[l],
               jnp.stack([_pad_rows(rw_w2[l, dr], dr * DECAY_LORA, LANES) for dr in range(2)]),
               rw_a0[l],
               jnp.stack([_pad_rows(rw_a2[l, dr], dr * ICLR_LORA, LANES) for dr in range(2)]),
               rw_g2[l], rw_ka[l], rw_rk[l].reshape(-1), rw_gn_g[l], rw_gn_b[l])
        s0 = jnp.zeros((2 * bsz * n_pair, LANES, LANES), F32)
        rwf_c, rwb_c, s_ctx = _rwkv_bidir(prep_c, lora_c, s0, prm)
        rwf_l, rwb_l, _ = _rwkv_bidir(prep_l, lora_l, s_ctx, prm)

        att_l = _na_attention(qkv_l, qkv_c, _na_bias_table(na_rpb[l]))
        four_l = _fourier(f_l, n1_lat, n2_lat)

        wfm = w_fmap[l].astype(BF16)
        wo = w_out[l].astype(BF16)
        wr_pad = jnp.zeros((d, LANES), F32).at[:, 0:N_EXPERTS].set(w_router[l])
        br_pad = jnp.zeros((LANES,), F32).at[0:N_EXPERTS].set(b_router[l])
        n_l = bsz * n_lat
        n_all = n_l if last else n_l + bsz * c_len
        xl, h_all, lg_all = _outproj(alpha, four_l, rwf_l, rwb_l, att_l, xl, lat(2), lat(4), lat(3),
                                     wfm, wo, ln1_g[l], ln1_b[l], wr_pad, br_pad, tm_lat, n_all, 0,
                                     None)
        if last:
            yk, gates = _moe(h_all, lg_all, l, w_gu, b_gu, w_dn, b_dn, tm_moe)
            xl = _ln2(alpha, xl, yk, gates, 0, lat(5), ln2_g[l], ln2_b[l], tm_lat)
        else:
            att_c = _ctx_attention(qkv_c)
            four_c = _fourier(f_c, n1_ctx, n2_ctx)
            xc, h_all, lg_all = _outproj(alpha, four_c, rwf_c, rwb_c, att_c, xc, cx(2), cx(4), cx(3),
                                         wfm, wo, ln1_g[l], ln1_b[l], wr_pad, br_pad, tm_ctx, n_all,
                                         n_l, (h_all, lg_all))
            yk, gates = _moe(h_all, lg_all, l, w_gu, b_gu, w_dn, b_dn, tm_moe)
            xl = _ln2(alpha, xl, yk, gates, 0, lat(5), ln2_g[l], ln2_b[l], tm_lat)
            xc = _ln2(alpha, xc, yk, gates, n_l, cx(5), ln2_g[l], ln2_b[l], tm_ctx)
    return xl
```
